```python
import jax, jax.numpy as jnp
from jax import lax
import numpy as np

D_MODEL = 1024
BATCH = 4
SEQ = 4096
DEPTH = 4
DEC_BATCH = 128
DEC_SEQ = 4
PAST_LEN = 8192
PAGE_SIZE = 128

EPS = 1e-6
NEG_BIG = -1e30
LB_FLOOR = 1e-30
N_BRANCH = 3
BRANCH_WIDTH = 512
A_CHUNK = 128
A_GROUPS = 4
A_WIDTH = BRANCH_WIDTH
A_GROUP_DIM = A_WIDTH // A_GROUPS
B_HEADS = 4
B_DK = 128
B_DV = 128
B_WIDTH = B_HEADS * B_DV
B_CHUNK = 16
C_HEADS = 8
C_NOPE = 64
C_ROPE = 32
C_V = 64
C_Q_LORA = 384
C_KV_LORA = 256
C_QBLOCK = 128
C_WIDTH = C_HEADS * C_V
ROPE_THETA = 10000.0
D_FF = 2816
CONV_W = 3
IN_SIZES = (A_WIDTH, A_WIDTH, B_HEADS * B_DK, B_HEADS * B_DK, B_WIDTH, B_WIDTH, C_Q_LORA, C_KV_LORA, C_ROPE, N_BRANCH * D_MODEL)
IN_WIDTH = sum(IN_SIZES)

kernel_name = 'hybrid_gmlp_hgrn2_mla_decoder_step'


def _rmsnorm(x, g):
    xf = x.astype(jnp.float32)
    y = xf * lax.rsqrt(jnp.mean(xf * xf, axis=-1, keepdims=True) + EPS)
    return (y * g.astype(jnp.float32)).astype(x.dtype)


def _layernorm(x, g, b):
    xf = x.astype(jnp.float32)
    mu = jnp.mean(xf, axis=-1, keepdims=True)
    var = jnp.mean(jnp.square(xf - mu), axis=-1, keepdims=True)
    y = (xf - mu) * lax.rsqrt(var + EPS)
    return (y * g.astype(jnp.float32) + b.astype(jnp.float32)).astype(x.dtype)


def _rope(x, pos):
    half = x.shape[-1] // 2
    inv = ROPE_THETA ** (-jnp.arange(half, dtype=jnp.float32) / half)
    ang = pos.astype(jnp.float32)[:, None] * inv[None, :]
    shp = (ang.shape[0],) + (1,) * (x.ndim - 3) + (half,)
    cos = jnp.cos(ang).reshape(shp)
    sin = jnp.sin(ang).reshape(shp)
    xf = x.astype(jnp.float32)
    x1, x2 = xf[..., :half], xf[..., half:]
    return jnp.concatenate([x1 * cos - x2 * sin, x2 * cos + x1 * sin], axis=-1).astype(x.dtype)


def _split_in(z):
    outs, o = [], 0
    for s in IN_SIZES:
        outs.append(z[..., o:o + s])
        o += s
    return outs


def _chunk_gmlp(u, v, w_s, b_s):
    Bsz, T, _ = v.shape
    n = -(-T // A_CHUNK)
    pad = n * A_CHUNK - T
    vp = jnp.pad(v, ((0, 0), (0, pad), (0, 0))).reshape(Bsz, n, A_CHUNK, A_GROUPS, A_GROUP_DIM)
    mask = jnp.tril(jnp.ones((A_CHUNK, A_CHUNK), dtype=bool))
    w = jnp.where(mask[None], w_s, jnp.zeros_like(w_s))
    mixed = jnp.einsum('gts,bnsgc->bntgc', w, vp) + b_s.T[None, None, :, :, None]
    mixed = mixed.reshape(Bsz, n * A_CHUNK, A_WIDTH)[:, :T]
    return u * mixed


def _hgrn2_chunked(q, k, v, logf, S0):
    Bsz, T, H, _ = q.shape
    cb = min(B_CHUNK, T)
    n = -(-T // cb)
    pad = n * cb - T

    def prep(a):
        a = jnp.pad(a.astype(jnp.float32), ((0, 0), (0, pad), (0, 0), (0, 0)))
        return a.reshape(Bsz, n, cb, H, a.shape[-1]).transpose(1, 0, 3, 2, 4)

    mask = jnp.tril(jnp.ones((cb, cb), dtype=bool))[:, :, None]

    def step(S, inp):
        qc, kc, vc, lc = inp
        A = jnp.cumsum(lc, axis=2)
        diff = A[:, :, :, None, :] - A[:, :, None, :, :]
        decay = jnp.exp(jnp.where(mask, diff, NEG_BIG))
        scores = jnp.einsum('bhtd,bhtsd,bhsd->bhts', qc, decay, kc)
        o = jnp.einsum('bhts,bhse->bhte', scores, vc) + jnp.einsum('bhtd,bhde->bhte', qc * jnp.exp(A), S)
        A_end = A[:, :, -1, :]
        S = jnp.exp(A_end)[..., None] * S + jnp.einsum('bhsd,bhse->bhde', kc * jnp.exp(A_end[:, :, None, :] - A), vc)
        return S, o

    S, o = lax.scan(step, S0.astype(jnp.float32), (prep(q), prep(k), prep(v), prep(logf)))
    o = o.transpose(1, 0, 3, 2, 4).reshape(Bsz, n * cb, H, -1)[:, :T]
    return o, S


def _mla_attend_prompt(q_lat, q_rope, lat, kr):
    Bsz, T, H, L = q_lat.shape
    n = -(-T // C_QBLOCK)
    pad = n * C_QBLOCK - T
    scale = (C_NOPE + C_ROPE) ** -0.5

    def blocks(a):
        a = jnp.pad(a, ((0, 0), (0, pad), (0, 0), (0, 0)))
        return jnp.moveaxis(a.reshape(Bsz, n, C_QBLOCK, H, a.shape[-1]), 1, 0)

    kpos = jnp.arange(T)

    def one(args):
        ql, qr, i = args
        qpos = i * C_QBLOCK + jnp.arange(C_QBLOCK)
        s = (jnp.einsum('bthc,bsc->bhts', ql, lat) + jnp.einsum('bthr,bsr->bhts', qr, kr)).astype(jnp.float32) * scale
        s = jnp.where(kpos[None, :] <= qpos[:, None], s, NEG_BIG)
        pr = jax.nn.softmax(s, axis=-1).astype(lat.dtype)
        return jnp.einsum('bhts,bsc->bthc', pr, lat)

    out = lax.map(one, (blocks(q_lat), blocks(q_rope), jnp.arange(n)))
    return jnp.moveaxis(out, 0, 1).reshape(Bsz, n * C_QBLOCK, H, L)[:, :T]


def _mla_attend_sample(q_lat, q_rope, lat_past, kr_past, lat_new, kr_new):
    T = q_lat.shape[1]
    P = lat_past.shape[1]
    scale = (C_NOPE + C_ROPE) ** -0.5
    s_past = (jnp.einsum('bthc,bsc->bhts', q_lat, lat_past) + jnp.einsum('bthr,bsr->bhts', q_rope, kr_past)).astype(jnp.float32) * scale
    s_new = (jnp.einsum('bthc,bsc->bhts', q_lat, lat_new) + jnp.einsum('bthr,bsr->bhts', q_rope, kr_new)).astype(jnp.float32) * scale
    causal = jnp.tril(jnp.ones((T, T), dtype=bool))
    s_new = jnp.where(causal, s_new, NEG_BIG)
    pr = jax.nn.softmax(jnp.concatenate([s_past, s_new], axis=-1), axis=-1).astype(lat_new.dtype)
    return jnp.einsum('bhts,bsc->bthc', pr[..., :P], lat_past) + jnp.einsum('bhts,bsc->bthc', pr[..., P:], lat_new)


def _mixer(h, pos, lb, S0, past_lat, past_kr, w_in, w_s, b_s, lnv_g, lnv_b, hgrn_g, q_a_g, kv_a_g, w_uq, w_ukv, w_branch, w_o):
    Bsz, T, _ = h.shape
    ua, va, qb, fb, ib, gb, cq, ckv, krr, gt = _split_in(h @ w_in)
    u = jax.nn.gelu(ua)
    v = _layernorm(jax.nn.gelu(va), lnv_g, lnv_b)
    yA = _chunk_gmlp(u, v, w_s, b_s)
    v_rows = v[:, ((T - 1) // A_CHUNK) * A_CHUNK:]
    q = jax.nn.silu(qb).reshape(Bsz, T, B_HEADS, B_DK)
    zf = fb.astype(jnp.float32).reshape(Bsz, T, B_HEADS, B_DK)
    lbf = lb.astype(jnp.float32).reshape(B_HEADS, B_DK)
    log_lb = jnp.log(jnp.maximum(lbf, LB_FLOOR))
    logf = jnp.logaddexp(log_lb, jnp.log1p(-lbf) + jax.nn.log_sigmoid(zf))
    k = (1.0 - lbf) * jax.nn.sigmoid(-zf)
    vb = ib.reshape(Bsz, T, B_HEADS, B_DV)
    o, S = _hgrn2_chunked(q, k, vb, logf, S0)
    o = _rmsnorm(o, hgrn_g.reshape(B_HEADS, B_DV)).astype(h.dtype)
    yB = (o * jax.nn.silu(gb.reshape(Bsz, T, B_HEADS, B_DV))).reshape(Bsz, T, B_WIDTH)
    qc = (_rmsnorm(cq, q_a_g) @ w_uq).reshape(Bsz, T, C_HEADS, C_NOPE + C_ROPE)
    q_nope = qc[..., :C_NOPE]
    q_rope = _rope(qc[..., C_NOPE:], pos)
    lat = _rmsnorm(ckv, kv_a_g)
    kr = _rope(krr, pos)
    w_kv = w_ukv.reshape(C_KV_LORA, C_HEADS, C_NOPE + C_V)
    q_lat = jnp.einsum('bthn,lhn->bthl', q_nope, w_kv[..., :C_NOPE])
    if past_lat is None:
        out_lat = _mla_attend_prompt(q_lat, q_rope, lat, kr)
    else:
        out_lat = _mla_attend_sample(q_lat, q_rope, past_lat, past_kr, lat, kr)
    yC = jnp.einsum('bthl,lhv->bthv', out_lat, w_kv[..., C_NOPE:]).reshape(Bsz, T, C_WIDTH)
    ys = jnp.stack([yA, yB, yC], axis=2)
    br = jnp.einsum('btnc,ncd->btnd', ys, w_branch)
    gates = jax.nn.sigmoid(gt.reshape(Bsz, T, N_BRANCH, D_MODEL))
    out = jnp.sum(gates * br, axis=2) @ w_o
    return out, S, v_rows, lat, kr


def _conv_ffn(h, prev, w_up, conv_w, conv_b, w_down):
    up = h @ w_up
    T = up.shape[1]
    xp = jnp.concatenate([prev.astype(up.dtype), up], axis=1)
    y = conv_b
    for j in range(CONV_W):
        y = y + conv_w[j] * xp[:, j:j + T]
    a, b = jnp.split(y, 2, axis=-1)
    return (jax.nn.silu(a) * b) @ w_down, xp[:, T:]


def _trunk(x, c, pos, lbs, S0, conv0, cache_latent, cache_krope, page_table, w):
    Bsz = x.shape[0]
    S_out, v_out, lat_out, kr_out, conv_out = [], [], [], [], []
    for l in range(DEPTH):
        mod = jax.nn.silu(c) @ w['w_ada'][l] + w['b_ada'][l]
        sh1, sc1, g1, sh2, sc2, g2 = [m[:, None, :] for m in jnp.split(mod, 6, axis=-1)]
        h = _rmsnorm(x, w['norm1_g'][l]) * (1.0 + sc1) + sh1
        if page_table is None:
            plat, pkr = None, None
        else:
            plat = cache_latent[l, page_table].reshape(Bsz, -1, C_KV_LORA)
            pkr = cache_krope[l, page_table].reshape(Bsz, -1, C_ROPE)
        out, S, v_rows, lat, kr = _mixer(h, pos, lbs[l], S0[l], plat, pkr, w['w_in'][l], w['w_s'][l], w['b_s'][l], w['lnv_g'][l], w['lnv_b'][l], w['hgrn_g'][l], w['q_a_g'][l], w['kv_a_g'][l], w['w_uq'][l], w['w_ukv'][l], w['w_branch'][l], w['w_o'][l])
        x = x + g1 * out
        h2 = _rmsnorm(x, w['norm2_g'][l]) * (1.0 + sc2) + sh2
        f, cs = _conv_ffn(h2, conv0[l], w['w_up'][l], w['conv_w'][l], w['conv_b'][l], w['w_down'][l])
        x = x + g2 * f
        S_out.append(S)
        v_out.append(v_rows)
        lat_out.append(lat)
        kr_out.append(kr)
        conv_out.append(cs)
    y = _rmsnorm(x, w['final_g'])
    return y, jnp.stack(lat_out), jnp.stack(kr_out), jnp.stack(v_out), jnp.stack(S_out), jnp.stack(conv_out)


def setup_inputs(seed: int = 0) -> dict:
    key = jax.random.key(seed)
    ks = jax.random.split(key, 40)

    def nrm(i, shape, s):
        return jax.random.normal(ks[i], shape, jnp.float32) * s

    n_pages = PAST_LEN // PAGE_SIZE
    n_phys = (DEC_BATCH * n_pages * 5) // 4
    perm = jax.random.permutation(ks[0], n_phys)
    page_table = perm[:DEC_BATCH * n_pages].reshape(DEC_BATCH, n_pages).astype(jnp.int32)
    gate_off = jnp.zeros((6 * D_MODEL,), jnp.float32).at[2 * D_MODEL:3 * D_MODEL].set(1.0).at[5 * D_MODEL:].set(1.0)
    return {
        'x_prompt': nrm(1, (BATCH, SEQ, D_MODEL), 1.0),
        'x_sample': nrm(2, (DEC_BATCH, DEC_SEQ, D_MODEL), 1.0),
        'c_prompt': nrm(3, (BATCH, D_MODEL), 1.0),
        'c_sample': nrm(4, (DEC_BATCH, D_MODEL), 1.0),
        'cache_latent': nrm(5, (DEPTH, n_phys, PAGE_SIZE, C_KV_LORA), 1.0),
        'cache_krope': nrm(6, (DEPTH, n_phys, PAGE_SIZE, C_ROPE), 1.0),
        'state_hgrn': nrm(7, (DEPTH, DEC_BATCH, B_HEADS, B_DK, B_DV), 0.5),
        'state_conv': nrm(8, (DEPTH, DEC_BATCH, CONV_W - 1, 2 * D_FF), 1.0),
        'page_table': page_table,
        'norm1_g': 1.0 + nrm(9, (DEPTH, D_MODEL), 0.02),
        'norm2_g': 1.0 + nrm(10, (DEPTH, D_MODEL), 0.02),
        'final_g': 1.0 + nrm(11, (D_MODEL,), 0.02),
        'w_ada': nrm(12, (DEPTH, D_MODEL, 6 * D_MODEL), 0.1 * D_MODEL ** -0.5),
        'b_ada': gate_off[None, :] + nrm(13, (DEPTH, 6 * D_MODEL), 0.02),
        'w_in': nrm(14, (DEPTH, D_MODEL, IN_WIDTH), D_MODEL ** -0.5),
        'w_s': nrm(15, (DEPTH, A_GROUPS, A_CHUNK, A_CHUNK), A_CHUNK ** -0.5),
        'b_s': 1.0 + nrm(16, (DEPTH, A_GROUPS, A_CHUNK), 0.02),
        'lnv_g': 1.0 + nrm(17, (DEPTH, A_WIDTH), 0.02),
        'lnv_b': nrm(18, (DEPTH, A_WIDTH), 0.02),
        'lb_logits': nrm(19, (DEPTH, B_HEADS * B_DK), 0.5),
        'hgrn_g': 1.0 + nrm(20, (DEPTH, B_WIDTH), 0.02),
        'q_a_g': 1.0 + nrm(21, (DEPTH, C_Q_LORA), 0.02),
        'kv_a_g': 1.0 + nrm(22, (DEPTH, C_KV_LORA), 0.02),
        'w_uq': nrm(23, (DEPTH, C_Q_LORA, C_HEADS * (C_NOPE + C_ROPE)), C_Q_LORA ** -0.5),
        'w_ukv': nrm(24, (DEPTH, C_KV_LORA, C_HEADS * (C_NOPE + C_V)), C_KV_LORA ** -0.5),
        'w_branch': nrm(25, (DEPTH, N_BRANCH, BRANCH_WIDTH, D_MODEL), BRANCH_WIDTH ** -0.5),
        'w_o': nrm(26, (DEPTH, D_MODEL, D_MODEL), D_MODEL ** -0.5),
        'w_up': nrm(27, (DEPTH, D_MODEL, 2 * D_FF), D_MODEL ** -0.5),
        'conv_w': nrm(28, (DEPTH, CONV_W, 2 * D_FF), CONV_W ** -0.5),
        'conv_b': nrm(29, (DEPTH, 2 * D_FF), 0.02),
        'w_down': nrm(30, (DEPTH, D_FF, D_MODEL), D_FF ** -0.5),
    }


def reference(x_prompt, x_sample, c_prompt, c_sample, cache_latent, cache_krope, state_hgrn, state_conv, page_table, norm1_g, norm2_g, final_g, w_ada, b_ada, w_in, w_s, b_s, lnv_g, lnv_b, lb_logits, hgrn_g, q_a_g, kv_a_g, w_uq, w_ukv, w_branch, w_o, w_up, conv_w, conv_b, w_down):
    w = {'norm1_g': norm1_g, 'norm2_g': norm2_g, 'final_g': final_g, 'w_ada': w_ada, 'b_ada': b_ada, 'w_in': w_in, 'w_s': w_s, 'b_s': b_s, 'lnv_g': lnv_g, 'lnv_b': lnv_b, 'hgrn_g': hgrn_g, 'q_a_g': q_a_g, 'kv_a_g': kv_a_g, 'w_uq': w_uq, 'w_ukv': w_ukv, 'w_branch': w_branch, 'w_o': w_o, 'w_up': w_up, 'conv_w': conv_w, 'conv_b': conv_b, 'w_down': w_down}
    pl = jax.nn.softmax(lb_logits.astype(jnp.float32), axis=0)
    lbs = jnp.cumsum(pl, axis=0) - pl[0:1]
    Bp, Tp, _ = x_prompt.shape
    pos_p = jnp.arange(Tp)
    S0_p = jnp.zeros((DEPTH, Bp, B_HEADS, B_DK, B_DV), jnp.float32)
    conv0_p = jnp.zeros((DEPTH, Bp, CONV_W - 1, 2 * D_FF), x_prompt.dtype)
    y_prompt, lat_p, kr_p, gv_p, hgrn_p, conv_p = _trunk(x_prompt, c_prompt, pos_p, lbs, S0_p, conv0_p, None, None, None, w)
    past = page_table.shape[1] * cache_latent.shape[2]
    pos_s = past + jnp.arange(x_sample.shape[1])
    y_sample, lat_s, kr_s, gv_s, hgrn_s, conv_s = _trunk(x_sample, c_sample, pos_s, lbs, state_hgrn, state_conv, cache_latent, cache_krope, page_table, w)
    return (y_prompt, y_sample, lat_p, kr_p, gv_p, hgrn_p, conv_p, lat_s, kr_s, gv_s, hgrn_s, conv_s)
```

```python
import functools
import math

import numpy as np
import jax
import jax.numpy as jnp
from jax import lax
from jax.experimental import pallas as pl
from jax.experimental.pallas import tpu as pltpu

F32 = jnp.float32
BF16 = jnp.bfloat16

D_MODEL = 1024
DEPTH = 4
PAGE = 128
EPS = 1e-6
NEG_BIG = -1e30
LB_FLOOR = 1e-30
BW = 512
A_CHUNK = 128
A_GROUPS = 4
HG_HEADS = 4
HG_D = 128
C_HEADS = 8
C_NOPE = 64
C_ROPE = 32
C_V = 64
C_QL = 384
C_KVL = 256
QK_W = 384
D_FF = 2816
ROPE_THETA = 10000.0
ATT_SCALE = (C_NOPE + C_ROPE) ** -0.5

O_UA, O_VA, O_B, O_GT, O_CQ, O_CKV, O_KRA, O_KRB, W_IN_COLS = 0, 512, 1024, 3072, 6144, 6528, 6784, 6912, 7040

VMEM_LIMIT_BYTES = 58 * 1024 * 1024
LANE = 128
FF_COL = 256
N_FF_CHUNKS = D_FF // FF_COL


def _cparams(*sem):
    return pltpu.CompilerParams(dimension_semantics=sem, vmem_limit_bytes=VMEM_LIMIT_BYTES)


def _resident(shape):
    nd = len(shape)
    return pl.BlockSpec(shape, lambda *_: (0,) * nd, pipeline_mode=pl.Buffered(1))


def _bf(x):
    return x.astype(BF16)


def _dot(a, b):
    return jnp.dot(a, b, preferred_element_type=F32)


def _dot_nt(a, b):
    return lax.dot_general(a, b, (((1,), (1,)), ((), ())), preferred_element_type=F32)


def _dot_tn(a, b):
    return lax.dot_general(a, b, (((0,), (0,)), ((), ())), preferred_element_type=F32)


def _sigmoid(x):
    return 1.0 / (1.0 + jnp.exp(-x))


def _silu(x):
    return x * _sigmoid(x)


def _gelu_tanh(x):
    return 0.5 * x * (1.0 + jnp.tanh(math.sqrt(2.0 / math.pi) * (x + 0.044715 * (x * x * x))))


def _softplus(x):
    return jnp.maximum(x, 0.0) + jnp.log1p(jnp.exp(-jnp.abs(x)))


def _logaddexp(a, b):
    return jnp.maximum(a, b) + jnp.log1p(jnp.exp(-jnp.abs(a - b)))


def _rms(x, g):
    return x * lax.rsqrt(jnp.mean(x * x, axis=-1, keepdims=True) + EPS) * g


def _norm_mod(x, g, sc, sh):
    return _rms(x, g) * (1.0 + sc) + sh


def _split3(x):
    hi = _bf(x)
    r1 = x - hi.astype(F32)
    mid = _bf(r1)
    lo = _bf(r1 - mid.astype(F32))
    return hi, mid, lo


def _ada_kernel(c_ref, w_ref, b_ref, o_ref):
    c = c_ref[...]
    o_ref[0] = _dot(_bf(_silu(c)), _bf(w_ref[0])) + b_ref[0]


def _ada(c_all, w_ada, b_ada):
    rows = c_all.shape[0]
    tn = 1536
    return pl.pallas_call(
        _ada_kernel,
        grid=(DEPTH, 6 * D_MODEL // tn),
        in_specs=[
            pl.BlockSpec((rows, D_MODEL), lambda l, j: (0, 0)),
            pl.BlockSpec((1, D_MODEL, tn), lambda l, j: (l, 0, j)),
            pl.BlockSpec((1, 1, tn), lambda l, j: (l, 0, j)),
        ],
        out_specs=pl.BlockSpec((1, rows, tn), lambda l, j: (l, 0, j)),
        out_shape=jax.ShapeDtypeStruct((DEPTH, rows, 6 * D_MODEL), F32),
        compiler_params=_cparams("arbitrary", "arbitrary"),
        name="ada",
    )(c_all, w_ada, b_ada.reshape(DEPTH, 1, 6 * D_MODEL))


def _in_proj_kernel(x_ref, sc_ref, sh_ref, g_ref, w_ref, ws_ref, bs_ref, lng_ref, lnb_ref, qag_ref, kvg_ref,
                    wqn_ref, wqa_ref, wqb_ref, wk_ref, cos_ref, sin_ref,
                    ya_ref, gv_ref, zb_ref, gt_ref, q_ref, k_ref, lat_ref, kr_ref, *, tm, gv_rows, sample):
    hb = _bf(_norm_mod(x_ref[...], g_ref[...], sc_ref[0], sh_ref[0]))

    u = _gelu_tanh(_dot(hb, w_ref[:, O_UA:O_UA + BW]))
    va = _gelu_tanh(_dot(hb, w_ref[:, O_VA:O_VA + BW]))
    mu = jnp.mean(va, axis=-1, keepdims=True)
    var = jnp.mean(jnp.square(va - mu), axis=-1, keepdims=True)
    v = (va - mu) * lax.rsqrt(var + EPS) * lng_ref[...] + lnb_ref[...]
    if sample:
        v3 = v.reshape(tm // 8, 8, BW)
        mixed = v3 * ws_ref[0][None] + bs_ref[...][None]
        for j in range(1, 4):
            vj = pltpu.roll(v, j, 0).reshape(tm // 8, 8, BW)
            mixed = mixed + vj * ws_ref[j][None]
        ya_ref[...] = _bf(u * mixed.reshape(tm, BW))
    else:
        row = lax.broadcasted_iota(jnp.int32, (A_CHUNK, A_CHUNK), 0)
        col = lax.broadcasted_iota(jnp.int32, (A_CHUNK, A_CHUNK), 1)
        wc = [_bf(jnp.where(col <= row, ws_ref[g], 0.0)) for g in range(A_GROUPS)]
        for c in range(tm // A_CHUNK):
            rs = slice(c * A_CHUNK, (c + 1) * A_CHUNK)
            for g in range(A_GROUPS):
                cs = slice(g * LANE, (g + 1) * LANE)
                mixed = _dot(wc[g], _bf(v[rs, cs])) + bs_ref[:, cs]
                ya_ref[rs, cs] = _bf(u[rs, cs] * mixed)

    @pl.when(pl.program_id(1) == pl.num_programs(1) - 1)
    def _():
        gv_ref[0] = v[tm - gv_rows:, :]

    zb_ref[...] = _dot(hb, w_ref[:, O_B:O_B + 4 * BW])
    gt_ref[...] = _sigmoid(_dot(hb, w_ref[:, O_GT:O_GT + 3 * D_MODEL]))

    cos = cos_ref[...]
    sin = sin_ref[...]
    qn = _bf(_rms(_dot(hb, w_ref[:, O_CQ:O_CQ + C_QL]), qag_ref[...]))
    for h in range(C_HEADS):
        hs = slice(h * LANE, (h + 1) * LANE)
        q_nope = _bf(_dot(qn, wqn_ref[:, hs]))
        q_lat = _dot(q_nope, wk_ref[h])
        q_rope = _dot(qn, wqa_ref[:, hs]) * cos + _dot(qn, wqb_ref[:, hs]) * sin
        q_ref[:, h * QK_W:h * QK_W + C_KVL] = _bf(q_lat * ATT_SCALE)
        q_ref[:, h * QK_W + C_KVL:(h + 1) * QK_W] = _bf(q_rope * ATT_SCALE)
    lat = _rms(_dot(hb, w_ref[:, O_CKV:O_CKV + C_KVL]), kvg_ref[...])
    kr = _dot(hb, w_ref[:, O_KRA:O_KRA + LANE]) * cos + _dot(hb, w_ref[:, O_KRB:O_KRB + LANE]) * sin
    lat_ref[...] = lat
    kr_ref[...] = kr[:, :C_ROPE]
    k_ref[:, :C_KVL] = _bf(lat)
    k_ref[:, C_KVL:] = _bf(kr)


def _in_proj(x, sc, sh, g, w, ws, bs, lng, lnb, qag, kvg, wqn, wqa, wqb, wk, cos, sin, *, nb, tm, gv_rows, sample):
    n = x.shape[0]
    nt = n // (nb * tm)
    mod_rows = sc.shape[1]
    tok = lambda b, i: (b * nt + i, 0)
    kern = functools.partial(_in_proj_kernel, tm=tm, gv_rows=gv_rows, sample=sample)
    return pl.pallas_call(
        kern,
        grid=(nb, nt),
        in_specs=[
            pl.BlockSpec((tm, D_MODEL), tok),
            pl.BlockSpec((1, mod_rows, D_MODEL), lambda b, i: (b, 0, 0)),
            pl.BlockSpec((1, mod_rows, D_MODEL), lambda b, i: (b, 0, 0)),
            _resident((1, D_MODEL)),
            _resident((D_MODEL, W_IN_COLS)),
            _resident(ws.shape),
            _resident(bs.shape),
            _resident((1, BW)),
            _resident((1, BW)),
            _resident((1, C_QL)),
            _resident((1, C_KVL)),
            _resident((C_QL, C_HEADS * LANE)),
            _resident((C_QL, C_HEADS * LANE)),
            _resident((C_QL, C_HEADS * LANE)),
            _resident((C_HEADS, LANE, C_KVL)),
            pl.BlockSpec((tm, LANE), lambda b, i: (i, 0)),
            pl.BlockSpec((tm, LANE), lambda b, i: (i, 0)),
        ],
        out_specs=[
            pl.BlockSpec((tm, BW), tok),
            pl.BlockSpec((1, gv_rows, BW), lambda b, i: (b, 0, 0)),
            pl.BlockSpec((tm, 4 * BW), tok),
            pl.BlockSpec((tm, 3 * D_MODEL), tok),
            pl.BlockSpec((tm, C_HEADS * QK_W), tok),
            pl.BlockSpec((tm, QK_W), tok),
            pl.BlockSpec((tm, C_KVL), tok),
            pl.BlockSpec((tm, C_ROPE), tok),
        ],
        out_shape=[
            jax.ShapeDtypeStruct((n, BW), BF16),
            jax.ShapeDtypeStruct((nb, gv_rows, BW), F32),
            jax.ShapeDtypeStruct((n, 4 * BW), F32),
            jax.ShapeDtypeStruct((n, 3 * D_MODEL), F32),
            jax.ShapeDtypeStruct((n, C_HEADS * QK_W), BF16),
            jax.ShapeDtypeStruct((n, QK_W), BF16),
            jax.ShapeDtypeStruct((n, C_KVL), F32),
            jax.ShapeDtypeStruct((n, C_ROPE), F32),
        ],
        compiler_params=_cparams("arbitrary", "arbitrary"),
        name="in_proj_s" if sample else "in_proj_p",
    )(x, sc, sh, g, w, ws, bs, lng, lnb, qag, kvg, wqn, wqa, wqb, wk, cos, sin)


def _hgrn_consts(chunk, seg):
    t = np.arange(chunk)
    u = t[None, :]
    tt = t[:, None]
    same_seg = (t // seg)[:, None] == (t // seg)[None, :]
    blocks = [same_seg & (u <= tt), same_seg & (u > tt)]
    masks = [np.eye(chunk, dtype=bool)]
    b = seg // 2
    while b >= 1:
        pos = t % (2 * b)
        p = t - pos
        m = (p + b - 1)[:, None]
        second = pos >= b
        blocks.append(np.where(second[:, None], (u > m) & (u <= tt), (u > tt) & (u <= m)))
        masks.append((p[:, None] == p[None, :]) & second[:, None] & (~second)[None, :])
        b //= 2
    sums = np.concatenate(blocks, 0).astype(np.float32)
    return jnp.asarray(sums, BF16), jnp.asarray(np.stack(masks).astype(np.float32))


def _hgrn_lb(lbl_ref, layer):
    logits = lbl_ref[...]
    e = jnp.exp(logits - jnp.max(logits, axis=0, keepdims=True))
    p = e / jnp.sum(e, axis=0, keepdims=True)
    lb = jnp.zeros((1, HG_HEADS * HG_D), F32)
    for i in range(1, layer + 1):
        lb = lb + p[i:i + 1, :]
    return lb


def _hgrn_chunk(zb, lb, hg, sums_ref, masks_ref, sel, states, row_masks, chunk):
    nlev = masks_ref.shape[0] - 1
    q = _silu(zb[:, 0:BW])
    zf = zb[:, BW:2 * BW]
    vb = zb[:, 2 * BW:3 * BW]
    gate = _silu(zb[:, 3 * BW:4 * BW])
    log_lb = jnp.log(jnp.maximum(lb, LB_FLOOR))
    logf = _logaddexp(log_lb, jnp.log1p(-lb) - _softplus(-zf))
    k = (1.0 - lb) * _sigmoid(-zf)

    parts = _split3(logf)
    sums = sums_ref[...]
    x = _dot(sums, parts[0]) + _dot(sums, parts[1]) + _dot(sums, parts[2])
    e = jnp.exp(x)
    e_cum = e[0:chunk]
    e_end = e[chunk:2 * chunk]
    nseq = len(states)

    outs = []
    new_states = [[None] * HG_HEADS for _ in range(nseq)]
    for h in range(HG_HEADS):
        hs = slice(h * HG_D, (h + 1) * HG_D)
        qh, kh, vh = q[:, hs], k[:, hs], _bf(vb[:, hs])
        scores = masks_ref[0] * _dot_nt(_bf(qh), _bf(kh))
        for lev in range(nlev):
            el = e[(2 + lev) * chunk:(3 + lev) * chunk, hs]
            scores = scores + masks_ref[1 + lev] * _dot_nt(_bf(qh * el), _bf(kh * el))
        o = _dot(_bf(scores), vh)
        q_in = _bf(qh * e_cum[:, hs])
        k_out = kh * e_end[:, hs]
        decay = jnp.exp(_dot_tn(parts[0][:, hs], sel) + _dot_tn(parts[1][:, hs], sel) + _dot_tn(parts[2][:, hs], sel))
        for j in range(nseq):
            s_prev = states[j][h]
            if nseq == 1:
                o = o + _dot(q_in, _bf(s_prev))
                kj = _bf(k_out)
            else:
                o = o + row_masks[j] * _dot(q_in, _bf(s_prev))
                kj = _bf(k_out * row_masks[j])
            new_states[j][h] = decay[:, j * HG_D:(j + 1) * HG_D] * s_prev + _dot_tn(kj, vh)
        y = o * lax.rsqrt(jnp.mean(o * o, axis=-1, keepdims=True) + EPS) * hg[:, hs]
        outs.append(_bf(y * gate[:, hs]))
    return outs, new_states


def _hgrn_prompt_kernel(z_ref, lbl_ref, hg_ref, sums_ref, masks_ref, sel_ref, s0_ref, y_ref, sout_ref, state, *,
                        layer, tc, chunk):
    @pl.when(pl.program_id(1) == 0)
    def _():
        state[...] = s0_ref[0]

    lb = _hgrn_lb(lbl_ref, layer)
    hg = hg_ref[...]
    sel = sel_ref[...]
    for c in range(tc // chunk):
        rs = slice(c * chunk, (c + 1) * chunk)
        states = [[state[h] for h in range(HG_HEADS)]]
        outs, new_states = _hgrn_chunk(z_ref[rs, :], lb, hg, sums_ref, masks_ref, sel, states, None, chunk)
        for h in range(HG_HEADS):
            y_ref[rs, h * HG_D:(h + 1) * HG_D] = outs[h]
            state[h] = new_states[0][h]

    @pl.when(pl.program_id(1) == pl.num_programs(1) - 1)
    def _():
        sout_ref[0] = state[...]


def _hgrn_prompt(zb, lb_logits, hgrn_g, s0, *, layer, nb, tc, chunk):
    n = zb.shape[0]
    nt = n // (nb * tc)
    sums, masks = _hgrn_consts(chunk, chunk)
    sel = jnp.ones((chunk, HG_D), BF16)
    tok = lambda b, i: (b * nt + i, 0)
    kern = functools.partial(_hgrn_prompt_kernel, layer=layer, tc=tc, chunk=chunk)
    return pl.pallas_call(
        kern,
        grid=(nb, nt),
        in_specs=[
            pl.BlockSpec((tc, 4 * BW), tok),
            _resident(lb_logits.shape),
            _resident((1, BW)),
            _resident(sums.shape),
            _resident(masks.shape),
            _resident(sel.shape),
            pl.BlockSpec((1, HG_HEADS, HG_D, HG_D), lambda b, i: (b, 0, 0, 0)),
        ],
        out_specs=[
            pl.BlockSpec((tc, BW), tok),
            pl.BlockSpec((1, HG_HEADS, HG_D, HG_D), lambda b, i: (b, 0, 0, 0)),
        ],
        out_shape=[
            jax.ShapeDtypeStruct((n, BW), BF16),
            jax.ShapeDtypeStruct((nb, HG_HEADS, HG_D, HG_D), F32),
        ],
        scratch_shapes=[pltpu.VMEM((HG_HEADS, HG_D, HG_D), F32)],
        compiler_params=_cparams("arbitrary", "arbitrary"),
        name="hgrn_p",
    )(zb, lb_logits, hgrn_g, sums, masks, sel, s0)


def _hgrn_sample_kernel(z_ref, lbl_ref, hg_ref, sums_ref, masks_ref, sel_ref, s0_ref, y_ref, sout_ref, *,
                        layer, rows, seq):
    nseq = rows // seq
    lb = _hgrn_lb(lbl_ref, layer)
    ridx = lax.broadcasted_iota(jnp.int32, (rows, 1), 0)
    row_masks = [jnp.where((ridx >= j * seq) & (ridx < (j + 1) * seq), 1.0, 0.0) for j in range(nseq)]
    states = [[s0_ref[j, h] for h in range(HG_HEADS)] for j in range(nseq)]
    outs, new_states = _hgrn_chunk(z_ref[...], lb, hg_ref[...], sums_ref, masks_ref, sel_ref[...], states,
                                   row_masks, rows)
    for h in range(HG_HEADS):
        y_ref[:, h * HG_D:(h + 1) * HG_D] = outs[h]
        for j in range(nseq):
            sout_ref[j, h] = new_states[j][h]


def _hgrn_sample(zb, lb_logits, hgrn_g, s0, *, layer, seq, rows):
    n = zb.shape[0]
    nseq = rows // seq
    sums, masks = _hgrn_consts(rows, seq)
    sel = np.zeros((rows, nseq * HG_D), np.float32)
    for j in range(nseq):
        sel[j * seq:(j + 1) * seq, j * HG_D:(j + 1) * HG_D] = 1.0
    sel = jnp.asarray(sel, BF16)
    kern = functools.partial(_hgrn_sample_kernel, layer=layer, rows=rows, seq=seq)
    return pl.pallas_call(
        kern,
        grid=(n // rows,),
        in_specs=[
            pl.BlockSpec((rows, 4 * BW), lambda i: (i, 0)),
            _resident(lb_logits.shape),
            _resident((1, BW)),
            _resident(sums.shape),
            _resident(masks.shape),
            _resident(sel.shape),
            pl.BlockSpec((nseq, HG_HEADS, HG_D, HG_D), lambda i: (i, 0, 0, 0)),
        ],
        out_specs=[
            pl.BlockSpec((rows, BW), lambda i: (i, 0)),
            pl.BlockSpec((nseq, HG_HEADS, HG_D, HG_D), lambda i: (i, 0, 0, 0)),
        ],
        out_shape=[
            jax.ShapeDtypeStruct((n, BW), BF16),
            jax.ShapeDtypeStruct(s0.shape, F32),
        ],
        compiler_params=_cparams("arbitrary"),
        name="hgrn_s",
    )(zb, lb_logits, hgrn_g, sums, masks, sel, s0)


def _attn_prompt_kernel(q_ref, k_ref, o_ref, m_sc, l_sc, acc_sc, *, tq, tk):
    i = pl.program_id(1)
    j = pl.program_id(2)
    last_j = (i * tq + tq - 1) // tk

    @pl.when(j == 0)
    def _():
        m_sc[...] = jnp.full(m_sc.shape, NEG_BIG, F32)
        l_sc[...] = jnp.zeros(l_sc.shape, F32)
        acc_sc[...] = jnp.zeros(acc_sc.shape, F32)

    @pl.when(j <= last_j)
    def _():
        kk = k_ref[...]
        vv = kk[:, :C_KVL]
        qpos = i * tq + lax.broadcasted_iota(jnp.int32, (tq, tk), 0)
        kpos = j * tk + lax.broadcasted_iota(jnp.int32, (tq, tk), 1)
        keep = kpos <= qpos
        for h in range(C_HEADS):
            s = _dot_nt(q_ref[:, h * QK_W:(h + 1) * QK_W], kk)
            s = jnp.where(keep, s, NEG_BIG)
            m_prev = m_sc[h]
            m_new = jnp.maximum(m_prev, jnp.max(s, axis=-1, keepdims=True))
            alpha = jnp.exp(m_prev - m_new)
            p = jnp.exp(s - m_new[:, :1])
            l_sc[h] = alpha * l_sc[h] + jnp.sum(p, axis=-1, keepdims=True)
            acc_sc[h] = alpha[:, :1] * acc_sc[h] + _dot(_bf(p), vv)
            m_sc[h] = m_new

    @pl.when(j == last_j)
    def _():
        for h in range(C_HEADS):
            o_ref[:, h * C_KVL:(h + 1) * C_KVL] = _bf(acc_sc[h] / l_sc[h][:, :1])


def _attn_prompt(q, k, *, nb, t, tq, tk):
    n = q.shape[0]
    nq, nk = t // tq, t // tk
    kern = functools.partial(_attn_prompt_kernel, tq=tq, tk=tk)

    def k_map(b, i, j):
        return (b * nk + jnp.minimum(j, (i * tq + tq - 1) // tk), 0)

    return pl.pallas_call(
        kern,
        grid=(nb, nq, nk),
        in_specs=[
            pl.BlockSpec((tq, C_HEADS * QK_W), lambda b, i, j: (b * nq + i, 0)),
            pl.BlockSpec((tk, QK_W), k_map),
        ],
        out_specs=pl.BlockSpec((tq, C_HEADS * C_KVL), lambda b, i, j: (b * nq + i, 0)),
        out_shape=jax.ShapeDtypeStruct((n, C_HEADS * C_KVL), BF16),
        scratch_shapes=[
            pltpu.VMEM((C_HEADS, tq, LANE), F32),
            pltpu.VMEM((C_HEADS, tq, LANE), F32),
            pltpu.VMEM((C_HEADS, tq, C_KVL), F32),
        ],
        compiler_params=_cparams("arbitrary", "arbitrary", "arbitrary"),
        name="attn_p",
    )(q, k)


def _attn_sample_kernel(pt_ref, q_ref, kn_ref, *refs, pages, seq):
    lat_refs = refs[:pages]
    kr_refs = refs[pages:2 * pages]
    o_ref = refs[2 * pages]
    m_sc, l_sc, acc_sc = refs[2 * pages + 1:]
    j = pl.program_id(1)
    rows = C_HEADS * seq

    @pl.when(j == 0)
    def _():
        m_sc[...] = jnp.full(m_sc.shape, NEG_BIG, F32)
        l_sc[...] = jnp.zeros(l_sc.shape, F32)
        acc_sc[...] = jnp.zeros(acc_sc.shape, F32)

    q = q_ref[0]
    q_lat = q[:, :C_KVL]
    q_rope = q[:, C_KVL:C_KVL + C_ROPE]
    lats = [_bf(r[...]) for r in lat_refs]
    s = jnp.concatenate(
        [_dot_nt(q_lat, lats[p]) + _dot_nt(q_rope, _bf(kr_refs[p][...])) for p in range(pages)], axis=-1)
    m_prev = m_sc[...]
    m_new = jnp.maximum(m_prev, jnp.max(s, axis=-1, keepdims=True))
    alpha = jnp.exp(m_prev - m_new)
    p = jnp.exp(s - m_new[:, :1])
    l_new = alpha * l_sc[...] + jnp.sum(p, axis=-1, keepdims=True)
    pb = _bf(p)
    acc = alpha[:, :1] * acc_sc[...]
    for pg in range(pages):
        acc = acc + _dot(pb[:, pg * PAGE:(pg + 1) * PAGE], lats[pg])
    m_sc[...] = m_new
    l_sc[...] = l_new
    acc_sc[...] = acc

    @pl.when(j == pl.num_programs(1) - 1)
    def _():
        qf = q.astype(F32)
        kn = kn_ref[0].astype(F32)
        tpos = lax.broadcasted_iota(jnp.int32, (rows, 1), 0) % seq
        sn = [jnp.where(tpos >= t, jnp.sum(qf * kn[t:t + 1, :], axis=-1, keepdims=True), NEG_BIG)
              for t in range(seq)]
        m1 = m_sc[...][:, :1]
        m2 = m1
        for t in range(seq):
            m2 = jnp.maximum(m2, sn[t])
        a2 = jnp.exp(m1 - m2)
        l2 = a2 * l_sc[...][:, :1]
        acc2 = a2 * acc_sc[...]
        for t in range(seq):
            pt = jnp.exp(sn[t] - m2)
            l2 = l2 + pt
            acc2 = acc2 + pt * kn[t:t + 1, :C_KVL]
        o_ref[0] = _bf(acc2 / l2)


def _attn_sample(page_table, q, k_new, cache_latent, cache_krope, *, layer, pages, seq):
    nb, n_pages = page_table.shape
    rows = C_HEADS * seq
    kern = functools.partial(_attn_sample_kernel, pages=pages, seq=seq)

    def page_map(p):
        return lambda b, j, pt: (layer, pt[b, j * pages + p], 0, 0)

    grid_spec = pltpu.PrefetchScalarGridSpec(
        num_scalar_prefetch=1,
        grid=(nb, n_pages // pages),
        in_specs=(
            [pl.BlockSpec((1, rows, QK_W), lambda b, j, pt: (b, 0, 0)),
             pl.BlockSpec((1, k_new.shape[1], QK_W), lambda b, j, pt: (b, 0, 0))]
            + [pl.BlockSpec((None, None, PAGE, C_KVL), page_map(p)) for p in range(pages)]
            + [pl.BlockSpec((None, None, PAGE, C_ROPE), page_map(p)) for p in range(pages)]
        ),
        out_specs=pl.BlockSpec((1, rows, C_KVL), lambda b, j, pt: (b, 0, 0)),
        scratch_shapes=[
            pltpu.VMEM((rows, LANE), F32),
            pltpu.VMEM((rows, LANE), F32),
            pltpu.VMEM((rows, C_KVL), F32),
        ],
    )
    return pl.pallas_call(
        kern,
        grid_spec=grid_spec,
        out_shape=jax.ShapeDtypeStruct((nb, rows, C_KVL), BF16),
        compiler_params=_cparams("arbitrary", "arbitrary"),
        name="attn_s",
    )(page_table, q, k_new, *([cache_latent] * pages), *([cache_krope] * pages))


def _merge_kernel(ya_ref, yb_ref, oc_ref, gt_ref, x_ref, g1_ref, wb_ref, wv_ref, wo_ref, o_ref):
    yc = _bf(_dot(oc_ref[...], wv_ref[...]))
    m = gt_ref[:, 0:D_MODEL] * _dot(ya_ref[...], wb_ref[0])
    m = m + gt_ref[:, D_MODEL:2 * D_MODEL] * _dot(yb_ref[...], wb_ref[1])
    m = m + gt_ref[:, 2 * D_MODEL:3 * D_MODEL] * _dot(yc, wb_ref[2])
    o_ref[...] = x_ref[...] + g1_ref[0] * _dot(_bf(m), wo_ref[...])


def _merge(ya, yb, oc, gt, x, g1, wb, wv, wo, *, nb, tm):
    n = x.shape[0]
    nt = n // (nb * tm)
    mod_rows = g1.shape[1]
    tok = lambda b, i: (b * nt + i, 0)
    return pl.pallas_call(
        _merge_kernel,
        grid=(nb, nt),
        in_specs=[
            pl.BlockSpec((tm, BW), tok),
            pl.BlockSpec((tm, BW), tok),
            pl.BlockSpec((tm, C_HEADS * C_KVL), tok),
            pl.BlockSpec((tm, 3 * D_MODEL), tok),
            pl.BlockSpec((tm, D_MODEL), tok),
            pl.BlockSpec((1, mod_rows, D_MODEL), lambda b, i: (b, 0, 0)),
            _resident(wb.shape),
            _resident(wv.shape),
            _resident(wo.shape),
        ],
        out_specs=pl.BlockSpec((tm, D_MODEL), tok),
        out_shape=jax.ShapeDtypeStruct((n, D_MODEL), F32),
        compiler_params=_cparams("arbitrary", "arbitrary"),
        name="merge",
    )(ya, yb, oc, gt, x, g1, wb, wv, wo)


def _conv_gate(up_a, up_b, prev_a, prev_b, cw_a, cw_b, cb_a, cb_b, keep1, keep2):
    def conv(up, prev, cw, cb):
        s1 = jnp.where(keep1, pltpu.roll(up, 1, 0), prev[0])
        s2 = jnp.where(keep2, pltpu.roll(up, 2, 0), prev[1])
        return cb + cw[0:1, :] * s2 + cw[1:2, :] * s1 + cw[2:3, :] * up

    return _bf(_silu(conv(up_a, prev_a, cw_a, cb_a)) * conv(up_b, prev_b, cw_b, cb_b))


def _ffn_up_prompt_kernel(x_ref, sc_ref, sh_ref, g_ref, w_ref, cw_ref, cb_ref, c0_ref, act_ref, cs_ref, carry, *, tm):
    @pl.when(pl.program_id(1) == 0)
    def _():
        carry[...] = c0_ref[0]

    hb = _bf(_norm_mod(x_ref[...], g_ref[...], sc_ref[0], sh_ref[0]))
    ridx = lax.broadcasted_iota(jnp.int32, (tm, 1), 0)
    keep1 = ridx >= 1
    keep2 = ridx >= 2
    for c in range(N_FF_CHUNKS):
        ca = slice(c * FF_COL, (c + 1) * FF_COL)
        cb = slice(D_FF + c * FF_COL, D_FF + (c + 1) * FF_COL)
        up_a = _dot(hb, w_ref[:, ca])
        up_b = _dot(hb, w_ref[:, cb])
        old_a = carry[:, ca]
        old_b = carry[:, cb]
        prev_a = (old_a[7:8, :], jnp.where(ridx == 0, old_a[6:7, :], old_a[7:8, :]))
        prev_b = (old_b[7:8, :], jnp.where(ridx == 0, old_b[6:7, :], old_b[7:8, :]))
        act_ref[:, ca] = _conv_gate(up_a, up_b, prev_a, prev_b, cw_ref[:, ca], cw_ref[:, cb],
                                    cb_ref[:, ca], cb_ref[:, cb], keep1, keep2)
        carry[:, ca] = up_a[tm - 8:, :]
        carry[:, cb] = up_b[tm - 8:, :]

    @pl.when(pl.program_id(1) == pl.num_programs(1) - 1)
    def _():
        cs_ref[0] = carry[...]


def _ffn_up_prompt(x, sc, sh, g, w_up, conv_w, conv_b, conv0, *, nb, tm):
    n = x.shape[0]
    nt = n // (nb * tm)
    tok = lambda b, i: (b * nt + i, 0)
    kern = functools.partial(_ffn_up_prompt_kernel, tm=tm)
    return pl.pallas_call(
        kern,
        grid=(nb, nt),
        in_specs=[
            pl.BlockSpec((tm, D_MODEL), tok),
            pl.BlockSpec((1, 1, D_MODEL), lambda b, i: (b, 0, 0)),
            pl.BlockSpec((1, 1, D_MODEL), lambda b, i: (b, 0, 0)),
            _resident((1, D_MODEL)),
            _resident((D_MODEL, 2 * D_FF)),
            _resident((3, 2 * D_FF)),
            _resident((1, 2 * D_FF)),
            pl.BlockSpec((1, 8, 2 * D_FF), lambda b, i: (b, 0, 0)),
        ],
        out_specs=[
            pl.BlockSpec((tm, D_FF), tok),
            pl.BlockSpec((1, 8, 2 * D_FF), lambda b, i: (b, 0, 0)),
        ],
        out_shape=[
            jax.ShapeDtypeStruct((n, D_FF), BF16),
            jax.ShapeDtypeStruct((nb, 8, 2 * D_FF), F32),
        ],
        scratch_shapes=[pltpu.VMEM((8, 2 * D_FF), F32)],
        compiler_params=_cparams("arbitrary", "arbitrary"),
        name="ffn_up_p",
    )(x, sc, sh, g, w_up, conv_w, conv_b, conv0)


def _ffn_up_sample_kernel(x_ref, sc_ref, sh_ref, g_ref, wa_ref, wb_ref, cwa_ref, cwb_ref, cba_ref, cbb_ref,
                          pa_ref, pb_ref, act_ref, upa_ref, upb_ref, *, n, seq):
    hb = _bf(_norm_mod(x_ref[...], g_ref[...], sc_ref[0], sh_ref[0]))
    tpos = lax.broadcasted_iota(jnp.int32, (n, 1), 0) % seq
    up_a = _dot(hb, wa_ref[...])
    up_b = _dot(hb, wb_ref[...])
    upa_ref[...] = up_a
    upb_ref[...] = up_b
    act_ref[...] = _conv_gate(up_a, up_b, (pa_ref[0], pa_ref[1]), (pb_ref[0], pb_ref[1]), cwa_ref[...], cwb_ref[...],
                              cba_ref[...], cbb_ref[...], tpos >= 1, tpos >= 2)


def _ffn_up_sample(x, sc, sh, g, w_up, conv_w, conv_b, prev, *, seq):
    n = x.shape[0]
    col_a = lambda j: (0, j)
    col_b = lambda j: (0, N_FF_CHUNKS + j)
    kern = functools.partial(_ffn_up_sample_kernel, n=n, seq=seq)
    return pl.pallas_call(
        kern,
        grid=(N_FF_CHUNKS,),
        in_specs=[
            _resident((n, D_MODEL)),
            _resident((1, n, D_MODEL)),
            _resident((1, n, D_MODEL)),
            _resident((1, D_MODEL)),
            pl.BlockSpec((D_MODEL, FF_COL), col_a),
            pl.BlockSpec((D_MODEL, FF_COL), col_b),
            pl.BlockSpec((3, FF_COL), col_a),
            pl.BlockSpec((3, FF_COL), col_b),
            pl.BlockSpec((1, FF_COL), col_a),
            pl.BlockSpec((1, FF_COL), col_b),
            pl.BlockSpec((2, n, FF_COL), lambda j: (0, 0, j)),
            pl.BlockSpec((2, n, FF_COL), lambda j: (0, 0, N_FF_CHUNKS + j)),
        ],
        out_specs=[
            pl.BlockSpec((n, FF_COL), col_a),
            pl.BlockSpec((n, FF_COL), col_a),
            pl.BlockSpec((n, FF_COL), col_a),
        ],
        out_shape=[
            jax.ShapeDtypeStruct((n, D_FF), BF16),
            jax.ShapeDtypeStruct((n, D_FF), F32),
            jax.ShapeDtypeStruct((n, D_FF), F32),
        ],
        compiler_params=_cparams("arbitrary"),
        name="ffn_up_s",
    )(x, sc, sh, g, w_up, w_up, conv_w, conv_w, conv_b, conv_b, prev, prev)


def _ffn_down_kernel(act_ref, x_ref, g2_ref, w_ref, fg_ref, o_ref, *, final):
    xn = x_ref[...] + g2_ref[0] * _dot(act_ref[...], w_ref[...])
    o_ref[...] = _rms(xn, fg_ref[...]) if final else xn


def _ffn_down(act, x, g2, w_down, final_g, *, nb, tm, final):
    n = x.shape[0]
    nt = n // (nb * tm)
    mod_rows = g2.shape[1]
    tok = lambda b, i: (b * nt + i, 0)
    return pl.pallas_call(
        functools.partial(_ffn_down_kernel, final=final),
        grid=(nb, nt),
        in_specs=[
            pl.BlockSpec((tm, D_FF), tok),
            pl.BlockSpec((tm, D_MODEL), tok),
            pl.BlockSpec((1, mod_rows, D_MODEL), lambda b, i: (b, 0, 0)),
            _resident((D_FF, D_MODEL)),
            _resident((1, D_MODEL)),
        ],
        out_specs=pl.BlockSpec((tm, D_MODEL), tok),
        out_shape=jax.ShapeDtypeStruct((n, D_MODEL), F32),
        compiler_params=_cparams("arbitrary", "arbitrary"),
        name="ffn_down",
    )(act, x, g2, w_down, final_g)


def _rope_tables(pos):
    half = C_ROPE // 2
    inv = ROPE_THETA ** (-jnp.arange(half, dtype=F32) / half)
    ang = pos.astype(F32)[:, None] * inv[None, :]
    cos, sin = jnp.cos(ang), jnp.sin(ang)
    pad = jnp.zeros((pos.shape[0], LANE - C_ROPE), F32)
    return jnp.concatenate([cos, cos, pad], -1), jnp.concatenate([-sin, sin, pad], -1)


def _swap_halves(w):
    half = C_ROPE // 2
    return jnp.concatenate([w[..., half:], w[..., :half]], axis=-1)


def _layer_weights(l, w_in, w_uq, w_ukv, w_branch, w_o, w_up, w_down):
    wi = w_in[l]
    o = 0
    segs = []
    for s in (BW, BW, BW, BW, BW, BW, C_QL, C_KVL, C_ROPE, 3 * D_MODEL):
        segs.append(wi[:, o:o + s])
        o += s
    ua, va, qb, fb, ib, gb, cq, ckv, krr, gt = segs
    zpad = jnp.zeros((D_MODEL, LANE - C_ROPE), F32)
    w_in_l = _bf(jnp.concatenate([ua, va, qb, fb, ib, gb, gt, cq, ckv, krr, zpad, _swap_halves(krr), zpad], axis=1))

    wq = w_uq[l].reshape(C_QL, C_HEADS, C_NOPE + C_ROPE)
    q_nope, q_rope = wq[..., :C_NOPE], wq[..., C_NOPE:]
    pad_n = jnp.zeros((C_QL, C_HEADS, LANE - C_NOPE), F32)
    pad_r = jnp.zeros((C_QL, C_HEADS, LANE - C_ROPE), F32)
    wqn = _bf(jnp.concatenate([q_nope, pad_n], -1).reshape(C_QL, C_HEADS * LANE))
    wqa = _bf(jnp.concatenate([q_rope, pad_r], -1).reshape(C_QL, C_HEADS * LANE))
    wqb = _bf(jnp.concatenate([_swap_halves(q_rope), pad_r], -1).reshape(C_QL, C_HEADS * LANE))

    wkv = w_ukv[l].reshape(C_KVL, C_HEADS, C_NOPE + C_V)
    wk = jnp.transpose(wkv[..., :C_NOPE], (1, 2, 0))
    wk = _bf(jnp.concatenate([wk, jnp.zeros((C_HEADS, LANE - C_NOPE, C_KVL), F32)], axis=1))
    wv = jnp.transpose(wkv[..., C_NOPE:], (1, 0, 2))
    eye = jnp.eye(C_HEADS, dtype=F32)
    wv_bd = _bf((wv[:, :, None, :] * eye[:, None, :, None]).reshape(C_HEADS * C_KVL, C_HEADS * C_V))
    return dict(w_in=w_in_l, wqn=wqn, wqa=wqa, wqb=wqb, wk=wk, wv=wv_bd, wb=_bf(w_branch[l]), wo=_bf(w_o[l]),
                w_up=_bf(w_up[l]), w_down=_bf(w_down[l]))


def _sample_mix_tables(w_s_l, b_s_l, seq):
    t = np.arange(8) % seq
    coef = []
    for j in range(seq):
        src = t - j
        wj = w_s_l[:, t, np.maximum(src, 0)]
        wj = jnp.where(jnp.asarray(src >= 0)[None, :], wj, 0.0)
        coef.append(jnp.repeat(wj.T, LANE, axis=1))
    bias = jnp.repeat(b_s_l[:, t].T, LANE, axis=1)
    return jnp.stack(coef), bias


def _trunk(x, mods, lw, *, sample, nb, t, pos, s0, conv0, cache_latent, cache_krope, page_table,
           w_s, b_s, lnv_g, lnv_b, lb_logits, hgrn_g, q_a_g, kv_a_g, norm1_g, norm2_g, final_g, conv_w, conv_b):
    n = nb * t
    cos, sin = _rope_tables(pos)
    lat_out, kr_out, gv_out, s_out, conv_out = [], [], [], [], []
    for l in range(DEPTH):
        sh1, sc1, g1, sh2, sc2, g2 = mods[l]
        w = lw[l]
        row = lambda a: a[l].reshape(1, -1)
        if sample:
            ws, bs = _sample_mix_tables(w_s[l], b_s[l], t)
            kw = dict(nb=1, tm=n, gv_rows=n, sample=True)
        else:
            ws, bs = w_s[l], jnp.repeat(b_s[l].T, LANE, axis=1)
            kw = dict(nb=nb, tm=256, gv_rows=A_CHUNK, sample=False)
        ya, gv, zb, gt, q, k, lat, kr = _in_proj(
            x, sc1, sh1, row(norm1_g), w["w_in"], ws, bs, row(lnv_g), row(lnv_b), row(q_a_g), row(kv_a_g),
            w["wqn"], w["wqa"], w["wqb"], w["wk"], cos, sin, **kw)
        if sample:
            yb, s_new = _hgrn_sample(zb, lb_logits, row(hgrn_g), s0[l], layer=l, seq=t, rows=64)
            q_b = q.reshape(nb, t, C_HEADS, QK_W).transpose(0, 2, 1, 3).reshape(nb, C_HEADS * t, QK_W)
            oc = _attn_sample(page_table, q_b, k.reshape(nb, t, QK_W), cache_latent, cache_krope,
                              layer=l, pages=16, seq=t)
            oc = oc.reshape(nb, C_HEADS, t, C_KVL).transpose(0, 2, 1, 3).reshape(n, C_HEADS * C_KVL)
            mkw = dict(nb=1, tm=n)
        else:
            yb, s_new = _hgrn_prompt(zb, lb_logits, row(hgrn_g), s0[l], layer=l, nb=nb, tc=256, chunk=128)
            oc = _attn_prompt(q, k, nb=nb, t=t, tq=256, tk=512)
            mkw = dict(nb=nb, tm=512)
        x = _merge(ya, yb, oc, gt, x, g1, w["wb"], w["wv"], w["wo"], **mkw)
        if sample:
            c0 = conv0[l]
            zero = jnp.zeros((nb, t - 1, 2 * D_FF), F32)
            prev1 = jnp.concatenate([c0[:, 1:2], zero], axis=1)
            prev2 = jnp.concatenate([c0[:, 0:2], zero[:, 1:]], axis=1)
            prev = jnp.stack([prev1.reshape(n, 2 * D_FF), prev2.reshape(n, 2 * D_FF)])
            act, up_a, up_b = _ffn_up_sample(x, sc2, sh2, row(norm2_g), w["w_up"], conv_w[l], row(conv_b), prev, seq=t)
            up = jnp.concatenate([up_a, up_b], axis=-1).reshape(nb, t, 2 * D_FF)
            conv_out.append(up[:, t - 2:, :])
        else:
            c0 = jnp.concatenate([jnp.zeros((nb, 6, 2 * D_FF), F32), conv0[l]], axis=1)
            act, cs = _ffn_up_prompt(x, sc2, sh2, row(norm2_g), w["w_up"], conv_w[l], row(conv_b), c0, nb=nb, tm=512)
            conv_out.append(cs[:, 6:, :])
        x = _ffn_down(act, x, g2, w["w_down"], final_g.reshape(1, -1), final=(l == DEPTH - 1), **mkw)
        lat_out.append(lat.reshape(nb, t, C_KVL))
        kr_out.append(kr.reshape(nb, t, C_ROPE))
        gv_out.append(gv.reshape(nb, -1, BW))
        s_out.append(s_new)
    return (x.reshape(nb, t, D_MODEL), jnp.stack(lat_out), jnp.stack(kr_out), jnp.stack(gv_out), jnp.stack(s_out),
            jnp.stack(conv_out))


def kernel(x_prompt, x_sample, c_prompt, c_sample, cache_latent, cache_krope, state_hgrn, state_conv, page_table,
           norm1_g, norm2_g, final_g, w_ada, b_ada, w_in, w_s, b_s, lnv_g, lnv_b, lb_logits, hgrn_g, q_a_g, kv_a_g,
           w_uq, w_ukv, w_branch, w_o, w_up, conv_w, conv_b, w_down):
    bp, tp, _ = x_prompt.shape
    bs_, ts, _ = x_sample.shape
    past = page_table.shape[1] * cache_latent.shape[2]

    rows = bp + bs_
    rows_pad = -(-rows // 8) * 8
    c_all = jnp.concatenate([c_prompt, c_sample, jnp.zeros((rows_pad - rows, D_MODEL), F32)], axis=0)
    mod = _ada(c_all, w_ada, b_ada)
    mods_p, mods_s = [], []
    for l in range(DEPTH):
        chunks = [mod[l, :, i * D_MODEL:(i + 1) * D_MODEL] for i in range(6)]
        mods_p.append([m[:bp].reshape(bp, 1, D_MODEL) for m in chunks])
        mods_s.append([jnp.repeat(m[bp:rows], ts, axis=0).reshape(1, bs_ * ts, D_MODEL) for m in chunks])

    lw = [_layer_weights(l, w_in, w_uq, w_ukv, w_branch, w_o, w_up, w_down) for l in range(DEPTH)]
    shared = dict(w_s=w_s, b_s=b_s, lnv_g=lnv_g, lnv_b=lnv_b, lb_logits=lb_logits, hgrn_g=hgrn_g, q_a_g=q_a_g,
                  kv_a_g=kv_a_g, norm1_g=norm1_g, norm2_g=norm2_g, final_g=final_g, conv_w=conv_w, conv_b=conv_b)

    s0_p = jnp.zeros((DEPTH, bp, HG_HEADS, HG_D, HG_D), F32)
    conv0_p = jnp.zeros((DEPTH, bp, 2, 2 * D_FF), F32)
    out_p = _trunk(x_prompt.reshape(bp * tp, D_MODEL), mods_p, lw, sample=False, nb=bp, t=tp, pos=jnp.arange(tp),
                   s0=s0_p, conv0=conv0_p, cache_latent=None, cache_krope=None, page_table=None, **shared)
    pos_s = jnp.tile(past + jnp.arange(ts), bs_)
    out_s = _trunk(x_sample.reshape(bs_ * ts, D_MODEL), mods_s, lw, sample=True, nb=bs_, t=ts, pos=pos_s,
                   s0=state_hgrn, conv0=state_conv, cache_latent=cache_latent, cache_krope=cache_krope,
                   page_table=page_table, **shared)
    y_p, lat_p, kr_p, gv_p, hgrn_p, conv_p = out_p
    y_s, lat_s, kr_s, gv_s, hgrn_s, conv_s = out_s
    return (y_p, y_s, lat_p, kr_p, gv_p, hgrn_p, conv_p, lat_s, kr_s, gv_s, hgrn_s, conv_s)
```

```python
import functools
import math

import numpy as np
import jax
import jax.numpy as jnp
from jax import lax
from jax.experimental import pallas as pl
from jax.experimental.pallas import tpu as pltpu

F32 = jnp.float32
BF16 = jnp.bfloat16

D_MODEL = 1024
DEPTH = 4
PAGE = 128
EPS = 1e-6
NEG_BIG = -1e30
LB_FLOOR = 1e-30
BW = 512
A_CHUNK = 128
A_GROUPS = 4
HG_HEADS = 4
HG_D = 128
C_HEADS = 8
C_NOPE = 64
C_ROPE = 32
C_V = 64
C_QL = 384
C_KVL = 256
QK_W = 384
D_FF = 2816
ROPE_THETA = 10000.0
ATT_SCALE = (C_NOPE + C_ROPE) ** -0.5
LOG2E = math.log2(math.e)
ROPE_LANE = C_NOPE

O_UA, O_VA, O_B, O_GT, O_CQ, O_CKV, O_KRA, O_KRB, W_IN_COLS = 0, 512, 1024, 3072, 6144, 6528, 6784, 6912, 7040

VMEM_LIMIT_BYTES = 58 * 1024 * 1024
LANE = 128
FF_COL = 256
N_FF_CHUNKS = D_FF // FF_COL


def _cparams(*sem):
    return pltpu.CompilerParams(dimension_semantics=sem, vmem_limit_bytes=VMEM_LIMIT_BYTES)


def _resident(shape):
    nd = len(shape)
    return pl.BlockSpec(shape, lambda *_: (0,) * nd, pipeline_mode=pl.Buffered(1))


def _bf(x):
    return x.astype(BF16)


def _dot(a, b):
    return jnp.dot(a, b, preferred_element_type=F32)


def _dot_nt(a, b):
    return lax.dot_general(a, b, (((1,), (1,)), ((), ())), preferred_element_type=F32)


def _dot_tn(a, b):
    return lax.dot_general(a, b, (((0,), (0,)), ((), ())), preferred_element_type=F32)


def _sigmoid(x):
    return 1.0 / (1.0 + jnp.exp(-x))


def _silu(x):
    return x * _sigmoid(x)


def _gelu_tanh(x):
    return 0.5 * x * (1.0 + jnp.tanh(math.sqrt(2.0 / math.pi) * (x + 0.044715 * (x * x * x))))


def _softplus(x):
    return jnp.maximum(x, 0.0) + jnp.log1p(jnp.exp(-jnp.abs(x)))


def _logaddexp(a, b):
    return jnp.maximum(a, b) + jnp.log1p(jnp.exp(-jnp.abs(a - b)))


def _rms(x, g):
    return x * lax.rsqrt(jnp.mean(x * x, axis=-1, keepdims=True) + EPS) * g


def _norm_mod(x, g, sc, sh):
    return _rms(x, g) * (1.0 + sc) + sh


def _split3(x):
    hi = _bf(x)
    r1 = x - hi.astype(F32)
    mid = _bf(r1)
    lo = _bf(r1 - mid.astype(F32))
    return hi, mid, lo


def _ada_kernel(c_ref, w_ref, b_ref, o_ref):
    c = c_ref[...]
    o_ref[0] = _dot(_bf(_silu(c)), _bf(w_ref[0])) + b_ref[0]


def _ada(c_all, w_ada, b_ada):
    rows = c_all.shape[0]
    tn = 1536
    return pl.pallas_call(
        _ada_kernel,
        grid=(DEPTH, 6 * D_MODEL // tn),
        in_specs=[
            pl.BlockSpec((rows, D_MODEL), lambda l, j: (0, 0)),
            pl.BlockSpec((1, D_MODEL, tn), lambda l, j: (l, 0, j)),
            pl.BlockSpec((1, 1, tn), lambda l, j: (l, 0, j)),
        ],
        out_specs=pl.BlockSpec((1, rows, tn), lambda l, j: (l, 0, j)),
        out_shape=jax.ShapeDtypeStruct((DEPTH, rows, 6 * D_MODEL), F32),
        compiler_params=_cparams("arbitrary", "arbitrary"),
        name="ada",
    )(c_all, w_ada, b_ada.reshape(DEPTH, 1, 6 * D_MODEL))


def _in_proj_kernel(x_ref, sc_ref, sh_ref, g_ref, w_ref, ws_ref, bs_ref, lng_ref, lnb_ref, qag_ref, kvg_ref,
                    cos_ref, sin_ref, *refs, tm, gv_rows, sample):
    if sample:
        wqn_ref, wqa_ref, wqb_ref, wk_ref, ya_ref, gv_ref, zb_ref, gt_ref, q_ref, k_ref, lat_ref, kr_ref = refs
    else:
        wq1_ref, wqb_ref, wkn_ref, wvv_ref, ya_ref, gv_ref, zb_ref, gt_ref, q_ref, k_ref, v_ref, lat_ref, kr_ref = refs
    hb = _bf(_norm_mod(x_ref[...], g_ref[...], sc_ref[0], sh_ref[0]))

    u = _gelu_tanh(_dot(hb, w_ref[:, O_UA:O_UA + BW]))
    va = _gelu_tanh(_dot(hb, w_ref[:, O_VA:O_VA + BW]))
    mu = jnp.mean(va, axis=-1, keepdims=True)
    var = jnp.mean(jnp.square(va - mu), axis=-1, keepdims=True)
    v = (va - mu) * lax.rsqrt(var + EPS) * lng_ref[...] + lnb_ref[...]
    if sample:
        v3 = v.reshape(tm // 8, 8, BW)
        mixed = v3 * ws_ref[0][None] + bs_ref[...][None]
        for j in range(1, 4):
            vj = pltpu.roll(v, j, 0).reshape(tm // 8, 8, BW)
            mixed = mixed + vj * ws_ref[j][None]
        ya_ref[...] = _bf(u * mixed.reshape(tm, BW))
    else:
        row = lax.broadcasted_iota(jnp.int32, (A_CHUNK, A_CHUNK), 0)
        col = lax.broadcasted_iota(jnp.int32, (A_CHUNK, A_CHUNK), 1)
        wc = [_bf(jnp.where(col <= row, ws_ref[g], 0.0)) for g in range(A_GROUPS)]
        for c in range(tm // A_CHUNK):
            rs = slice(c * A_CHUNK, (c + 1) * A_CHUNK)
            for g in range(A_GROUPS):
                cs = slice(g * LANE, (g + 1) * LANE)
                mixed = _dot(wc[g], _bf(v[rs, cs])) + bs_ref[:, cs]
                ya_ref[rs, cs] = _bf(u[rs, cs] * mixed)

    @pl.when(pl.program_id(1) == pl.num_programs(1) - 1)
    def _():
        gv_ref[0] = v[tm - gv_rows:, :]

    zb_ref[...] = _dot(hb, w_ref[:, O_B:O_B + 4 * BW])
    gt_ref[...] = _sigmoid(_dot(hb, w_ref[:, O_GT:O_GT + 3 * D_MODEL]))

    cos = cos_ref[...]
    sin = sin_ref[...]
    qn = _bf(_rms(_dot(hb, w_ref[:, O_CQ:O_CQ + C_QL]), qag_ref[...]))
    lat = _rms(_dot(hb, w_ref[:, O_CKV:O_CKV + C_KVL]), kvg_ref[...])
    kr = _dot(hb, w_ref[:, O_KRA:O_KRA + LANE]) * cos + _dot(hb, w_ref[:, O_KRB:O_KRB + LANE]) * sin
    lat_ref[...] = lat
    kr_ref[...] = kr[:, ROPE_LANE:ROPE_LANE + C_ROPE]
    if sample:
        for h in range(C_HEADS):
            hs = slice(h * LANE, (h + 1) * LANE)
            q_nope = _bf(_dot(qn, wqn_ref[:, hs]))
            q_lat = _dot(q_nope, wk_ref[h])
            q_rope = _dot(qn, wqa_ref[:, hs]) * cos + _dot(qn, wqb_ref[:, hs]) * sin
            q_ref[:, h * QK_W:h * QK_W + C_KVL] = _bf(q_lat * ATT_SCALE)
            q_ref[:, h * QK_W + C_KVL:(h + 1) * QK_W] = _bf(q_rope * ATT_SCALE)
        k_ref[:, :C_KVL] = _bf(lat)
        k_ref[:, C_KVL:] = _bf(kr)
    else:
        lane = lax.broadcasted_iota(jnp.int32, (1, LANE), 1)
        cos_q = cos + jnp.where(lane < C_NOPE, 1.0, 0.0)
        latb = _bf(lat)
        for h in range(C_HEADS):
            hs = slice(h * LANE, (h + 1) * LANE)
            q = _dot(qn, wq1_ref[:, hs]) * cos_q + _dot(qn, wqb_ref[:, hs]) * sin
            q_ref[:, hs] = _bf(q * (ATT_SCALE * LOG2E))
            k_ref[:, hs] = _bf(_dot(latb, wkn_ref[:, hs]) + kr)
        v_ref[...] = _bf(_dot(latb, wvv_ref[...]))


def _in_proj(x, sc, sh, g, w, ws, bs, lng, lnb, qag, kvg, cos, sin, attn_w, *, nb, tm, gv_rows, sample):
    n = x.shape[0]
    nt = n // (nb * tm)
    mod_rows = sc.shape[1]
    tok = lambda b, i: (b * nt + i, 0)
    kern = functools.partial(_in_proj_kernel, tm=tm, gv_rows=gv_rows, sample=sample)
    if sample:
        qkv_specs = [pl.BlockSpec((tm, C_HEADS * QK_W), tok), pl.BlockSpec((tm, QK_W), tok)]
        qkv_shapes = [jax.ShapeDtypeStruct((n, C_HEADS * QK_W), BF16), jax.ShapeDtypeStruct((n, QK_W), BF16)]
    else:
        qkv_specs = [pl.BlockSpec((tm, C_HEADS * LANE), tok)] * 3
        qkv_shapes = [jax.ShapeDtypeStruct((n, C_HEADS * LANE), BF16)] * 3
    return pl.pallas_call(
        kern,
        grid=(nb, nt),
        in_specs=[
            pl.BlockSpec((tm, D_MODEL), tok),
            pl.BlockSpec((1, mod_rows, D_MODEL), lambda b, i: (b, 0, 0)),
            pl.BlockSpec((1, mod_rows, D_MODEL), lambda b, i: (b, 0, 0)),
            _resident((1, D_MODEL)),
            _resident((D_MODEL, W_IN_COLS)),
            _resident(ws.shape),
            _resident(bs.shape),
            _resident((1, BW)),
            _resident((1, BW)),
            _resident((1, C_QL)),
            _resident((1, C_KVL)),
            pl.BlockSpec((tm, LANE), lambda b, i: (i, 0)),
            pl.BlockSpec((tm, LANE), lambda b, i: (i, 0)),
        ] + [_resident(a.shape) for a in attn_w],
        out_specs=[
            pl.BlockSpec((tm, BW), tok),
            pl.BlockSpec((1, gv_rows, BW), lambda b, i: (b, 0, 0)),
            pl.BlockSpec((tm, 4 * BW), tok),
            pl.BlockSpec((tm, 3 * D_MODEL), tok),
        ] + qkv_specs + [
            pl.BlockSpec((tm, C_KVL), tok),
            pl.BlockSpec((tm, C_ROPE), tok),
        ],
        out_shape=[
            jax.ShapeDtypeStruct((n, BW), BF16),
            jax.ShapeDtypeStruct((nb, gv_rows, BW), F32),
            jax.ShapeDtypeStruct((n, 4 * BW), F32),
            jax.ShapeDtypeStruct((n, 3 * D_MODEL), F32),
        ] + qkv_shapes + [
            jax.ShapeDtypeStruct((n, C_KVL), F32),
            jax.ShapeDtypeStruct((n, C_ROPE), F32),
        ],
        compiler_params=_cparams("arbitrary", "arbitrary"),
        name="in_proj_s" if sample else "in_proj_p",
    )(x, sc, sh, g, w, ws, bs, lng, lnb, qag, kvg, cos, sin, *attn_w)


def _hgrn_consts(chunk, seg):
    t = np.arange(chunk)
    u = t[None, :]
    tt = t[:, None]
    same_seg = (t // seg)[:, None] == (t // seg)[None, :]
    blocks = [same_seg & (u <= tt), same_seg & (u > tt)]
    masks = [np.eye(chunk, dtype=bool)]
    b = seg // 2
    while b >= 1:
        pos = t % (2 * b)
        p = t - pos
        m = (p + b - 1)[:, None]
        second = pos >= b
        blocks.append(np.where(second[:, None], (u > m) & (u <= tt), (u > tt) & (u <= m)))
        masks.append((p[:, None] == p[None, :]) & second[:, None] & (~second)[None, :])
        b //= 2
    sums = np.concatenate(blocks, 0).astype(np.float32)
    return jnp.asarray(sums, BF16), jnp.asarray(np.stack(masks).astype(np.float32))


def _hgrn_lb(lbl_ref, layer):
    logits = lbl_ref[...]
    e = jnp.exp(logits - jnp.max(logits, axis=0, keepdims=True))
    p = e / jnp.sum(e, axis=0, keepdims=True)
    lb = jnp.zeros((1, HG_HEADS * HG_D), F32)
    for i in range(1, layer + 1):
        lb = lb + p[i:i + 1, :]
    return lb


def _hgrn_chunk(zb, lb, hg, sums_ref, masks_ref, sel, states, row_masks, chunk):
    nlev = masks_ref.shape[0] - 1
    q = _silu(zb[:, 0:BW])
    zf = zb[:, BW:2 * BW]
    vb = zb[:, 2 * BW:3 * BW]
    gate = _silu(zb[:, 3 * BW:4 * BW])
    log_lb = jnp.log(jnp.maximum(lb, LB_FLOOR))
    logf = _logaddexp(log_lb, jnp.log1p(-lb) - _softplus(-zf))
    k = (1.0 - lb) * _sigmoid(-zf)

    parts = _split3(logf)
    sums = sums_ref[...]
    x = _dot(sums, parts[0]) + _dot(sums, parts[1]) + _dot(sums, parts[2])
    e = jnp.exp(x)
    e_cum = e[0:chunk]
    e_end = e[chunk:2 * chunk]
    nseq = len(states)

    outs = []
    new_states = [[None] * HG_HEADS for _ in range(nseq)]
    for h in range(HG_HEADS):
        hs = slice(h * HG_D, (h + 1) * HG_D)
        qh, kh, vh = q[:, hs], k[:, hs], _bf(vb[:, hs])
        scores = masks_ref[0] * _dot_nt(_bf(qh), _bf(kh))
        for lev in range(nlev):
            el = e[(2 + lev) * chunk:(3 + lev) * chunk, hs]
            scores = scores + masks_ref[1 + lev] * _dot_nt(_bf(qh * el), _bf(kh * el))
        o = _dot(_bf(scores), vh)
        q_in = _bf(qh * e_cum[:, hs])
        k_out = kh * e_end[:, hs]
        decay = jnp.exp(_dot_tn(parts[0][:, hs], sel) + _dot_tn(parts[1][:, hs], sel) + _dot_tn(parts[2][:, hs], sel))
        for j in range(nseq):
            s_prev = states[j][h]
            if nseq == 1:
                o = o + _dot(q_in, _bf(s_prev))
                kj = _bf(k_out)
            else:
                o = o + row_masks[j] * _dot(q_in, _bf(s_prev))
                kj = _bf(k_out * row_masks[j])
            new_states[j][h] = decay[:, j * HG_D:(j + 1) * HG_D] * s_prev + _dot_tn(kj, vh)
        y = o * lax.rsqrt(jnp.mean(o * o, axis=-1, keepdims=True) + EPS) * hg[:, hs]
        outs.append(_bf(y * gate[:, hs]))
    return outs, new_states


def _hgrn_prompt_kernel(z_ref, lbl_ref, hg_ref, sums_ref, masks_ref, sel_ref, s0_ref, y_ref, sout_ref, state, *,
                        layer, tc, chunk):
    @pl.when(pl.program_id(1) == 0)
    def _():
        state[...] = s0_ref[0]

    lb = _hgrn_lb(lbl_ref, layer)
    hg = hg_ref[...]
    sel = sel_ref[...]
    for c in range(tc // chunk):
        rs = slice(c * chunk, (c + 1) * chunk)
        states = [[state[h] for h in range(HG_HEADS)]]
        outs, new_states = _hgrn_chunk(z_ref[rs, :], lb, hg, sums_ref, masks_ref, sel, states, None, chunk)
        for h in range(HG_HEADS):
            y_ref[rs, h * HG_D:(h + 1) * HG_D] = outs[h]
            state[h] = new_states[0][h]

    @pl.when(pl.program_id(1) == pl.num_programs(1) - 1)
    def _():
        sout_ref[0] = state[...]


def _hgrn_prompt(zb, lb_logits, hgrn_g, s0, *, layer, nb, tc, chunk):
    n = zb.shape[0]
    nt = n // (nb * tc)
    sums, masks = _hgrn_consts(chunk, chunk)
    sel = jnp.ones((chunk, HG_D), BF16)
    tok = lambda b, i: (b * nt + i, 0)
    kern = functools.partial(_hgrn_prompt_kernel, layer=layer, tc=tc, chunk=chunk)
    return pl.pallas_call(
        kern,
        grid=(nb, nt),
        in_specs=[
            pl.BlockSpec((tc, 4 * BW), tok),
            _resident(lb_logits.shape),
            _resident((1, BW)),
            _resident(sums.shape),
            _resident(masks.shape),
            _resident(sel.shape),
            pl.BlockSpec((1, HG_HEADS, HG_D, HG_D), lambda b, i: (b, 0, 0, 0)),
        ],
        out_specs=[
            pl.BlockSpec((tc, BW), tok),
            pl.BlockSpec((1, HG_HEADS, HG_D, HG_D), lambda b, i: (b, 0, 0, 0)),
        ],
        out_shape=[
            jax.ShapeDtypeStruct((n, BW), BF16),
            jax.ShapeDtypeStruct((nb, HG_HEADS, HG_D, HG_D), F32),
        ],
        scratch_shapes=[pltpu.VMEM((HG_HEADS, HG_D, HG_D), F32)],
        compiler_params=_cparams("arbitrary", "arbitrary"),
        name="hgrn_p",
    )(zb, lb_logits, hgrn_g, sums, masks, sel, s0)


def _hgrn_sample_kernel(z_ref, lbl_ref, hg_ref, sums_ref, masks_ref, sel_ref, s0_ref, y_ref, sout_ref, *,
                        layer, rows, seq):
    nseq = rows // seq
    lb = _hgrn_lb(lbl_ref, layer)
    ridx = lax.broadcasted_iota(jnp.int32, (rows, 1), 0)
    row_masks = [jnp.where((ridx >= j * seq) & (ridx < (j + 1) * seq), 1.0, 0.0) for j in range(nseq)]
    states = [[s0_ref[j, h] for h in range(HG_HEADS)] for j in range(nseq)]
    outs, new_states = _hgrn_chunk(z_ref[...], lb, hg_ref[...], sums_ref, masks_ref, sel_ref[...], states,
                                   row_masks, rows)
    for h in range(HG_HEADS):
        y_ref[:, h * HG_D:(h + 1) * HG_D] = outs[h]
        for j in range(nseq):
            sout_ref[j, h] = new_states[j][h]


def _hgrn_sample(zb, lb_logits, hgrn_g, s0, *, layer, seq, rows):
    n = zb.shape[0]
    nseq = rows // seq
    sums, masks = _hgrn_consts(rows, seq)
    sel = np.zeros((rows, nseq * HG_D), np.float32)
    for j in range(nseq):
        sel[j * seq:(j + 1) * seq, j * HG_D:(j + 1) * HG_D] = 1.0
    sel = jnp.asarray(sel, BF16)
    kern = functools.partial(_hgrn_sample_kernel, layer=layer, rows=rows, seq=seq)
    return pl.pallas_call(
        kern,
        grid=(n // rows,),
        in_specs=[
            pl.BlockSpec((rows, 4 * BW), lambda i: (i, 0)),
            _resident(lb_logits.shape),
            _resident((1, BW)),
            _resident(sums.shape),
            _resident(masks.shape),
            _resident(sel.shape),
            pl.BlockSpec((nseq, HG_HEADS, HG_D, HG_D), lambda i: (i, 0, 0, 0)),
        ],
        out_specs=[
            pl.BlockSpec((rows, BW), lambda i: (i, 0)),
            pl.BlockSpec((nseq, HG_HEADS, HG_D, HG_D), lambda i: (i, 0, 0, 0)),
        ],
        out_shape=[
            jax.ShapeDtypeStruct((n, BW), BF16),
            jax.ShapeDtypeStruct(s0.shape, F32),
        ],
        compiler_params=_cparams("arbitrary"),
        name="hgrn_s",
    )(zb, lb_logits, hgrn_g, sums, masks, sel, s0)


def _attn_prompt_kernel(qi_ref, kj_ref, q_ref, k_ref, v_ref, o_ref, m_sc, l_sc, acc_sc, *, tb):
    step = pl.program_id(1)
    i = qi_ref[step]
    j = kj_ref[step]

    @pl.when(j == 0)
    def _():
        m_sc[...] = jnp.full(m_sc.shape, NEG_BIG, F32)
        l_sc[...] = jnp.zeros(l_sc.shape, F32)
        acc_sc[...] = jnp.zeros(acc_sc.shape, F32)

    def update(diagonal):
        if diagonal:
            keep = (lax.broadcasted_iota(jnp.int32, (tb, tb), 1) <= lax.broadcasted_iota(jnp.int32, (tb, tb), 0))
        for h in range(C_HEADS):
            hs = slice(h * LANE, (h + 1) * LANE)
            s = _dot_nt(q_ref[:, hs], k_ref[:, hs])
            if diagonal:
                s = jnp.where(keep, s, NEG_BIG)
            m_prev = m_sc[h]
            m_new = jnp.maximum(m_prev, jnp.max(s, axis=-1, keepdims=True))
            alpha = jnp.exp2(m_prev - m_new)
            p = jnp.exp2(s - m_new[:, :1])
            l_sc[h] = alpha * l_sc[h] + jnp.sum(p, axis=-1, keepdims=True)
            acc_sc[h] = alpha * acc_sc[h] + _dot(_bf(p), v_ref[:, hs])
            m_sc[h] = m_new

    @pl.when(j < i)
    def _():
        update(False)

    @pl.when(j == i)
    def _():
        update(True)
        for h in range(0, C_HEADS, 2):
            o0 = acc_sc[h][:, :C_V] / l_sc[h][:, :C_V]
            o1 = acc_sc[h + 1][:, :C_V] / l_sc[h + 1][:, :C_V]
            o_ref[:, h * C_V:(h + 2) * C_V] = _bf(jnp.concatenate([o0, o1], axis=-1))


def _attn_prompt(q, k, v, *, nb, t, tb):
    n = q.shape[0]
    nblk = t // tb
    pairs = [(i, j) for i in range(nblk) for j in range(i + 1)]
    qi = jnp.asarray([p[0] for p in pairs], jnp.int32)
    kj = jnp.asarray([p[1] for p in pairs], jnp.int32)
    q_map = lambda b, s, qi, kj: (b * nblk + qi[s], 0)
    k_map = lambda b, s, qi, kj: (b * nblk + kj[s], 0)
    grid_spec = pltpu.PrefetchScalarGridSpec(
        num_scalar_prefetch=2,
        grid=(nb, len(pairs)),
        in_specs=[
            pl.BlockSpec((tb, C_HEADS * LANE), q_map),
            pl.BlockSpec((tb, C_HEADS * LANE), k_map),
            pl.BlockSpec((tb, C_HEADS * LANE), k_map),
        ],
        out_specs=pl.BlockSpec((tb, C_HEADS * C_V), q_map),
        scratch_shapes=[
            pltpu.VMEM((C_HEADS, tb, LANE), F32),
            pltpu.VMEM((C_HEADS, tb, LANE), F32),
            pltpu.VMEM((C_HEADS, tb, LANE), F32),
        ],
    )
    return pl.pallas_call(
        functools.partial(_attn_prompt_kernel, tb=tb),
        grid_spec=grid_spec,
        out_shape=jax.ShapeDtypeStruct((n, C_HEADS * C_V), BF16),
        compiler_params=_cparams("arbitrary", "arbitrary"),
        name="attn_p",
    )(qi, kj, q, k, v)


def _attn_sample_kernel(pt_ref, q_ref, kn_ref, *refs, pages, seq):
    lat_refs = refs[:pages]
    kr_refs = refs[pages:2 * pages]
    o_ref = refs[2 * pages]
    rows = C_HEADS * seq

    q = q_ref[0]
    q_lat = q[:, :C_KVL]
    q_rope = q[:, C_KVL + ROPE_LANE:C_KVL + ROPE_LANE + C_ROPE]
    lats = [_bf(r[...]) for r in lat_refs]
    s = [_dot_nt(q_lat, lats[p]) + _dot(q_rope, _bf(kr_refs[p][...])) for p in range(pages)]

    qf = q.astype(F32)
    kn = kn_ref[0].astype(F32)
    tpos = lax.broadcasted_iota(jnp.int32, (rows, 1), 0) % seq
    sn = [jnp.where(tpos >= t, jnp.sum(qf * kn[t:t + 1, :], axis=-1, keepdims=True), NEG_BIG) for t in range(seq)]

    m_lane = s[0]
    for p in range(1, pages):
        m_lane = jnp.maximum(m_lane, s[p])
    m = jnp.max(m_lane, axis=-1, keepdims=True)
    for t in range(seq):
        m = jnp.maximum(m, sn[t])

    l_lane = jnp.zeros((rows, PAGE), F32)
    acc = jnp.zeros((rows, C_KVL), F32)
    for p in range(pages):
        e = jnp.exp(s[p] - m)
        l_lane = l_lane + e
        acc = acc + _dot(_bf(e), lats[p])
    l = jnp.sum(l_lane, axis=-1, keepdims=True)
    for t in range(seq):
        e = jnp.exp(sn[t] - m)
        l = l + e
        acc = acc + e * kn[t:t + 1, :C_KVL]
    o_ref[0] = _bf(acc / l)


def _attn_sample(page_table, q, k_new, cache_latent, cache_krope_t, *, layer, seq):
    nb, pages = page_table.shape
    rows = C_HEADS * seq
    kern = functools.partial(_attn_sample_kernel, pages=pages, seq=seq)

    def page_map(p):
        return lambda b, pt: (layer, pt[b, p], 0, 0)

    grid_spec = pltpu.PrefetchScalarGridSpec(
        num_scalar_prefetch=1,
        grid=(nb,),
        in_specs=(
            [pl.BlockSpec((1, rows, QK_W), lambda b, pt: (b, 0, 0)),
             pl.BlockSpec((1, seq, QK_W), lambda b, pt: (b, 0, 0))]
            + [pl.BlockSpec((None, None, PAGE, C_KVL), page_map(p)) for p in range(pages)]
            + [pl.BlockSpec((None, None, C_ROPE, PAGE), page_map(p)) for p in range(pages)]
        ),
        out_specs=pl.BlockSpec((1, rows, C_KVL), lambda b, pt: (b, 0, 0)),
    )
    return pl.pallas_call(
        kern,
        grid_spec=grid_spec,
        out_shape=jax.ShapeDtypeStruct((nb, rows, C_KVL), BF16),
        compiler_params=_cparams("arbitrary"),
        name="attn_s",
    )(page_table, q, k_new, *([cache_latent] * pages), *([cache_krope_t] * pages))


def _merge_kernel(ya_ref, yb_ref, oc_ref, gt_ref, x_ref, g1_ref, wb_ref, wo_ref, *refs, latent_out):
    if latent_out:
        wv_ref, o_ref = refs
        yc = _bf(_dot(oc_ref[...], wv_ref[...]))
    else:
        (o_ref,) = refs
        yc = oc_ref[...]
    m = gt_ref[:, 0:D_MODEL] * _dot(ya_ref[...], wb_ref[0])
    m = m + gt_ref[:, D_MODEL:2 * D_MODEL] * _dot(yb_ref[...], wb_ref[1])
    m = m + gt_ref[:, 2 * D_MODEL:3 * D_MODEL] * _dot(yc, wb_ref[2])
    o_ref[...] = x_ref[...] + g1_ref[0] * _dot(_bf(m), wo_ref[...])


def _merge(ya, yb, oc, gt, x, g1, wb, wo, wv, *, nb, tm):
    n = x.shape[0]
    nt = n // (nb * tm)
    mod_rows = g1.shape[1]
    tok = lambda b, i: (b * nt + i, 0)
    extra = [] if wv is None else [wv]
    return pl.pallas_call(
        functools.partial(_merge_kernel, latent_out=wv is not None),
        grid=(nb, nt),
        in_specs=[
            pl.BlockSpec((tm, BW), tok),
            pl.BlockSpec((tm, BW), tok),
            pl.BlockSpec((tm, oc.shape[1]), tok),
            pl.BlockSpec((tm, 3 * D_MODEL), tok),
            pl.BlockSpec((tm, D_MODEL), tok),
            pl.BlockSpec((1, mod_rows, D_MODEL), lambda b, i: (b, 0, 0)),
            _resident(wb.shape),
            _resident(wo.shape),
        ] + [_resident(a.shape) for a in extra],
        out_specs=pl.BlockSpec((tm, D_MODEL), tok),
        out_shape=jax.ShapeDtypeStruct((n, D_MODEL), F32),
        compiler_params=_cparams("arbitrary", "arbitrary"),
        name="merge",
    )(ya, yb, oc, gt, x, g1, wb, wo, *extra)


def _conv_gate(up_a, up_b, prev_a, prev_b, cw_a, cw_b, cb_a, cb_b, keep1, keep2):
    def conv(up, prev, cw, cb):
        s1 = jnp.where(keep1, pltpu.roll(up, 1, 0), prev[0])
        s2 = jnp.where(keep2, pltpu.roll(up, 2, 0), prev[1])
        return cb + cw[0:1, :] * s2 + cw[1:2, :] * s1 + cw[2:3, :] * up

    return _bf(_silu(conv(up_a, prev_a, cw_a, cb_a)) * conv(up_b, prev_b, cw_b, cb_b))


def _ffn_up_prompt_kernel(x_ref, sc_ref, sh_ref, g_ref, w_ref, cw_ref, cb_ref, c0_ref, act_ref, cs_ref, carry, *, tm):
    @pl.when(pl.program_id(1) == 0)
    def _():
        carry[...] = c0_ref[0]

    hb = _bf(_norm_mod(x_ref[...], g_ref[...], sc_ref[0], sh_ref[0]))
    ridx = lax.broadcasted_iota(jnp.int32, (tm, 1), 0)
    keep1 = ridx >= 1
    keep2 = ridx >= 2
    for c in range(N_FF_CHUNKS):
        ca = slice(c * FF_COL, (c + 1) * FF_COL)
        cb = slice(D_FF + c * FF_COL, D_FF + (c + 1) * FF_COL)
        up_a = _dot(hb, w_ref[:, ca])
        up_b = _dot(hb, w_ref[:, cb])
        old_a = carry[:, ca]
        old_b = carry[:, cb]
        prev_a = (old_a[7:8, :], jnp.where(ridx == 0, old_a[6:7, :], old_a[7:8, :]))
        prev_b = (old_b[7:8, :], jnp.where(ridx == 0, old_b[6:7, :], old_b[7:8, :]))
        act_ref[:, ca] = _conv_gate(up_a, up_b, prev_a, prev_b, cw_ref[:, ca], cw_ref[:, cb],
                                    cb_ref[:, ca], cb_ref[:, cb], keep1, keep2)
        carry[:, ca] = up_a[tm - 8:, :]
        carry[:, cb] = up_b[tm - 8:, :]

    @pl.when(pl.program_id(1) == pl.num_programs(1) - 1)
    def _():
        cs_ref[0] = carry[...]


def _ffn_up_prompt(x, sc, sh, g, w_up, conv_w, conv_b, conv0, *, nb, tm):
    n = x.shape[0]
    nt = n // (nb * tm)
    tok = lambda b, i: (b * nt + i, 0)
    kern = functools.partial(_ffn_up_prompt_kernel, tm=tm)
    return pl.pallas_call(
        kern,
        grid=(nb, nt),
        in_specs=[
            pl.BlockSpec((tm, D_MODEL), tok),
            pl.BlockSpec((1, 1, D_MODEL), lambda b, i: (b, 0, 0)),
            pl.BlockSpec((1, 1, D_MODEL), lambda b, i: (b, 0, 0)),
            _resident((1, D_MODEL)),
            _resident((D_MODEL, 2 * D_FF)),
            _resident((3, 2 * D_FF)),
            _resident((1, 2 * D_FF)),
            pl.BlockSpec((1, 8, 2 * D_FF), lambda b, i: (b, 0, 0)),
        ],
        out_specs=[
            pl.BlockSpec((tm, D_FF), tok),
            pl.BlockSpec((1, 8, 2 * D_FF), lambda b, i: (b, 0, 0)),
        ],
        out_shape=[
            jax.ShapeDtypeStruct((n, D_FF), BF16),
            jax.ShapeDtypeStruct((nb, 8, 2 * D_FF), F32),
        ],
        scratch_shapes=[pltpu.VMEM((8, 2 * D_FF), F32)],
        compiler_params=_cparams("arbitrary", "arbitrary"),
        name="ffn_up_p",
    )(x, sc, sh, g, w_up, conv_w, conv_b, conv0)


def _ffn_up_sample_kernel(x_ref, sc_ref, sh_ref, g_ref, wa_ref, wb_ref, cwa_ref, cwb_ref, cba_ref, cbb_ref,
                          pa_ref, pb_ref, act_ref, upa_ref, upb_ref, *, n, seq):
    hb = _bf(_norm_mod(x_ref[...], g_ref[...], sc_ref[0], sh_ref[0]))
    tpos = lax.broadcasted_iota(jnp.int32, (n, 1), 0) % seq
    up_a = _dot(hb, wa_ref[...])
    up_b = _dot(hb, wb_ref[...])
    upa_ref[...] = up_a
    upb_ref[...] = up_b
    act_ref[...] = _conv_gate(up_a, up_b, (pa_ref[0], pa_ref[1]), (pb_ref[0], pb_ref[1]), cwa_ref[...], cwb_ref[...],
                              cba_ref[...], cbb_ref[...], tpos >= 1, tpos >= 2)


def _ffn_up_sample(x, sc, sh, g, w_up, conv_w, conv_b, prev, *, seq):
    n = x.shape[0]
    col_a = lambda j: (0, j)
    col_b = lambda j: (0, N_FF_CHUNKS + j)
    kern = functools.partial(_ffn_up_sample_kernel, n=n, seq=seq)
    return pl.pallas_call(
        kern,
        grid=(N_FF_CHUNKS,),
        in_specs=[
            _resident((n, D_MODEL)),
            _resident((1, n, D_MODEL)),
            _resident((1, n, D_MODEL)),
            _resident((1, D_MODEL)),
            pl.BlockSpec((D_MODEL, FF_COL), col_a),
            pl.BlockSpec((D_MODEL, FF_COL), col_b),
            pl.BlockSpec((3, FF_COL), col_a),
            pl.BlockSpec((3, FF_COL), col_b),
            pl.BlockSpec((1, FF_COL), col_a),
            pl.BlockSpec((1, FF_COL), col_b),
            pl.BlockSpec((2, n, FF_COL), lambda j: (0, 0, j)),
            pl.BlockSpec((2, n, FF_COL), lambda j: (0, 0, N_FF_CHUNKS + j)),
        ],
        out_specs=[
            pl.BlockSpec((n, FF_COL), col_a),
            pl.BlockSpec((n, FF_COL), col_a),
            pl.BlockSpec((n, FF_COL), col_a),
        ],
        out_shape=[
            jax.ShapeDtypeStruct((n, D_FF), BF16),
            jax.ShapeDtypeStruct((n, D_FF), F32),
            jax.ShapeDtypeStruct((n, D_FF), F32),
        ],
        compiler_params=_cparams("arbitrary"),
        name="ffn_up_s",
    )(x, sc, sh, g, w_up, w_up, conv_w, conv_w, conv_b, conv_b, prev, prev)


def _ffn_down_kernel(act_ref, x_ref, g2_ref, w_ref, fg_ref, o_ref, *, final):
    xn = x_ref[...] + g2_ref[0] * _dot(act_ref[...], w_ref[...])
    o_ref[...] = _rms(xn, fg_ref[...]) if final else xn


def _ffn_down(act, x, g2, w_down, final_g, *, nb, tm, final):
    n = x.shape[0]
    nt = n // (nb * tm)
    mod_rows = g2.shape[1]
    tok = lambda b, i: (b * nt + i, 0)
    return pl.pallas_call(
        functools.partial(_ffn_down_kernel, final=final),
        grid=(nb, nt),
        in_specs=[
            pl.BlockSpec((tm, D_FF), tok),
            pl.BlockSpec((tm, D_MODEL), tok),
            pl.BlockSpec((1, mod_rows, D_MODEL), lambda b, i: (b, 0, 0)),
            _resident((D_FF, D_MODEL)),
            _resident((1, D_MODEL)),
        ],
        out_specs=pl.BlockSpec((tm, D_MODEL), tok),
        out_shape=jax.ShapeDtypeStruct((n, D_MODEL), F32),
        compiler_params=_cparams("arbitrary", "arbitrary"),
        name="ffn_down",
    )(act, x, g2, w_down, final_g)


def _rope_tables(pos):
    half = C_ROPE // 2
    inv = ROPE_THETA ** (-jnp.arange(half, dtype=F32) / half)
    ang = pos.astype(F32)[:, None] * inv[None, :]
    cos, sin = jnp.cos(ang), jnp.sin(ang)
    return _rope_block(jnp.concatenate([cos, cos], -1)), _rope_block(jnp.concatenate([-sin, sin], -1))


def _rope_block(a):
    lead = a.shape[:-1]
    return jnp.concatenate([jnp.zeros(lead + (ROPE_LANE,), a.dtype), a,
                            jnp.zeros(lead + (LANE - ROPE_LANE - C_ROPE,), a.dtype)], axis=-1)


def _swap_halves(w):
    half = C_ROPE // 2
    return jnp.concatenate([w[..., half:], w[..., :half]], axis=-1)


def _layer_weights(l, w_in, w_uq, w_ukv, w_branch, w_o, w_up, w_down):
    wi = w_in[l]
    o = 0
    segs = []
    for s in (BW, BW, BW, BW, BW, BW, C_QL, C_KVL, C_ROPE, 3 * D_MODEL):
        segs.append(wi[:, o:o + s])
        o += s
    ua, va, qb, fb, ib, gb, cq, ckv, krr, gt = segs
    w_in_l = _bf(jnp.concatenate([ua, va, qb, fb, ib, gb, gt, cq, ckv, _rope_block(krr),
                                  _rope_block(_swap_halves(krr))], axis=1))

    def head_blocks(a):
        pad = jnp.zeros(a.shape[:-1] + (LANE - a.shape[-1],), F32)
        return jnp.concatenate([a, pad], -1).reshape(a.shape[0], C_HEADS * LANE)

    wq = w_uq[l].reshape(C_QL, C_HEADS, C_NOPE + C_ROPE)
    q_nope, q_rope = wq[..., :C_NOPE], wq[..., C_NOPE:]
    wqn = head_blocks(q_nope)
    wqa = _rope_block(q_rope).reshape(C_QL, C_HEADS * LANE)
    wqb = _rope_block(_swap_halves(q_rope)).reshape(C_QL, C_HEADS * LANE)

    wkv = w_ukv[l].reshape(C_KVL, C_HEADS, C_NOPE + C_V)
    wk = jnp.transpose(wkv[..., :C_NOPE], (1, 2, 0))
    wk = jnp.concatenate([wk, jnp.zeros((C_HEADS, LANE - C_NOPE, C_KVL), F32)], axis=1)
    wv = jnp.transpose(wkv[..., C_NOPE:], (1, 0, 2))
    eye = jnp.eye(C_HEADS, dtype=F32)
    wv_bd = (wv[:, :, None, :] * eye[:, None, :, None]).reshape(C_HEADS * C_KVL, C_HEADS * C_V)
    attn_sample = [_bf(wqn), _bf(wqa), _bf(wqb), _bf(wk)]
    attn_prompt = [_bf(wqn + wqa), _bf(wqb), _bf(head_blocks(wkv[..., :C_NOPE])), _bf(head_blocks(wkv[..., C_NOPE:]))]
    return dict(w_in=w_in_l, attn_s=attn_sample, attn_p=attn_prompt, wv=_bf(wv_bd), wb=_bf(w_branch[l]),
                wo=_bf(w_o[l]), w_up=_bf(w_up[l]), w_down=_bf(w_down[l]))


def _sample_mix_tables(w_s_l, b_s_l, seq):
    t = np.arange(8) % seq
    coef = []
    for j in range(seq):
        src = t - j
        wj = w_s_l[:, t, np.maximum(src, 0)]
        wj = jnp.where(jnp.asarray(src >= 0)[None, :], wj, 0.0)
        coef.append(jnp.repeat(wj.T, LANE, axis=1))
    bias = jnp.repeat(b_s_l[:, t].T, LANE, axis=1)
    return jnp.stack(coef), bias


def _trunk(x, mods, lw, *, sample, nb, t, pos, s0, conv0, cache_latent, cache_krope, page_table,
           w_s, b_s, lnv_g, lnv_b, lb_logits, hgrn_g, q_a_g, kv_a_g, norm1_g, norm2_g, final_g, conv_w, conv_b):
    n = nb * t
    cos, sin = _rope_tables(pos)
    lat_out, kr_out, gv_out, s_out, conv_out = [], [], [], [], []
    for l in range(DEPTH):
        sh1, sc1, g1, sh2, sc2, g2 = mods[l]
        w = lw[l]
        row = lambda a: a[l].reshape(1, -1)
        if sample:
            ws, bs = _sample_mix_tables(w_s[l], b_s[l], t)
            kw = dict(nb=1, tm=n, gv_rows=n, sample=True)
        else:
            ws, bs = w_s[l], jnp.repeat(b_s[l].T, LANE, axis=1)
            kw = dict(nb=nb, tm=256, gv_rows=A_CHUNK, sample=False)
        outs = _in_proj(
            x, sc1, sh1, row(norm1_g), w["w_in"], ws, bs, row(lnv_g), row(lnv_b), row(q_a_g), row(kv_a_g),
            cos, sin, w["attn_s"] if sample else w["attn_p"], **kw)
        if sample:
            ya, gv, zb, gt, q, k, lat, kr = outs
            yb, s_new = _hgrn_sample(zb, lb_logits, row(hgrn_g), s0[l], layer=l, seq=t, rows=64)
            q_b = q.reshape(nb, t, C_HEADS, QK_W).transpose(0, 2, 1, 3).reshape(nb, C_HEADS * t, QK_W)
            oc = _attn_sample(page_table, q_b, k.reshape(nb, t, QK_W), cache_latent, cache_krope, layer=l, seq=t)
            oc = oc.reshape(nb, C_HEADS, t, C_KVL).transpose(0, 2, 1, 3).reshape(n, C_HEADS * C_KVL)
            mkw = dict(nb=1, tm=n)
            wv = w["wv"]
        else:
            ya, gv, zb, gt, q, k, v, lat, kr = outs
            yb, s_new = _hgrn_prompt(zb, lb_logits, row(hgrn_g), s0[l], layer=l, nb=nb, tc=256, chunk=128)
            oc = _attn_prompt(q, k, v, nb=nb, t=t, tb=512)
            mkw = dict(nb=nb, tm=512)
            wv = None
        x = _merge(ya, yb, oc, gt, x, g1, w["wb"], w["wo"], wv, **mkw)
        if sample:
            c0 = conv0[l]
            zero = jnp.zeros((nb, t - 1, 2 * D_FF), F32)
            prev1 = jnp.concatenate([c0[:, 1:2], zero], axis=1)
            prev2 = jnp.concatenate([c0[:, 0:2], zero[:, 1:]], axis=1)
            prev = jnp.stack([prev1.reshape(n, 2 * D_FF), prev2.reshape(n, 2 * D_FF)])
            act, up_a, up_b = _ffn_up_sample(x, sc2, sh2, row(norm2_g), w["w_up"], conv_w[l], row(conv_b), prev, seq=t)
            up = jnp.concatenate([up_a, up_b], axis=-1).reshape(nb, t, 2 * D_FF)
            conv_out.append(up[:, t - 2:, :])
        else:
            c0 = jnp.concatenate([jnp.zeros((nb, 6, 2 * D_FF), F32), conv0[l]], axis=1)
            act, cs = _ffn_up_prompt(x, sc2, sh2, row(norm2_g), w["w_up"], conv_w[l], row(conv_b), c0, nb=nb, tm=512)
            conv_out.append(cs[:, 6:, :])
        x = _ffn_down(act, x, g2, w["w_down"], final_g.reshape(1, -1), final=(l == DEPTH - 1), **mkw)
        lat_out.append(lat.reshape(nb, t, C_KVL))
        kr_out.append(kr.reshape(nb, t, C_ROPE))
        gv_out.append(gv.reshape(nb, -1, BW))
        s_out.append(s_new)
    return (x.reshape(nb, t, D_MODEL), jnp.stack(lat_out), jnp.stack(kr_out), jnp.stack(gv_out), jnp.stack(s_out),
            jnp.stack(conv_out))


def kernel(x_prompt, x_sample, c_prompt, c_sample, cache_latent, cache_krope, state_hgrn, state_conv, page_table,
           norm1_g, norm2_g, final_g, w_ada, b_ada, w_in, w_s, b_s, lnv_g, lnv_b, lb_logits, hgrn_g, q_a_g, kv_a_g,
           w_uq, w_ukv, w_branch, w_o, w_up, conv_w, conv_b, w_down):
    bp, tp, _ = x_prompt.shape
    bs_, ts, _ = x_sample.shape
    past = page_table.shape[1] * cache_latent.shape[2]

    rows = bp + bs_
    rows_pad = -(-rows // 8) * 8
    c_all = jnp.concatenate([c_prompt, c_sample, jnp.zeros((rows_pad - rows, D_MODEL), F32)], axis=0)
    mod = _ada(c_all, w_ada, b_ada)
    mods_p, mods_s = [], []
    for l in range(DEPTH):
        chunks = [mod[l, :, i * D_MODEL:(i + 1) * D_MODEL] for i in range(6)]
        mods_p.append([m[:bp].reshape(bp, 1, D_MODEL) for m in chunks])
        mods_s.append([jnp.repeat(m[bp:rows], ts, axis=0).reshape(1, bs_ * ts, D_MODEL) for m in chunks])

    lw = [_layer_weights(l, w_in, w_uq, w_ukv, w_branch, w_o, w_up, w_down) for l in range(DEPTH)]
    shared = dict(w_s=w_s, b_s=b_s, lnv_g=lnv_g, lnv_b=lnv_b, lb_logits=lb_logits, hgrn_g=hgrn_g, q_a_g=q_a_g,
                  kv_a_g=kv_a_g, norm1_g=norm1_g, norm2_g=norm2_g, final_g=final_g, conv_w=conv_w, conv_b=conv_b)

    s0_p = jnp.zeros((DEPTH, bp, HG_HEADS, HG_D, HG_D), F32)
    conv0_p = jnp.zeros((DEPTH, bp, 2, 2 * D_FF), F32)
    out_p = _trunk(x_prompt.reshape(bp * tp, D_MODEL), mods_p, lw, sample=False, nb=bp, t=tp, pos=jnp.arange(tp),
                   s0=s0_p, conv0=conv0_p, cache_latent=None, cache_krope=None, page_table=None, **shared)
    pos_s = jnp.tile(past + jnp.arange(ts), bs_)
    out_s = _trunk(x_sample.reshape(bs_ * ts, D_MODEL), mods_s, lw, sample=True, nb=bs_, t=ts, pos=pos_s,
                   s0=state_hgrn, conv0=state_conv, cache_latent=cache_latent,
                   cache_krope=jnp.swapaxes(cache_krope, 2, 3),
                   page_table=page_table, **shared)
    y_p, lat_p, kr_p, gv_p, hgrn_p, conv_p = out_p
    y_s, lat_s, kr_s, gv_s, hgrn_s, conv_s = out_s
    return (y_p, y_s, lat_p, kr_p, gv_p, hgrn_p, conv_p, lat_s, kr_s, gv_s, hgrn_s, conv_s)
```

```python
import functools
import math

import numpy as np
import jax
import jax.numpy as jnp
from jax import lax
from jax.experimental import pallas as pl
from jax.experimental.pallas import tpu as pltpu

F32 = jnp.float32
BF16 = jnp.bfloat16

D_MODEL = 1024
DEPTH = 4
PAGE = 128
EPS = 1e-6
NEG_BIG = -1e30
LB_FLOOR = 1e-30
BW = 512
A_CHUNK = 128
A_GROUPS = 4
HG_HEADS = 4
HG_D = 128
C_HEADS = 8
C_NOPE = 64
C_ROPE = 32
C_V = 64
C_QL = 384
C_KVL = 256
QK_W = 384
D_FF = 2816
ROPE_THETA = 10000.0
ATT_SCALE = (C_NOPE + C_ROPE) ** -0.5
LOG2E = math.log2(math.e)
ROPE_LANE = C_NOPE

O_UA, O_VA, O_B, O_GT, O_CQ, O_CKV, O_KRA, O_KRB, W_IN_COLS = 0, 512, 1024, 3072, 6144, 6528, 6784, 6912, 7040

VMEM_LIMIT_BYTES = 58 * 1024 * 1024
LANE = 128
FF_COL = 256
N_FF_CHUNKS = D_FF // FF_COL


def _cparams(*sem):
    return pltpu.CompilerParams(dimension_semantics=sem, vmem_limit_bytes=VMEM_LIMIT_BYTES)


def _resident(shape):
    nd = len(shape)
    return pl.BlockSpec(shape, lambda *_: (0,) * nd, pipeline_mode=pl.Buffered(1))


def _bf(x):
    return x.astype(BF16)


def _dot(a, b):
    return jnp.dot(a, b, preferred_element_type=F32)


def _dot_nt(a, b):
    return lax.dot_general(a, b, (((1,), (1,)), ((), ())), preferred_element_type=F32)


def _dot_tn(a, b):
    return lax.dot_general(a, b, (((0,), (0,)), ((), ())), preferred_element_type=F32)


def _sigmoid(x):
    return 1.0 / (1.0 + jnp.exp(-x))


def _silu(x):
    return x * _sigmoid(x)


def _gelu_tanh(x):
    return 0.5 * x * (1.0 + jnp.tanh(math.sqrt(2.0 / math.pi) * (x + 0.044715 * (x * x * x))))


def _softplus(x):
    return jnp.maximum(x, 0.0) + jnp.log1p(jnp.exp(-jnp.abs(x)))


def _logaddexp(a, b):
    return jnp.maximum(a, b) + jnp.log1p(jnp.exp(-jnp.abs(a - b)))


def _rms(x, g):
    return x * lax.rsqrt(jnp.mean(x * x, axis=-1, keepdims=True) + EPS) * g


def _norm_mod(x, g, sc, sh):
    return _rms(x, g) * (1.0 + sc) + sh


def _split3(x):
    hi = _bf(x)
    r1 = x - hi.astype(F32)
    mid = _bf(r1)
    lo = _bf(r1 - mid.astype(F32))
    return hi, mid, lo


def _ada_kernel(c_ref, w_ref, b_ref, o_ref):
    c = c_ref[...]
    o_ref[0] = _dot(_bf(_silu(c)), _bf(w_ref[0])) + b_ref[0]


def _ada(c_all, w_ada, b_ada):
    rows = c_all.shape[0]
    tn = 1536
    return pl.pallas_call(
        _ada_kernel,
        grid=(DEPTH, 6 * D_MODEL // tn),
        in_specs=[
            pl.BlockSpec((rows, D_MODEL), lambda l, j: (0, 0)),
            pl.BlockSpec((1, D_MODEL, tn), lambda l, j: (l, 0, j)),
            pl.BlockSpec((1, 1, tn), lambda l, j: (l, 0, j)),
        ],
        out_specs=pl.BlockSpec((1, rows, tn), lambda l, j: (l, 0, j)),
        out_shape=jax.ShapeDtypeStruct((DEPTH, rows, 6 * D_MODEL), F32),
        compiler_params=_cparams("arbitrary", "arbitrary"),
        name="ada",
    )(c_all, w_ada, b_ada.reshape(DEPTH, 1, 6 * D_MODEL))


def _in_proj_kernel(x_ref, sc_ref, sh_ref, g_ref, w_ref, ws_ref, bs_ref, lng_ref, lnb_ref, qag_ref, kvg_ref,
                    cos_ref, sin_ref, *refs, tm, gv_rows, sample):
    if sample:
        wqn_ref, wqa_ref, wqb_ref, wk_ref, ya_ref, gv_ref, zb_ref, gt_ref, q_ref, k_ref, lat_ref, kr_ref = refs
    else:
        wq1_ref, wqb_ref, wkn_ref, wvv_ref, ya_ref, gv_ref, zb_ref, gt_ref, q_ref, k_ref, v_ref, lat_ref, kr_ref = refs
    hb = _bf(_norm_mod(x_ref[...], g_ref[...], sc_ref[0], sh_ref[0]))

    u = _gelu_tanh(_dot(hb, w_ref[:, O_UA:O_UA + BW]))
    va = _gelu_tanh(_dot(hb, w_ref[:, O_VA:O_VA + BW]))
    mu = jnp.mean(va, axis=-1, keepdims=True)
    var = jnp.mean(jnp.square(va - mu), axis=-1, keepdims=True)
    v = (va - mu) * lax.rsqrt(var + EPS) * lng_ref[...] + lnb_ref[...]
    if sample:
        v3 = v.reshape(tm // 8, 8, BW)
        mixed = v3 * ws_ref[0][None] + bs_ref[...][None]
        for j in range(1, 4):
            vj = pltpu.roll(v, j, 0).reshape(tm // 8, 8, BW)
            mixed = mixed + vj * ws_ref[j][None]
        ya_ref[...] = _bf(u * mixed.reshape(tm, BW))
    else:
        row = lax.broadcasted_iota(jnp.int32, (A_CHUNK, A_CHUNK), 0)
        col = lax.broadcasted_iota(jnp.int32, (A_CHUNK, A_CHUNK), 1)
        wc = [_bf(jnp.where(col <= row, ws_ref[g], 0.0)) for g in range(A_GROUPS)]
        for c in range(tm // A_CHUNK):
            rs = slice(c * A_CHUNK, (c + 1) * A_CHUNK)
            for g in range(A_GROUPS):
                cs = slice(g * LANE, (g + 1) * LANE)
                mixed = _dot(wc[g], _bf(v[rs, cs])) + bs_ref[:, cs]
                ya_ref[rs, cs] = _bf(u[rs, cs] * mixed)

    @pl.when(pl.program_id(1) == pl.num_programs(1) - 1)
    def _():
        gv_ref[0] = v[tm - gv_rows:, :]

    zb_ref[...] = _dot(hb, w_ref[:, O_B:O_B + 4 * BW])
    gt_ref[...] = _sigmoid(_dot(hb, w_ref[:, O_GT:O_GT + 3 * D_MODEL]))

    cos = cos_ref[...]
    sin = sin_ref[...]
    qn = _bf(_rms(_dot(hb, w_ref[:, O_CQ:O_CQ + C_QL]), qag_ref[...]))
    lat = _rms(_dot(hb, w_ref[:, O_CKV:O_CKV + C_KVL]), kvg_ref[...])
    kr = _dot(hb, w_ref[:, O_KRA:O_KRA + LANE]) * cos + _dot(hb, w_ref[:, O_KRB:O_KRB + LANE]) * sin
    lat_ref[...] = lat
    kr_ref[...] = kr[:, ROPE_LANE:ROPE_LANE + C_ROPE]
    if sample:
        for h in range(C_HEADS):
            hs = slice(h * LANE, (h + 1) * LANE)
            q_nope = _bf(_dot(qn, wqn_ref[:, hs]))
            q_lat = _dot(q_nope, wk_ref[h])
            q_rope = _dot(qn, wqa_ref[:, hs]) * cos + _dot(qn, wqb_ref[:, hs]) * sin
            q_ref[:, h * QK_W:h * QK_W + C_KVL] = _bf(q_lat * ATT_SCALE)
            q_ref[:, h * QK_W + C_KVL:(h + 1) * QK_W] = _bf(q_rope * ATT_SCALE)
        k_ref[:, :C_KVL] = _bf(lat)
        k_ref[:, C_KVL:] = _bf(kr)
    else:
        lane = lax.broadcasted_iota(jnp.int32, (1, LANE), 1)
        cos_q = cos + jnp.where(lane < C_NOPE, 1.0, 0.0)
        latb = _bf(lat)
        for h in range(C_HEADS):
            hs = slice(h * LANE, (h + 1) * LANE)
            q = _dot(qn, wq1_ref[:, hs]) * cos_q + _dot(qn, wqb_ref[:, hs]) * sin
            q_ref[:, hs] = _bf(q * (ATT_SCALE * LOG2E))
            k_ref[:, hs] = _bf(_dot(latb, wkn_ref[:, hs]) + kr)
        v_ref[...] = _bf(_dot(latb, wvv_ref[...]))


def _in_proj(x, sc, sh, g, w, ws, bs, lng, lnb, qag, kvg, cos, sin, attn_w, *, nb, tm, gv_rows, sample):
    n = x.shape[0]
    nt = n // (nb * tm)
    mod_rows = sc.shape[1]
    tok = lambda b, i: (b * nt + i, 0)
    kern = functools.partial(_in_proj_kernel, tm=tm, gv_rows=gv_rows, sample=sample)
    if sample:
        qkv_specs = [pl.BlockSpec((tm, C_HEADS * QK_W), tok), pl.BlockSpec((tm, QK_W), tok)]
        qkv_shapes = [jax.ShapeDtypeStruct((n, C_HEADS * QK_W), BF16), jax.ShapeDtypeStruct((n, QK_W), BF16)]
    else:
        qkv_specs = [pl.BlockSpec((tm, C_HEADS * LANE), tok)] * 3
        qkv_shapes = [jax.ShapeDtypeStruct((n, C_HEADS * LANE), BF16)] * 3
    return pl.pallas_call(
        kern,
        grid=(nb, nt),
        in_specs=[
            pl.BlockSpec((tm, D_MODEL), tok),
            pl.BlockSpec((1, mod_rows, D_MODEL), lambda b, i: (b, 0, 0)),
            pl.BlockSpec((1, mod_rows, D_MODEL), lambda b, i: (b, 0, 0)),
            _resident((1, D_MODEL)),
            _resident((D_MODEL, W_IN_COLS)),
            _resident(ws.shape),
            _resident(bs.shape),
            _resident((1, BW)),
            _resident((1, BW)),
            _resident((1, C_QL)),
            _resident((1, C_KVL)),
            pl.BlockSpec((tm, LANE), lambda b, i: (i, 0)),
            pl.BlockSpec((tm, LANE), lambda b, i: (i, 0)),
        ] + [_resident(a.shape) for a in attn_w],
        out_specs=[
            pl.BlockSpec((tm, BW), tok),
            pl.BlockSpec((1, gv_rows, BW), lambda b, i: (b, 0, 0)),
            pl.BlockSpec((tm, 4 * BW), tok),
            pl.BlockSpec((tm, 3 * D_MODEL), tok),
        ] + qkv_specs + [
            pl.BlockSpec((tm, C_KVL), tok),
            pl.BlockSpec((tm, C_ROPE), tok),
        ],
        out_shape=[
            jax.ShapeDtypeStruct((n, BW), BF16),
            jax.ShapeDtypeStruct((nb, gv_rows, BW), F32),
            jax.ShapeDtypeStruct((n, 4 * BW), F32),
            jax.ShapeDtypeStruct((n, 3 * D_MODEL), F32),
        ] + qkv_shapes + [
            jax.ShapeDtypeStruct((n, C_KVL), F32),
            jax.ShapeDtypeStruct((n, C_ROPE), F32),
        ],
        compiler_params=_cparams("arbitrary", "arbitrary"),
        name="in_proj_s" if sample else "in_proj_p",
    )(x, sc, sh, g, w, ws, bs, lng, lnb, qag, kvg, cos, sin, *attn_w)


def _hgrn_consts(chunk, seg):
    t = np.arange(chunk)
    u = t[None, :]
    tt = t[:, None]
    same_seg = (t // seg)[:, None] == (t // seg)[None, :]
    blocks = [same_seg & (u <= tt), same_seg & (u > tt)]
    masks = [np.eye(chunk, dtype=bool)]
    b = seg // 2
    while b >= 1:
        pos = t % (2 * b)
        p = t - pos
        m = (p + b - 1)[:, None]
        second = pos >= b
        blocks.append(np.where(second[:, None], (u > m) & (u <= tt), (u > tt) & (u <= m)))
        masks.append((p[:, None] == p[None, :]) & second[:, None] & (~second)[None, :])
        b //= 2
    sums = np.concatenate(blocks, 0).astype(np.float32)
    return jnp.asarray(sums, BF16), jnp.asarray(np.stack(masks).astype(np.float32))


def _hgrn_lb(lbl_ref, layer):
    logits = lbl_ref[...]
    e = jnp.exp(logits - jnp.max(logits, axis=0, keepdims=True))
    p = e / jnp.sum(e, axis=0, keepdims=True)
    lb = jnp.zeros((1, HG_HEADS * HG_D), F32)
    for i in range(1, layer + 1):
        lb = lb + p[i:i + 1, :]
    return lb


def _hgrn_chunk(zb, lb, hg, sums_ref, masks_ref, sel, states, row_masks, chunk):
    nlev = masks_ref.shape[0] - 1
    q = _silu(zb[:, 0:BW])
    zf = zb[:, BW:2 * BW]
    vb = zb[:, 2 * BW:3 * BW]
    gate = _silu(zb[:, 3 * BW:4 * BW])
    log_lb = jnp.log(jnp.maximum(lb, LB_FLOOR))
    logf = _logaddexp(log_lb, jnp.log1p(-lb) - _softplus(-zf))
    k = (1.0 - lb) * _sigmoid(-zf)

    parts = _split3(logf)
    sums = sums_ref[...]
    x = _dot(sums, parts[0]) + _dot(sums, parts[1]) + _dot(sums, parts[2])
    e = jnp.exp(x)
    e_cum = e[0:chunk]
    e_end = e[chunk:2 * chunk]
    nseq = len(states)

    outs = []
    new_states = [[None] * HG_HEADS for _ in range(nseq)]
    for h in range(HG_HEADS):
        hs = slice(h * HG_D, (h + 1) * HG_D)
        qh, kh, vh = q[:, hs], k[:, hs], _bf(vb[:, hs])
        scores = masks_ref[0] * _dot_nt(_bf(qh), _bf(kh))
        for lev in range(nlev):
            el = e[(2 + lev) * chunk:(3 + lev) * chunk, hs]
            scores = scores + masks_ref[1 + lev] * _dot_nt(_bf(qh * el), _bf(kh * el))
        o = _dot(_bf(scores), vh)
        q_in = _bf(qh * e_cum[:, hs])
        k_out = kh * e_end[:, hs]
        decay = jnp.exp(_dot_tn(parts[0][:, hs], sel) + _dot_tn(parts[1][:, hs], sel) + _dot_tn(parts[2][:, hs], sel))
        for j in range(nseq):
            s_prev = states[j][h]
            if nseq == 1:
                o = o + _dot(q_in, _bf(s_prev))
                kj = _bf(k_out)
            else:
                o = o + row_masks[j] * _dot(q_in, _bf(s_prev))
                kj = _bf(k_out * row_masks[j])
            new_states[j][h] = decay[:, j * HG_D:(j + 1) * HG_D] * s_prev + _dot_tn(kj, vh)
        y = o * lax.rsqrt(jnp.mean(o * o, axis=-1, keepdims=True) + EPS) * hg[:, hs]
        outs.append(_bf(y * gate[:, hs]))
    return outs, new_states


def _hgrn_prompt_kernel(z_ref, lbl_ref, hg_ref, sums_ref, masks_ref, sel_ref, s0_ref, y_ref, sout_ref, state, *,
                        layer, tc, chunk):
    @pl.when(pl.program_id(1) == 0)
    def _():
        state[...] = s0_ref[0]

    lb = _hgrn_lb(lbl_ref, layer)
    hg = hg_ref[...]
    sel = sel_ref[...]
    for c in range(tc // chunk):
        rs = slice(c * chunk, (c + 1) * chunk)
        states = [[state[h] for h in range(HG_HEADS)]]
        outs, new_states = _hgrn_chunk(z_ref[rs, :], lb, hg, sums_ref, masks_ref, sel, states, None, chunk)
        for h in range(HG_HEADS):
            y_ref[rs, h * HG_D:(h + 1) * HG_D] = outs[h]
            state[h] = new_states[0][h]

    @pl.when(pl.program_id(1) == pl.num_programs(1) - 1)
    def _():
        sout_ref[0] = state[...]


def _hgrn_prompt(zb, lb_logits, hgrn_g, s0, *, layer, nb, tc, chunk):
    n = zb.shape[0]
    nt = n // (nb * tc)
    sums, masks = _hgrn_consts(chunk, chunk)
    sel = jnp.ones((chunk, HG_D), BF16)
    tok = lambda b, i: (b * nt + i, 0)
    kern = functools.partial(_hgrn_prompt_kernel, layer=layer, tc=tc, chunk=chunk)
    return pl.pallas_call(
        kern,
        grid=(nb, nt),
        in_specs=[
            pl.BlockSpec((tc, 4 * BW), tok),
            _resident(lb_logits.shape),
            _resident((1, BW)),
            _resident(sums.shape),
            _resident(masks.shape),
            _resident(sel.shape),
            pl.BlockSpec((1, HG_HEADS, HG_D, HG_D), lambda b, i: (b, 0, 0, 0)),
        ],
        out_specs=[
            pl.BlockSpec((tc, BW), tok),
            pl.BlockSpec((1, HG_HEADS, HG_D, HG_D), lambda b, i: (b, 0, 0, 0)),
        ],
        out_shape=[
            jax.ShapeDtypeStruct((n, BW), BF16),
            jax.ShapeDtypeStruct((nb, HG_HEADS, HG_D, HG_D), F32),
        ],
        scratch_shapes=[pltpu.VMEM((HG_HEADS, HG_D, HG_D), F32)],
        compiler_params=_cparams("arbitrary", "arbitrary"),
        name="hgrn_p",
    )(zb, lb_logits, hgrn_g, sums, masks, sel, s0)


def _hgrn_sample_kernel(z_ref, lbl_ref, hg_ref, sums_ref, masks_ref, sel_ref, s0_ref, y_ref, sout_ref, *,
                        layer, rows, seq):
    nseq = rows // seq
    lb = _hgrn_lb(lbl_ref, layer)
    ridx = lax.broadcasted_iota(jnp.int32, (rows, 1), 0)
    row_masks = [jnp.where((ridx >= j * seq) & (ridx < (j + 1) * seq), 1.0, 0.0) for j in range(nseq)]
    states = [[s0_ref[j, h] for h in range(HG_HEADS)] for j in range(nseq)]
    outs, new_states = _hgrn_chunk(z_ref[...], lb, hg_ref[...], sums_ref, masks_ref, sel_ref[...], states,
                                   row_masks, rows)
    for h in range(HG_HEADS):
        y_ref[:, h * HG_D:(h + 1) * HG_D] = outs[h]
        for j in range(nseq):
            sout_ref[j, h] = new_states[j][h]


def _hgrn_sample(zb, lb_logits, hgrn_g, s0, *, layer, seq, rows):
    n = zb.shape[0]
    nseq = rows // seq
    sums, masks = _hgrn_consts(rows, seq)
    sel = np.zeros((rows, nseq * HG_D), np.float32)
    for j in range(nseq):
        sel[j * seq:(j + 1) * seq, j * HG_D:(j + 1) * HG_D] = 1.0
    sel = jnp.asarray(sel, BF16)
    kern = functools.partial(_hgrn_sample_kernel, layer=layer, rows=rows, seq=seq)
    return pl.pallas_call(
        kern,
        grid=(n // rows,),
        in_specs=[
            pl.BlockSpec((rows, 4 * BW), lambda i: (i, 0)),
            _resident(lb_logits.shape),
            _resident((1, BW)),
            _resident(sums.shape),
            _resident(masks.shape),
            _resident(sel.shape),
            pl.BlockSpec((nseq, HG_HEADS, HG_D, HG_D), lambda i: (i, 0, 0, 0)),
        ],
        out_specs=[
            pl.BlockSpec((rows, BW), lambda i: (i, 0)),
            pl.BlockSpec((nseq, HG_HEADS, HG_D, HG_D), lambda i: (i, 0, 0, 0)),
        ],
        out_shape=[
            jax.ShapeDtypeStruct((n, BW), BF16),
            jax.ShapeDtypeStruct(s0.shape, F32),
        ],
        compiler_params=_cparams("arbitrary"),
        name="hgrn_s",
    )(zb, lb_logits, hgrn_g, sums, masks, sel, s0)


def _attn_prompt_kernel(qi_ref, kj_ref, q_ref, k_ref, v_ref, o_ref, m_sc, l_sc, acc_sc, *, tb):
    step = pl.program_id(1)
    i = qi_ref[step]
    j = kj_ref[step]

    @pl.when(j == 0)
    def _():
        m_sc[...] = jnp.full(m_sc.shape, NEG_BIG, F32)
        l_sc[...] = jnp.zeros(l_sc.shape, F32)
        acc_sc[...] = jnp.zeros(acc_sc.shape, F32)

    def update(diagonal):
        if diagonal:
            keep = (lax.broadcasted_iota(jnp.int32, (tb, tb), 1) <= lax.broadcasted_iota(jnp.int32, (tb, tb), 0))
        for h in range(C_HEADS):
            hs = slice(h * LANE, (h + 1) * LANE)
            s = _dot_nt(q_ref[:, hs], k_ref[:, hs])
            if diagonal:
                s = jnp.where(keep, s, NEG_BIG)
            m_prev = m_sc[h]
            m_new = jnp.maximum(m_prev, jnp.max(s, axis=-1, keepdims=True))
            alpha = jnp.exp2(m_prev - m_new)
            p = jnp.exp2(s - m_new[:, :1])
            l_sc[h] = alpha * l_sc[h] + jnp.sum(p, axis=-1, keepdims=True)
            acc_sc[h] = alpha * acc_sc[h] + _dot(_bf(p), v_ref[:, hs])
            m_sc[h] = m_new

    @pl.when(j < i)
    def _():
        update(False)

    @pl.when(j == i)
    def _():
        update(True)
        for h in range(0, C_HEADS, 2):
            o0 = acc_sc[h][:, :C_V] / l_sc[h][:, :C_V]
            o1 = acc_sc[h + 1][:, :C_V] / l_sc[h + 1][:, :C_V]
            o_ref[:, h * C_V:(h + 2) * C_V] = _bf(jnp.concatenate([o0, o1], axis=-1))


def _attn_prompt(q, k, v, *, nb, t, tb):
    n = q.shape[0]
    nblk = t // tb
    pairs = [(i, j) for i in range(nblk) for j in range(i + 1)]
    qi = jnp.asarray([p[0] for p in pairs], jnp.int32)
    kj = jnp.asarray([p[1] for p in pairs], jnp.int32)
    q_map = lambda b, s, qi, kj: (b * nblk + qi[s], 0)
    k_map = lambda b, s, qi, kj: (b * nblk + kj[s], 0)
    grid_spec = pltpu.PrefetchScalarGridSpec(
        num_scalar_prefetch=2,
        grid=(nb, len(pairs)),
        in_specs=[
            pl.BlockSpec((tb, C_HEADS * LANE), q_map),
            pl.BlockSpec((tb, C_HEADS * LANE), k_map),
            pl.BlockSpec((tb, C_HEADS * LANE), k_map),
        ],
        out_specs=pl.BlockSpec((tb, C_HEADS * C_V), q_map),
        scratch_shapes=[
            pltpu.VMEM((C_HEADS, tb, LANE), F32),
            pltpu.VMEM((C_HEADS, tb, LANE), F32),
            pltpu.VMEM((C_HEADS, tb, LANE), F32),
        ],
    )
    return pl.pallas_call(
        functools.partial(_attn_prompt_kernel, tb=tb),
        grid_spec=grid_spec,
        out_shape=jax.ShapeDtypeStruct((n, C_HEADS * C_V), BF16),
        compiler_params=_cparams("arbitrary", "arbitrary"),
        name="attn_p",
    )(qi, kj, q, k, v)


def _attn_sample_kernel(pt_ref, q_ref, kn_ref, lat_hbm, kr_hbm, o_ref, lat_buf, kr_buf, sem, *,
                        layer, pages, seq, key_chunk):
    b = pl.program_id(0)
    nb = pl.num_programs(0)
    slot = lax.rem(b, 2)
    rows = C_HEADS * seq

    def page_copies(seq_idx, slot_, with_source):
        copies = []
        for p in range(pages):
            page = pt_ref[seq_idx, p] if with_source else 0
            ks = pl.ds(p * PAGE, PAGE)
            copies.append(pltpu.make_async_copy(lat_hbm.at[layer, page], lat_buf.at[slot_, ks, :], sem.at[0, slot_]))
            copies.append(pltpu.make_async_copy(kr_hbm.at[layer, page], kr_buf.at[slot_, :, ks], sem.at[1, slot_]))
        return copies

    @pl.when(b == 0)
    def _():
        for c in page_copies(0, 0, True):
            c.start()

    for c in page_copies(b, slot, False):
        c.wait()
    nxt = jnp.minimum(b + 1, nb - 1)
    for c in page_copies(nxt, 1 - slot, True):
        c.start()

    q = q_ref[0]
    q_lat = q[:, :C_KVL]
    q_rope = q[:, C_KVL + ROPE_LANE:C_KVL + ROPE_LANE + C_ROPE]
    n_chunks = pages * PAGE // key_chunk
    lats, s = [], []
    for c in range(n_chunks):
        ks = pl.ds(c * key_chunk, key_chunk)
        lats.append(_bf(lat_buf[slot, ks, :]))
        s.append(_dot_nt(q_lat, lats[c]) + _dot(q_rope, _bf(kr_buf[slot, :, ks])))

    qf = q.astype(F32)
    kn = kn_ref[0].astype(F32)
    tpos = lax.broadcasted_iota(jnp.int32, (rows, 1), 0) % seq
    sn = [jnp.where(tpos >= t, jnp.sum(qf * kn[t:t + 1, :], axis=-1, keepdims=True), NEG_BIG) for t in range(seq)]

    m = jnp.max(s[0], axis=-1, keepdims=True)
    for c in range(1, n_chunks):
        m = jnp.maximum(m, jnp.max(s[c], axis=-1, keepdims=True))
    for t in range(seq):
        m = jnp.maximum(m, sn[t])

    l = jnp.zeros((rows, 1), F32)
    acc = jnp.zeros((rows, C_KVL), F32)
    for c in range(n_chunks):
        e = jnp.exp(s[c] - m)
        l = l + jnp.sum(e, axis=-1, keepdims=True)
        acc = acc + _dot(_bf(e), lats[c])
    for t in range(seq):
        e = jnp.exp(sn[t] - m)
        l = l + e
        acc = acc + e * kn[t:t + 1, :C_KVL]
    o_ref[0] = _bf(acc / l)

    @pl.when(b == nb - 1)
    def _():
        for c in page_copies(b, 1 - slot, False):
            c.wait()


def _attn_sample(page_table, q, k_new, cache_latent, cache_krope_t, *, layer, seq):
    nb, pages = page_table.shape
    rows = C_HEADS * seq
    keys = pages * PAGE
    kern = functools.partial(_attn_sample_kernel, layer=layer, pages=pages, seq=seq, key_chunk=min(keys, 2048))
    grid_spec = pltpu.PrefetchScalarGridSpec(
        num_scalar_prefetch=1,
        grid=(nb,),
        in_specs=[
            pl.BlockSpec((1, rows, QK_W), lambda b, pt: (b, 0, 0)),
            pl.BlockSpec((1, seq, QK_W), lambda b, pt: (b, 0, 0)),
            pl.BlockSpec(memory_space=pl.ANY),
            pl.BlockSpec(memory_space=pl.ANY),
        ],
        out_specs=pl.BlockSpec((1, rows, C_KVL), lambda b, pt: (b, 0, 0)),
        scratch_shapes=[
            pltpu.VMEM((2, keys, C_KVL), F32),
            pltpu.VMEM((2, C_ROPE, keys), F32),
            pltpu.SemaphoreType.DMA((2, 2)),
        ],
    )
    return pl.pallas_call(
        kern,
        grid_spec=grid_spec,
        out_shape=jax.ShapeDtypeStruct((nb, rows, C_KVL), BF16),
        compiler_params=_cparams("arbitrary"),
        name="attn_s",
    )(page_table, q, k_new, cache_latent, cache_krope_t)


def _merge_kernel(ya_ref, yb_ref, oc_ref, gt_ref, x_ref, g1_ref, wb_ref, wo_ref, *refs, latent_out):
    if latent_out:
        wv_ref, o_ref = refs
        yc = _bf(_dot(oc_ref[...], wv_ref[...]))
    else:
        (o_ref,) = refs
        yc = oc_ref[...]
    m = gt_ref[:, 0:D_MODEL] * _dot(ya_ref[...], wb_ref[0])
    m = m + gt_ref[:, D_MODEL:2 * D_MODEL] * _dot(yb_ref[...], wb_ref[1])
    m = m + gt_ref[:, 2 * D_MODEL:3 * D_MODEL] * _dot(yc, wb_ref[2])
    o_ref[...] = x_ref[...] + g1_ref[0] * _dot(_bf(m), wo_ref[...])


def _merge(ya, yb, oc, gt, x, g1, wb, wo, wv, *, nb, tm):
    n = x.shape[0]
    nt = n // (nb * tm)
    mod_rows = g1.shape[1]
    tok = lambda b, i: (b * nt + i, 0)
    extra = [] if wv is None else [wv]
    return pl.pallas_call(
        functools.partial(_merge_kernel, latent_out=wv is not None),
        grid=(nb, nt),
        in_specs=[
            pl.BlockSpec((tm, BW), tok),
            pl.BlockSpec((tm, BW), tok),
            pl.BlockSpec((tm, oc.shape[1]), tok),
            pl.BlockSpec((tm, 3 * D_MODEL), tok),
            pl.BlockSpec((tm, D_MODEL), tok),
            pl.BlockSpec((1, mod_rows, D_MODEL), lambda b, i: (b, 0, 0)),
            _resident(wb.shape),
            _resident(wo.shape),
        ] + [_resident(a.shape) for a in extra],
        out_specs=pl.BlockSpec((tm, D_MODEL), tok),
        out_shape=jax.ShapeDtypeStruct((n, D_MODEL), F32),
        compiler_params=_cparams("arbitrary", "arbitrary"),
        name="merge",
    )(ya, yb, oc, gt, x, g1, wb, wo, *extra)


def _conv_gate(up_a, up_b, prev_a, prev_b, cw_a, cw_b, cb_a, cb_b, keep1, keep2):
    def fill(rolled, keep, prev):
        n = keep.shape[0]
        if n == rolled.shape[0]:
            return jnp.where(keep, rolled, prev)
        return jnp.concatenate([jnp.where(keep, rolled[:n], prev), rolled[n:]], axis=0)

    def conv(up, prev, cw, cb):
        s1 = fill(pltpu.roll(up, 1, 0), keep1, prev[0])
        s2 = fill(pltpu.roll(up, 2, 0), keep2, prev[1])
        return cb + cw[0:1, :] * s2 + cw[1:2, :] * s1 + cw[2:3, :] * up

    return _bf(_silu(conv(up_a, prev_a, cw_a, cb_a)) * conv(up_b, prev_b, cw_b, cb_b))


def _ffn_up_prompt_kernel(x_ref, sc_ref, sh_ref, g_ref, w_ref, cw_ref, cb_ref, c0_ref, act_ref, cs_ref, carry, *, tm):
    @pl.when(pl.program_id(1) == 0)
    def _():
        carry[...] = c0_ref[0]

    hb = _bf(_norm_mod(x_ref[...], g_ref[...], sc_ref[0], sh_ref[0]))
    ridx = lax.broadcasted_iota(jnp.int32, (8, 1), 0)
    keep1 = ridx >= 1
    keep2 = ridx >= 2
    for c in range(N_FF_CHUNKS):
        ca = slice(c * FF_COL, (c + 1) * FF_COL)
        cb = slice(D_FF + c * FF_COL, D_FF + (c + 1) * FF_COL)
        up_a = _dot(hb, w_ref[:, ca])
        up_b = _dot(hb, w_ref[:, cb])
        old_a = carry[:, ca]
        old_b = carry[:, cb]
        prev_a = (old_a[7:8, :], jnp.where(ridx == 0, old_a[6:7, :], old_a[7:8, :]))
        prev_b = (old_b[7:8, :], jnp.where(ridx == 0, old_b[6:7, :], old_b[7:8, :]))
        act_ref[:, ca] = _conv_gate(up_a, up_b, prev_a, prev_b, cw_ref[:, ca], cw_ref[:, cb],
                                    cb_ref[:, ca], cb_ref[:, cb], keep1, keep2)
        carry[:, ca] = up_a[tm - 8:, :]
        carry[:, cb] = up_b[tm - 8:, :]

    @pl.when(pl.program_id(1) == pl.num_programs(1) - 1)
    def _():
        cs_ref[0] = carry[...]


def _ffn_up_prompt(x, sc, sh, g, w_up, conv_w, conv_b, conv0, *, nb, tm):
    n = x.shape[0]
    nt = n // (nb * tm)
    tok = lambda b, i: (b * nt + i, 0)
    kern = functools.partial(_ffn_up_prompt_kernel, tm=tm)
    return pl.pallas_call(
        kern,
        grid=(nb, nt),
        in_specs=[
            pl.BlockSpec((tm, D_MODEL), tok),
            pl.BlockSpec((1, 1, D_MODEL), lambda b, i: (b, 0, 0)),
            pl.BlockSpec((1, 1, D_MODEL), lambda b, i: (b, 0, 0)),
            _resident((1, D_MODEL)),
            _resident((D_MODEL, 2 * D_FF)),
            _resident((3, 2 * D_FF)),
            _resident((1, 2 * D_FF)),
            pl.BlockSpec((1, 8, 2 * D_FF), lambda b, i: (b, 0, 0)),
        ],
        out_specs=[
            pl.BlockSpec((tm, D_FF), tok),
            pl.BlockSpec((1, 8, 2 * D_FF), lambda b, i: (b, 0, 0)),
        ],
        out_shape=[
            jax.ShapeDtypeStruct((n, D_FF), BF16),
            jax.ShapeDtypeStruct((nb, 8, 2 * D_FF), F32),
        ],
        scratch_shapes=[pltpu.VMEM((8, 2 * D_FF), F32)],
        compiler_params=_cparams("arbitrary", "arbitrary"),
        name="ffn_up_p",
    )(x, sc, sh, g, w_up, conv_w, conv_b, conv0)


def _ffn_up_sample_kernel(x_ref, sc_ref, sh_ref, g_ref, wa_ref, wb_ref, cwa_ref, cwb_ref, cba_ref, cbb_ref,
                          pa_ref, pb_ref, act_ref, upa_ref, upb_ref, *, n, seq):
    hb = _bf(_norm_mod(x_ref[...], g_ref[...], sc_ref[0], sh_ref[0]))
    tpos = lax.broadcasted_iota(jnp.int32, (n, 1), 0) % seq
    up_a = _dot(hb, wa_ref[...])
    up_b = _dot(hb, wb_ref[...])
    upa_ref[...] = up_a
    upb_ref[...] = up_b
    act_ref[...] = _conv_gate(up_a, up_b, (pa_ref[0], pa_ref[1]), (pb_ref[0], pb_ref[1]), cwa_ref[...], cwb_ref[...],
                              cba_ref[...], cbb_ref[...], tpos >= 1, tpos >= 2)


def _ffn_up_sample(x, sc, sh, g, w_up, conv_w, conv_b, prev, *, seq):
    n = x.shape[0]
    col_a = lambda j: (0, j)
    col_b = lambda j: (0, N_FF_CHUNKS + j)
    kern = functools.partial(_ffn_up_sample_kernel, n=n, seq=seq)
    return pl.pallas_call(
        kern,
        grid=(N_FF_CHUNKS,),
        in_specs=[
            _resident((n, D_MODEL)),
            _resident((1, n, D_MODEL)),
            _resident((1, n, D_MODEL)),
            _resident((1, D_MODEL)),
            pl.BlockSpec((D_MODEL, FF_COL), col_a),
            pl.BlockSpec((D_MODEL, FF_COL), col_b),
            pl.BlockSpec((3, FF_COL), col_a),
            pl.BlockSpec((3, FF_COL), col_b),
            pl.BlockSpec((1, FF_COL), col_a),
            pl.BlockSpec((1, FF_COL), col_b),
            pl.BlockSpec((2, n, FF_COL), lambda j: (0, 0, j)),
            pl.BlockSpec((2, n, FF_COL), lambda j: (0, 0, N_FF_CHUNKS + j)),
        ],
        out_specs=[
            pl.BlockSpec((n, FF_COL), col_a),
            pl.BlockSpec((n, FF_COL), col_a),
            pl.BlockSpec((n, FF_COL), col_a),
        ],
        out_shape=[
            jax.ShapeDtypeStruct((n, D_FF), BF16),
            jax.ShapeDtypeStruct((n, D_FF), F32),
            jax.ShapeDtypeStruct((n, D_FF), F32),
        ],
        compiler_params=_cparams("arbitrary"),
        name="ffn_up_s",
    )(x, sc, sh, g, w_up, w_up, conv_w, conv_w, conv_b, conv_b, prev, prev)


def _ffn_down_kernel(act_ref, x_ref, g2_ref, w_ref, fg_ref, o_ref, *, final):
    xn = x_ref[...] + g2_ref[0] * _dot(act_ref[...], w_ref[...])
    o_ref[...] = _rms(xn, fg_ref[...]) if final else xn


def _ffn_down(act, x, g2, w_down, final_g, *, nb, tm, final):
    n = x.shape[0]
    nt = n // (nb * tm)
    mod_rows = g2.shape[1]
    tok = lambda b, i: (b * nt + i, 0)
    return pl.pallas_call(
        functools.partial(_ffn_down_kernel, final=final),
        grid=(nb, nt),
        in_specs=[
            pl.BlockSpec((tm, D_FF), tok),
            pl.BlockSpec((tm, D_MODEL), tok),
            pl.BlockSpec((1, mod_rows, D_MODEL), lambda b, i: (b, 0, 0)),
            _resident((D_FF, D_MODEL)),
            _resident((1, D_MODEL)),
        ],
        out_specs=pl.BlockSpec((tm, D_MODEL), tok),
        out_shape=jax.ShapeDtypeStruct((n, D_MODEL), F32),
        compiler_params=_cparams("arbitrary", "arbitrary"),
        name="ffn_down",
    )(act, x, g2, w_down, final_g)


def _rope_tables(pos):
    half = C_ROPE // 2
    inv = ROPE_THETA ** (-jnp.arange(half, dtype=F32) / half)
    ang = pos.astype(F32)[:, None] * inv[None, :]
    cos, sin = jnp.cos(ang), jnp.sin(ang)
    return _rope_block(jnp.concatenate([cos, cos], -1)), _rope_block(jnp.concatenate([-sin, sin], -1))


def _rope_block(a):
    lead = a.shape[:-1]
    return jnp.concatenate([jnp.zeros(lead + (ROPE_LANE,), a.dtype), a,
                            jnp.zeros(lead + (LANE - ROPE_LANE - C_ROPE,), a.dtype)], axis=-1)


def _swap_halves(w):
    half = C_ROPE // 2
    return jnp.concatenate([w[..., half:], w[..., :half]], axis=-1)


def _layer_weights(l, w_in, w_uq, w_ukv, w_branch, w_o, w_up, w_down):
    wi = w_in[l]
    o = 0
    segs = []
    for s in (BW, BW, BW, BW, BW, BW, C_QL, C_KVL, C_ROPE, 3 * D_MODEL):
        segs.append(wi[:, o:o + s])
        o += s
    ua, va, qb, fb, ib, gb, cq, ckv, krr, gt = segs
    w_in_l = _bf(jnp.concatenate([ua, va, qb, fb, ib, gb, gt, cq, ckv, _rope_block(krr),
                                  _rope_block(_swap_halves(krr))], axis=1))

    def head_blocks(a):
        pad = jnp.zeros(a.shape[:-1] + (LANE - a.shape[-1],), F32)
        return jnp.concatenate([a, pad], -1).reshape(a.shape[0], C_HEADS * LANE)

    wq = w_uq[l].reshape(C_QL, C_HEADS, C_NOPE + C_ROPE)
    q_nope, q_rope = wq[..., :C_NOPE], wq[..., C_NOPE:]
    wqn = head_blocks(q_nope)
    wqa = _rope_block(q_rope).reshape(C_QL, C_HEADS * LANE)
    wqb = _rope_block(_swap_halves(q_rope)).reshape(C_QL, C_HEADS * LANE)

    wkv = w_ukv[l].reshape(C_KVL, C_HEADS, C_NOPE + C_V)
    wk = jnp.transpose(wkv[..., :C_NOPE], (1, 2, 0))
    wk = jnp.concatenate([wk, jnp.zeros((C_HEADS, LANE - C_NOPE, C_KVL), F32)], axis=1)
    wv = jnp.transpose(wkv[..., C_NOPE:], (1, 0, 2))
    eye = jnp.eye(C_HEADS, dtype=F32)
    wv_bd = (wv[:, :, None, :] * eye[:, None, :, None]).reshape(C_HEADS * C_KVL, C_HEADS * C_V)
    attn_sample = [_bf(wqn), _bf(wqa), _bf(wqb), _bf(wk)]
    attn_prompt = [_bf(wqn + wqa), _bf(wqb), _bf(head_blocks(wkv[..., :C_NOPE])), _bf(head_blocks(wkv[..., C_NOPE:]))]
    return dict(w_in=w_in_l, attn_s=attn_sample, attn_p=attn_prompt, wv=_bf(wv_bd), wb=_bf(w_branch[l]),
                wo=_bf(w_o[l]), w_up=_bf(w_up[l]), w_down=_bf(w_down[l]))


def _sample_mix_tables(w_s_l, b_s_l, seq):
    t = np.arange(8) % seq
    coef = []
    for j in range(seq):
        src = t - j
        wj = w_s_l[:, t, np.maximum(src, 0)]
        wj = jnp.where(jnp.asarray(src >= 0)[None, :], wj, 0.0)
        coef.append(jnp.repeat(wj.T, LANE, axis=1))
    bias = jnp.repeat(b_s_l[:, t].T, LANE, axis=1)
    return jnp.stack(coef), bias


def _trunk(x, mods, lw, *, sample, nb, t, pos, s0, conv0, cache_latent, cache_krope, page_table,
           w_s, b_s, lnv_g, lnv_b, lb_logits, hgrn_g, q_a_g, kv_a_g, norm1_g, norm2_g, final_g, conv_w, conv_b):
    n = nb * t
    cos, sin = _rope_tables(pos)
    lat_out, kr_out, gv_out, s_out, conv_out = [], [], [], [], []
    for l in range(DEPTH):
        sh1, sc1, g1, sh2, sc2, g2 = mods[l]
        w = lw[l]
        row = lambda a: a[l].reshape(1, -1)
        if sample:
            ws, bs = _sample_mix_tables(w_s[l], b_s[l], t)
            kw = dict(nb=1, tm=n, gv_rows=n, sample=True)
        else:
            ws, bs = w_s[l], jnp.repeat(b_s[l].T, LANE, axis=1)
            kw = dict(nb=nb, tm=256, gv_rows=A_CHUNK, sample=False)
        outs = _in_proj(
            x, sc1, sh1, row(norm1_g), w["w_in"], ws, bs, row(lnv_g), row(lnv_b), row(q_a_g), row(kv_a_g),
            cos, sin, w["attn_s"] if sample else w["attn_p"], **kw)
        if sample:
            ya, gv, zb, gt, q, k, lat, kr = outs
            yb, s_new = _hgrn_sample(zb, lb_logits, row(hgrn_g), s0[l], layer=l, seq=t, rows=64)
            q_b = q.reshape(nb, t, C_HEADS, QK_W).transpose(0, 2, 1, 3).reshape(nb, C_HEADS * t, QK_W)
            oc = _attn_sample(page_table, q_b, k.reshape(nb, t, QK_W), cache_latent, cache_krope, layer=l, seq=t)
            oc = oc.reshape(nb, C_HEADS, t, C_KVL).transpose(0, 2, 1, 3).reshape(n, C_HEADS * C_KVL)
            mkw = dict(nb=1, tm=n)
            wv = w["wv"]
        else:
            ya, gv, zb, gt, q, k, v, lat, kr = outs
            yb, s_new = _hgrn_prompt(zb, lb_logits, row(hgrn_g), s0[l], layer=l, nb=nb, tc=512, chunk=128)
            oc = _attn_prompt(q, k, v, nb=nb, t=t, tb=512)
            mkw = dict(nb=nb, tm=512)
            wv = None
        x = _merge(ya, yb, oc, gt, x, g1, w["wb"], w["wo"], wv, **mkw)
        if sample:
            c0 = conv0[l]
            zero = jnp.zeros((nb, t - 1, 2 * D_FF), F32)
            prev1 = jnp.concatenate([c0[:, 1:2], zero], axis=1)
            prev2 = jnp.concatenate([c0[:, 0:2], zero[:, 1:]], axis=1)
            prev = jnp.stack([prev1.reshape(n, 2 * D_FF), prev2.reshape(n, 2 * D_FF)])
            act, up_a, up_b = _ffn_up_sample(x, sc2, sh2, row(norm2_g), w["w_up"], conv_w[l], row(conv_b), prev, seq=t)
            up = jnp.concatenate([up_a, up_b], axis=-1).reshape(nb, t, 2 * D_FF)
            conv_out.append(up[:, t - 2:, :])
        else:
            c0 = jnp.concatenate([jnp.zeros((nb, 6, 2 * D_FF), F32), conv0[l]], axis=1)
            act, cs = _ffn_up_prompt(x, sc2, sh2, row(norm2_g), w["w_up"], conv_w[l], row(conv_b), c0, nb=nb, tm=512)
            conv_out.append(cs[:, 6:, :])
        x = _ffn_down(act, x, g2, w["w_down"], final_g.reshape(1, -1), final=(l == DEPTH - 1), **mkw)
        lat_out.append(lat.reshape(nb, t, C_KVL))
        kr_out.append(kr.reshape(nb, t, C_ROPE))
        gv_out.append(gv.reshape(nb, -1, BW))
        s_out.append(s_new)
    return (x.reshape(nb, t, D_MODEL), jnp.stack(lat_out), jnp.stack(kr_out), jnp.stack(gv_out), jnp.stack(s_out),
            jnp.stack(conv_out))


def kernel(x_prompt, x_sample, c_prompt, c_sample, cache_latent, cache_krope, state_hgrn, state_conv, page_table,
           norm1_g, norm2_g, final_g, w_ada, b_ada, w_in, w_s, b_s, lnv_g, lnv_b, lb_logits, hgrn_g, q_a_g, kv_a_g,
           w_uq, w_ukv, w_branch, w_o, w_up, conv_w, conv_b, w_down):
    bp, tp, _ = x_prompt.shape
    bs_, ts, _ = x_sample.shape
    past = page_table.shape[1] * cache_latent.shape[2]

    rows = bp + bs_
    rows_pad = -(-rows // 8) * 8
    c_all = jnp.concatenate([c_prompt, c_sample, jnp.zeros((rows_pad - rows, D_MODEL), F32)], axis=0)
    mod = _ada(c_all, w_ada, b_ada)
    mods_p, mods_s = [], []
    for l in range(DEPTH):
        chunks = [mod[l, :, i * D_MODEL:(i + 1) * D_MODEL] for i in range(6)]
        mods_p.append([m[:bp].reshape(bp, 1, D_MODEL) for m in chunks])
        mods_s.append([jnp.repeat(m[bp:rows], ts, axis=0).reshape(1, bs_ * ts, D_MODEL) for m in chunks])

    lw = [_layer_weights(l, w_in, w_uq, w_ukv, w_branch, w_o, w_up, w_down) for l in range(DEPTH)]
    shared = dict(w_s=w_s, b_s=b_s, lnv_g=lnv_g, lnv_b=lnv_b, lb_logits=lb_logits, hgrn_g=hgrn_g, q_a_g=q_a_g,
                  kv_a_g=kv_a_g, norm1_g=norm1_g, norm2_g=norm2_g, final_g=final_g, conv_w=conv_w, conv_b=conv_b)

    s0_p = jnp.zeros((DEPTH, bp, HG_HEADS, HG_D, HG_D), F32)
    conv0_p = jnp.zeros((DEPTH, bp, 2, 2 * D_FF), F32)
    out_p = _trunk(x_prompt.reshape(bp * tp, D_MODEL), mods_p, lw, sample=False, nb=bp, t=tp, pos=jnp.arange(tp),
                   s0=s0_p, conv0=conv0_p, cache_latent=None, cache_krope=None, page_table=None, **shared)
    pos_s = jnp.tile(past + jnp.arange(ts), bs_)
    out_s = _trunk(x_sample.reshape(bs_ * ts, D_MODEL), mods_s, lw, sample=True, nb=bs_, t=ts, pos=pos_s,
                   s0=state_hgrn, conv0=state_conv, cache_latent=cache_latent,
                   cache_krope=jnp.swapaxes(cache_krope, 2, 3),
                   page_table=page_table, **shared)
    y_p, lat_p, kr_p, gv_p, hgrn_p, conv_p = out_p
    y_s, lat_s, kr_s, gv_s, hgrn_s, conv_s = out_s
    return (y_p, y_s, lat_p, kr_p, gv_p, hgrn_p, conv_p, lat_s, kr_s, gv_s, hgrn_s, conv_s)
```

```python
import functools
import math

import numpy as np
import jax
import jax.numpy as jnp
from jax import lax
from jax.experimental import pallas as pl
from jax.experimental.pallas import tpu as pltpu

F32 = jnp.float32
BF16 = jnp.bfloat16

D_MODEL = 1024
DEPTH = 4
PAGE = 128
EPS = 1e-6
NEG_BIG = -1e30
LB_FLOOR = 1e-30
BW = 512
A_CHUNK = 128
A_GROUPS = 4
HG_HEADS = 4
HG_D = 128
C_HEADS = 8
C_NOPE = 64
C_ROPE = 32
C_V = 64
C_QL = 384
C_KVL = 256
QK_W = 384
D_FF = 2816
ROPE_THETA = 10000.0
ATT_SCALE = (C_NOPE + C_ROPE) ** -0.5
LOG2E = math.log2(math.e)
ROPE_LANE = C_NOPE

O_UA, O_VA, O_B, O_CQ, O_CKV, O_KRA, O_KRB, O_GT, W_IN_COLS = 0, 512, 1024, 3072, 3456, 3712, 3840, 3968, 7040

VMEM_LIMIT_BYTES = 58 * 1024 * 1024
LANE = 128
FF_COL = 256
N_FF_CHUNKS = D_FF // FF_COL


def _cparams(*sem):
    return pltpu.CompilerParams(dimension_semantics=sem, vmem_limit_bytes=VMEM_LIMIT_BYTES)


def _resident(shape):
    nd = len(shape)
    return pl.BlockSpec(shape, lambda *_: (0,) * nd, pipeline_mode=pl.Buffered(1))


def _resident_layer(stacked, layer):
    shape = stacked.shape[1:]
    return pl.BlockSpec((None,) + shape, lambda *_: (layer,) + (0,) * len(shape), pipeline_mode=pl.Buffered(1))


def _bf(x):
    return x.astype(BF16)


def _dot(a, b):
    return jnp.dot(a, b, preferred_element_type=F32)


def _dot_nt(a, b):
    return lax.dot_general(a, b, (((1,), (1,)), ((), ())), preferred_element_type=F32)


def _dot_tn(a, b):
    return lax.dot_general(a, b, (((0,), (0,)), ((), ())), preferred_element_type=F32)


def _sigmoid(x):
    return 1.0 / (1.0 + jnp.exp(-x))


def _silu(x):
    return x * _sigmoid(x)


def _gelu_tanh(x):
    return 0.5 * x * (1.0 + jnp.tanh(math.sqrt(2.0 / math.pi) * (x + 0.044715 * (x * x * x))))


def _softplus(x):
    return jnp.maximum(x, 0.0) + jnp.log1p(jnp.exp(-jnp.abs(x)))


def _logaddexp(a, b):
    return jnp.maximum(a, b) + jnp.log1p(jnp.exp(-jnp.abs(a - b)))


def _rms(x, g):
    return x * lax.rsqrt(jnp.mean(x * x, axis=-1, keepdims=True) + EPS) * g


def _norm_mod(x, g, sc, sh):
    return _rms(x, g) * (1.0 + sc) + sh


def _split_bf16(x):
    hi = _bf(x)
    lo = _bf(x - hi.astype(F32))
    return hi, lo


def _ada_kernel(c_ref, w_ref, b_ref, o_ref):
    c = c_ref[...]
    o_ref[0] = _dot(_bf(_silu(c)), _bf(w_ref[0])) + b_ref[0]


def _ada(c_all, w_ada, b_ada):
    rows = c_all.shape[0]
    tn = 1536
    return pl.pallas_call(
        _ada_kernel,
        grid=(DEPTH, 6 * D_MODEL // tn),
        in_specs=[
            pl.BlockSpec((rows, D_MODEL), lambda l, j: (0, 0)),
            pl.BlockSpec((1, D_MODEL, tn), lambda l, j: (l, 0, j)),
            pl.BlockSpec((1, 1, tn), lambda l, j: (l, 0, j)),
        ],
        out_specs=pl.BlockSpec((1, rows, tn), lambda l, j: (l, 0, j)),
        out_shape=jax.ShapeDtypeStruct((DEPTH, rows, 6 * D_MODEL), F32),
        compiler_params=_cparams("arbitrary", "arbitrary"),
        name="ada",
    )(c_all, w_ada, b_ada.reshape(DEPTH, 1, 6 * D_MODEL))


def _in_proj_kernel(x_ref, sc_ref, sh_ref, g_ref, w_ref, ws_ref, bs_ref, lng_ref, lnb_ref, qag_ref, kvg_ref,
                    cos_ref, sin_ref, *refs, tm, gv_rows, sample, n_unused):
    refs = refs[:4] + refs[4 + n_unused:]
    if sample:
        wqn_ref, wqa_ref, wqb_ref, wk_ref, ya_ref, gv_ref, zb_ref, gt_ref, q_ref, k_ref, lat_ref, kr_ref = refs
    else:
        wq1_ref, wqb_ref, wkn_ref, wvv_ref, ya_ref, gv_ref, zb_ref, gt_ref, q_ref, k_ref, v_ref, lat_ref, kr_ref = refs
    hb = _bf(_norm_mod(x_ref[...], g_ref[...], sc_ref[0], sh_ref[0]))

    u = _gelu_tanh(_dot(hb, w_ref[:, O_UA:O_UA + BW]))
    va = _gelu_tanh(_dot(hb, w_ref[:, O_VA:O_VA + BW]))
    mu = jnp.mean(va, axis=-1, keepdims=True)
    var = jnp.mean(jnp.square(va - mu), axis=-1, keepdims=True)
    v = (va - mu) * lax.rsqrt(var + EPS) * lng_ref[...] + lnb_ref[...]
    if sample:
        v3 = v.reshape(tm // 8, 8, BW)
        mixed = v3 * ws_ref[0][None] + bs_ref[...][None]
        for j in range(1, 4):
            vj = pltpu.roll(v, j, 0).reshape(tm // 8, 8, BW)
            mixed = mixed + vj * ws_ref[j][None]
        ya_ref[...] = _bf(u * mixed.reshape(tm, BW))
    else:
        row = lax.broadcasted_iota(jnp.int32, (A_CHUNK, A_CHUNK), 0)
        col = lax.broadcasted_iota(jnp.int32, (A_CHUNK, A_CHUNK), 1)
        wc = [_bf(jnp.where(col <= row, ws_ref[g], 0.0)) for g in range(A_GROUPS)]
        for c in range(tm // A_CHUNK):
            rs = slice(c * A_CHUNK, (c + 1) * A_CHUNK)
            for g in range(A_GROUPS):
                cs = slice(g * LANE, (g + 1) * LANE)
                mixed = _dot(wc[g], _bf(v[rs, cs])) + bs_ref[:, cs]
                ya_ref[rs, cs] = _bf(u[rs, cs] * mixed)

    @pl.when(pl.program_id(1) == pl.num_programs(1) - 1)
    def _():
        gv_ref[0] = v[tm - gv_rows:, :]

    zb_ref[...] = _dot(hb, w_ref[:, O_B:O_B + 4 * BW])
    gt_ref[...] = _sigmoid(_dot(hb, w_ref[:, O_GT:O_GT + 3 * D_MODEL]))

    cos = cos_ref[...]
    sin = sin_ref[...]
    qn = _bf(_rms(_dot(hb, w_ref[:, O_CQ:O_CQ + C_QL]), qag_ref[...]))
    lat = _rms(_dot(hb, w_ref[:, O_CKV:O_CKV + C_KVL]), kvg_ref[...])
    kr = _dot(hb, w_ref[:, O_KRA:O_KRA + LANE]) * cos + _dot(hb, w_ref[:, O_KRB:O_KRB + LANE]) * sin
    lat_ref[...] = lat
    kr_ref[...] = kr[:, ROPE_LANE:ROPE_LANE + C_ROPE]
    if sample:
        for h in range(C_HEADS):
            hs = slice(h * LANE, (h + 1) * LANE)
            q_nope = _bf(_dot(qn, wqn_ref[:, hs]))
            q_lat = _dot(q_nope, wk_ref[h])
            q_rope = _dot(qn, wqa_ref[:, hs]) * cos + _dot(qn, wqb_ref[:, hs]) * sin
            q_ref[:, h * QK_W:h * QK_W + C_KVL] = _bf(q_lat * ATT_SCALE)
            q_ref[:, h * QK_W + C_KVL:(h + 1) * QK_W] = _bf(q_rope * ATT_SCALE)
        k_ref[:, :C_KVL] = _bf(lat)
        k_ref[:, C_KVL:] = _bf(kr)
    else:
        lane = lax.broadcasted_iota(jnp.int32, (1, LANE), 1)
        cos_q = cos + jnp.where(lane < C_NOPE, 1.0, 0.0)
        latb = _bf(lat)
        for h in range(C_HEADS):
            hs = slice(h * LANE, (h + 1) * LANE)
            q = _dot(qn, wq1_ref[:, hs]) * cos_q + _dot(qn, wqb_ref[:, hs]) * sin
            q_ref[:, hs] = _bf(q * (ATT_SCALE * LOG2E))
            k_ref[:, hs] = _bf(_dot(latb, wkn_ref[:, hs]) + kr)
        v_ref[...] = _bf(_dot(latb, wvv_ref[...]))


def _in_proj(x, sc, sh, g, w, ws, bs, lng, lnb, qag, kvg, cos, sin, attn_w, stacks, *, layer, nb, tm, gv_rows, sample):
    n = x.shape[0]
    nt = n // (nb * tm)
    mod_rows = sc.shape[1]
    tok = lambda b, i: (b * nt + i, 0)
    tok_l = lambda b, i: (layer, b * nt + i, 0)
    stacks = list(stacks)
    kern = functools.partial(_in_proj_kernel, tm=tm, gv_rows=gv_rows, sample=sample, n_unused=len(stacks))
    if sample:
        qkv_specs = [pl.BlockSpec((tm, C_HEADS * QK_W), tok), pl.BlockSpec((tm, QK_W), tok)]
        qkv_shapes = [jax.ShapeDtypeStruct((n, C_HEADS * QK_W), BF16), jax.ShapeDtypeStruct((n, QK_W), BF16)]
    else:
        qkv_specs = [pl.BlockSpec((tm, C_HEADS * LANE), tok)] * 3
        qkv_shapes = [jax.ShapeDtypeStruct((n, C_HEADS * LANE), BF16)] * 3
    n_in = 13 + len(attn_w)
    n_out = 6 + len(qkv_specs)
    return pl.pallas_call(
        kern,
        grid=(nb, nt),
        in_specs=[
            pl.BlockSpec((tm, D_MODEL), tok),
            pl.BlockSpec((1, mod_rows, D_MODEL), lambda b, i: (b, 0, 0)),
            pl.BlockSpec((1, mod_rows, D_MODEL), lambda b, i: (b, 0, 0)),
            _resident((1, D_MODEL)),
            _resident_layer(w, layer),
            _resident(ws.shape),
            _resident(bs.shape),
            _resident((1, BW)),
            _resident((1, BW)),
            _resident((1, C_QL)),
            _resident((1, C_KVL)),
            pl.BlockSpec((tm, LANE), lambda b, i: (i, 0)),
            pl.BlockSpec((tm, LANE), lambda b, i: (i, 0)),
        ] + [_resident_layer(a, layer) for a in attn_w] + [pl.BlockSpec(memory_space=pl.ANY)] * len(stacks),
        out_specs=[
            pl.BlockSpec((tm, BW), tok),
            pl.BlockSpec((None, 1, gv_rows, BW), lambda b, i: (layer, b, 0, 0)),
            pl.BlockSpec((tm, 4 * BW), tok),
            pl.BlockSpec((tm, 3 * D_MODEL), tok),
        ] + qkv_specs + [
            pl.BlockSpec((None, tm, C_KVL), tok_l),
            pl.BlockSpec((None, tm, C_ROPE), tok_l),
        ],
        out_shape=[
            jax.ShapeDtypeStruct((n, BW), BF16),
            jax.ShapeDtypeStruct((DEPTH, nb, gv_rows, BW), F32),
            jax.ShapeDtypeStruct((n, 4 * BW), F32),
            jax.ShapeDtypeStruct((n, 3 * D_MODEL), F32),
        ] + qkv_shapes + [
            jax.ShapeDtypeStruct((DEPTH, n, C_KVL), F32),
            jax.ShapeDtypeStruct((DEPTH, n, C_ROPE), F32),
        ],
        input_output_aliases={n_in: 1, n_in + 1: n_out - 2, n_in + 2: n_out - 1},
        compiler_params=_cparams("arbitrary", "arbitrary"),
        name="in_proj_s" if sample else "in_proj_p",
    )(x, sc, sh, g, w, ws, bs, lng, lnb, qag, kvg, cos, sin, *attn_w, *stacks)


def _hgrn_consts(chunk, seg):
    t = np.arange(chunk)
    u = t[None, :]
    tt = t[:, None]
    same_seg = (t // seg)[:, None] == (t // seg)[None, :]
    blocks = [same_seg & (u <= tt), same_seg & (u > tt)]
    masks = [np.eye(chunk, dtype=bool)]
    b = seg // 2
    while b >= 1:
        pos = t % (2 * b)
        p = t - pos
        m = (p + b - 1)[:, None]
        second = pos >= b
        blocks.append(np.where(second[:, None], (u > m) & (u <= tt), (u > tt) & (u <= m)))
        masks.append((p[:, None] == p[None, :]) & second[:, None] & (~second)[None, :])
        b //= 2
    sums = np.concatenate(blocks, 0).astype(np.float32)
    return jnp.asarray(sums, BF16), jnp.asarray(np.stack(masks).astype(np.float32))


def _hgrn_lb(lbl_ref, layer):
    logits = lbl_ref[...]
    e = jnp.exp(logits - jnp.max(logits, axis=0, keepdims=True))
    p = e / jnp.sum(e, axis=0, keepdims=True)
    lb = jnp.zeros((1, HG_HEADS * HG_D), F32)
    for i in range(1, layer + 1):
        lb = lb + p[i:i + 1, :]
    return lb


def _hgrn_chunk(zb, lb, hg, sums_ref, masks_ref, sel, states, row_masks, chunk):
    nlev = masks_ref.shape[0] - 1
    q = _silu(zb[:, 0:BW])
    zf = zb[:, BW:2 * BW]
    vb = zb[:, 2 * BW:3 * BW]
    gate = _silu(zb[:, 3 * BW:4 * BW])
    log_lb = jnp.log(jnp.maximum(lb, LB_FLOOR))
    logf = _logaddexp(log_lb, jnp.log1p(-lb) - _softplus(-zf))
    k = (1.0 - lb) * _sigmoid(-zf)

    parts = _split_bf16(logf)
    sums = sums_ref[...]
    x = _dot(sums, parts[0]) + _dot(sums, parts[1])
    e = jnp.exp(x)
    e_cum = e[0:chunk]
    e_end = e[chunk:2 * chunk]
    nseq = len(states)

    outs = []
    new_states = [[None] * HG_HEADS for _ in range(nseq)]
    for h in range(HG_HEADS):
        hs = slice(h * HG_D, (h + 1) * HG_D)
        qh, kh, vh = q[:, hs], k[:, hs], _bf(vb[:, hs])
        scores = masks_ref[0] * _dot_nt(_bf(qh), _bf(kh))
        for lev in range(nlev):
            el = e[(2 + lev) * chunk:(3 + lev) * chunk, hs]
            scores = scores + masks_ref[1 + lev] * _dot_nt(_bf(qh * el), _bf(kh * el))
        o = _dot(_bf(scores), vh)
        q_in = _bf(qh * e_cum[:, hs])
        k_out = kh * e_end[:, hs]
        decay = jnp.exp(_dot_tn(parts[0][:, hs], sel) + _dot_tn(parts[1][:, hs], sel))
        for j in range(nseq):
            s_prev = states[j][h]
            if nseq == 1:
                o = o + _dot(q_in, _bf(s_prev))
                kj = _bf(k_out)
            else:
                o = o + row_masks[j] * _dot(q_in, _bf(s_prev))
                kj = _bf(k_out * row_masks[j])
            new_states[j][h] = decay[:, j * HG_D:(j + 1) * HG_D] * s_prev + _dot_tn(kj, vh)
        y = o * lax.rsqrt(jnp.mean(o * o, axis=-1, keepdims=True) + EPS) * hg[:, hs]
        outs.append(_bf(y * gate[:, hs]))
    return outs, new_states


def _hgrn_prompt_kernel(z_ref, lbl_ref, hg_ref, sums_ref, masks_ref, sel_ref, s0_ref, y_ref, sout_ref, state, *,
                        layer, tc, chunk):
    @pl.when(pl.program_id(1) == 0)
    def _():
        state[...] = s0_ref[0]

    lb = _hgrn_lb(lbl_ref, layer)
    hg = hg_ref[...]
    sel = sel_ref[...]
    for c in range(tc // chunk):
        rs = slice(c * chunk, (c + 1) * chunk)
        states = [[state[h] for h in range(HG_HEADS)]]
        outs, new_states = _hgrn_chunk(z_ref[rs, :], lb, hg, sums_ref, masks_ref, sel, states, None, chunk)
        for h in range(HG_HEADS):
            y_ref[rs, h * HG_D:(h + 1) * HG_D] = outs[h]
            state[h] = new_states[0][h]

    @pl.when(pl.program_id(1) == pl.num_programs(1) - 1)
    def _():
        sout_ref[0] = state[...]


def _hgrn_prompt(zb, lb_logits, hgrn_g, s0, *, layer, nb, tc, chunk):
    n = zb.shape[0]
    nt = n // (nb * tc)
    sums, masks = _hgrn_consts(chunk, chunk)
    sel = jnp.ones((chunk, HG_D), BF16)
    tok = lambda b, i: (b * nt + i, 0)
    kern = functools.partial(_hgrn_prompt_kernel, layer=layer, tc=tc, chunk=chunk)
    return pl.pallas_call(
        kern,
        grid=(nb, nt),
        in_specs=[
            pl.BlockSpec((tc, 4 * BW), tok),
            _resident(lb_logits.shape),
            _resident((1, BW)),
            _resident(sums.shape),
            _resident(masks.shape),
            _resident(sel.shape),
            pl.BlockSpec((1, HG_HEADS, HG_D, HG_D), lambda b, i: (b, 0, 0, 0)),
        ],
        out_specs=[
            pl.BlockSpec((tc, BW), tok),
            pl.BlockSpec((1, HG_HEADS, HG_D, HG_D), lambda b, i: (b, 0, 0, 0)),
        ],
        out_shape=[
            jax.ShapeDtypeStruct((n, BW), BF16),
            jax.ShapeDtypeStruct((nb, HG_HEADS, HG_D, HG_D), F32),
        ],
        scratch_shapes=[pltpu.VMEM((HG_HEADS, HG_D, HG_D), F32)],
        compiler_params=_cparams("arbitrary", "arbitrary"),
        name="hgrn_p",
    )(zb, lb_logits, hgrn_g, sums, masks, sel, s0)


def _hgrn_sample_kernel(z_ref, lbl_ref, hg_ref, sums_ref, masks_ref, sel_ref, s0_ref, *refs, layer, rows, seq):
    y_ref, sout_ref = refs[-2:]
    nseq = rows // seq
    lb = _hgrn_lb(lbl_ref, layer)
    ridx = lax.broadcasted_iota(jnp.int32, (rows, 1), 0)
    row_masks = [jnp.where((ridx >= j * seq) & (ridx < (j + 1) * seq), 1.0, 0.0) for j in range(nseq)]
    states = [[s0_ref[j, h] for h in range(HG_HEADS)] for j in range(nseq)]
    outs, new_states = _hgrn_chunk(z_ref[...], lb, hg_ref[...], sums_ref, masks_ref, sel_ref[...], states,
                                   row_masks, rows)
    for h in range(HG_HEADS):
        y_ref[:, h * HG_D:(h + 1) * HG_D] = outs[h]
        for j in range(nseq):
            sout_ref[j, h] = new_states[j][h]


def _hgrn_sample(zb, lb_logits, hgrn_g, s0_all, s_stack, *, layer, seq, rows):
    n = zb.shape[0]
    nseq = rows // seq
    sums, masks = _hgrn_consts(rows, seq)
    sel = np.zeros((rows, nseq * HG_D), np.float32)
    for j in range(nseq):
        sel[j * seq:(j + 1) * seq, j * HG_D:(j + 1) * HG_D] = 1.0
    sel = jnp.asarray(sel, BF16)
    kern = functools.partial(_hgrn_sample_kernel, layer=layer, rows=rows, seq=seq)
    state_spec = pl.BlockSpec((None, nseq, HG_HEADS, HG_D, HG_D), lambda i: (layer, i, 0, 0, 0))
    return pl.pallas_call(
        kern,
        grid=(n // rows,),
        in_specs=[
            pl.BlockSpec((rows, 4 * BW), lambda i: (i, 0)),
            _resident(lb_logits.shape),
            _resident((1, BW)),
            _resident(sums.shape),
            _resident(masks.shape),
            _resident(sel.shape),
            state_spec,
            pl.BlockSpec(memory_space=pl.ANY),
        ],
        out_specs=[pl.BlockSpec((rows, BW), lambda i: (i, 0)), state_spec],
        out_shape=[
            jax.ShapeDtypeStruct((n, BW), BF16),
            jax.ShapeDtypeStruct(s0_all.shape, F32),
        ],
        input_output_aliases={7: 1},
        compiler_params=_cparams("arbitrary"),
        name="hgrn_s",
    )(zb, lb_logits, hgrn_g, sums, masks, sel, s0_all, s_stack)


def _attn_prompt_kernel(qi_ref, kj_ref, q_ref, k_ref, v_ref, o_ref, m_sc, l_sc, acc_sc, *, tb):
    step = pl.program_id(1)
    i = qi_ref[step]
    j = kj_ref[step]

    @pl.when(j == 0)
    def _():
        m_sc[...] = jnp.full(m_sc.shape, NEG_BIG, F32)
        l_sc[...] = jnp.zeros(l_sc.shape, F32)
        acc_sc[...] = jnp.zeros(acc_sc.shape, F32)

    def update(diagonal):
        if diagonal:
            keep = (lax.broadcasted_iota(jnp.int32, (tb, tb), 1) <= lax.broadcasted_iota(jnp.int32, (tb, tb), 0))
        for h in range(C_HEADS):
            hs = slice(h * LANE, (h + 1) * LANE)
            s = _dot_nt(q_ref[:, hs], k_ref[:, hs])
            if diagonal:
                s = jnp.where(keep, s, NEG_BIG)
            m_prev = m_sc[h]
            m_new = jnp.maximum(m_prev, jnp.max(s, axis=-1, keepdims=True))
            alpha = jnp.exp2(m_prev - m_new)
            p = jnp.exp2(s - m_new[:, :1])
            l_sc[h] = alpha * l_sc[h] + jnp.sum(p, axis=-1, keepdims=True)
            acc_sc[h] = alpha * acc_sc[h] + _dot(_bf(p), v_ref[:, hs])
            m_sc[h] = m_new

    @pl.when(j < i)
    def _():
        update(False)

    @pl.when(j == i)
    def _():
        update(True)
        for h in range(0, C_HEADS, 2):
            o0 = acc_sc[h][:, :C_V] / l_sc[h][:, :C_V]
            o1 = acc_sc[h + 1][:, :C_V] / l_sc[h + 1][:, :C_V]
            o_ref[:, h * C_V:(h + 2) * C_V] = _bf(jnp.concatenate([o0, o1], axis=-1))


def _attn_prompt(q, k, v, *, nb, t, tb):
    n = q.shape[0]
    nblk = t // tb
    pairs = [(i, j) for i in range(nblk) for j in range(i + 1)]
    qi = jnp.asarray([p[0] for p in pairs], jnp.int32)
    kj = jnp.asarray([p[1] for p in pairs], jnp.int32)
    q_map = lambda b, s, qi, kj: (b * nblk + qi[s], 0)
    k_map = lambda b, s, qi, kj: (b * nblk + kj[s], 0)
    grid_spec = pltpu.PrefetchScalarGridSpec(
        num_scalar_prefetch=2,
        grid=(nb, len(pairs)),
        in_specs=[
            pl.BlockSpec((tb, C_HEADS * LANE), q_map),
            pl.BlockSpec((tb, C_HEADS * LANE), k_map),
            pl.BlockSpec((tb, C_HEADS * LANE), k_map),
        ],
        out_specs=pl.BlockSpec((tb, C_HEADS * C_V), q_map),
        scratch_shapes=[
            pltpu.VMEM((C_HEADS, tb, LANE), F32),
            pltpu.VMEM((C_HEADS, tb, LANE), F32),
            pltpu.VMEM((C_HEADS, tb, LANE), F32),
        ],
    )
    return pl.pallas_call(
        functools.partial(_attn_prompt_kernel, tb=tb),
        grid_spec=grid_spec,
        out_shape=jax.ShapeDtypeStruct((n, C_HEADS * C_V), BF16),
        compiler_params=_cparams("arbitrary", "arbitrary"),
        name="attn_p",
    )(qi, kj, q, k, v)


def _attn_sample_kernel(pt_ref, q_ref, kn_ref, lat_hbm, kr_hbm, o_ref, lat_buf, kr_buf, sem, *,
                        layer, pages, seq, key_chunk):
    b = pl.program_id(0)
    nb = pl.num_programs(0)
    slot = lax.rem(b, 2)
    rows = C_HEADS * seq

    def page_copies(seq_idx, slot_, with_source):
        copies = []
        for p in range(pages):
            page = pt_ref[seq_idx, p] if with_source else 0
            ks = pl.ds(p * PAGE, PAGE)
            copies.append(pltpu.make_async_copy(lat_hbm.at[layer, page], lat_buf.at[slot_, ks, :], sem.at[0, slot_]))
            copies.append(pltpu.make_async_copy(kr_hbm.at[layer, page], kr_buf.at[slot_, :, ks], sem.at[1, slot_]))
        return copies

    @pl.when(b == 0)
    def _():
        for c in page_copies(0, 0, True):
            c.start()

    for c in page_copies(b, slot, False):
        c.wait()
    nxt = jnp.minimum(b + 1, nb - 1)
    for c in page_copies(nxt, 1 - slot, True):
        c.start()

    q = q_ref[0]
    q_lat = q[:, :C_KVL]
    q_rope = q[:, C_KVL + ROPE_LANE:C_KVL + ROPE_LANE + C_ROPE]
    n_chunks = pages * PAGE // key_chunk
    lats, s = [], []
    for c in range(n_chunks):
        ks = pl.ds(c * key_chunk, key_chunk)
        lats.append(_bf(lat_buf[slot, ks, :]))
        s.append(_dot_nt(q_lat, lats[c]) + _dot(q_rope, _bf(kr_buf[slot, :, ks])))

    qf = q.astype(F32)
    kn = kn_ref[0].astype(F32)
    tpos = lax.broadcasted_iota(jnp.int32, (rows, 1), 0) % seq
    sn = [jnp.where(tpos >= t, jnp.sum(qf * kn[t:t + 1, :], axis=-1, keepdims=True), NEG_BIG) for t in range(seq)]

    m = jnp.max(s[0], axis=-1, keepdims=True)
    for c in range(1, n_chunks):
        m = jnp.maximum(m, jnp.max(s[c], axis=-1, keepdims=True))
    for t in range(seq):
        m = jnp.maximum(m, sn[t])

    l = jnp.zeros((rows, 1), F32)
    acc = jnp.zeros((rows, C_KVL), F32)
    for c in range(n_chunks):
        e = jnp.exp(s[c] - m)
        l = l + jnp.sum(e, axis=-1, keepdims=True)
        acc = acc + _dot(_bf(e), lats[c])
    for t in range(seq):
        e = jnp.exp(sn[t] - m)
        l = l + e
        acc = acc + e * kn[t:t + 1, :C_KVL]
    o_ref[0] = _bf(acc / l)

    @pl.when(b == nb - 1)
    def _():
        for c in page_copies(b, 1 - slot, False):
            c.wait()


def _attn_sample(page_table, q, k_new, cache_latent, cache_krope_t, *, layer, seq):
    nb, pages = page_table.shape
    rows = C_HEADS * seq
    keys = pages * PAGE
    kern = functools.partial(_attn_sample_kernel, layer=layer, pages=pages, seq=seq, key_chunk=min(keys, 2048))
    grid_spec = pltpu.PrefetchScalarGridSpec(
        num_scalar_prefetch=1,
        grid=(nb,),
        in_specs=[
            pl.BlockSpec((1, rows, QK_W), lambda b, pt: (b, 0, 0)),
            pl.BlockSpec((1, seq, QK_W), lambda b, pt: (b, 0, 0)),
            pl.BlockSpec(memory_space=pl.ANY),
            pl.BlockSpec(memory_space=pl.ANY),
        ],
        out_specs=pl.BlockSpec((1, rows, C_KVL), lambda b, pt: (b, 0, 0)),
        scratch_shapes=[
            pltpu.VMEM((2, keys, C_KVL), F32),
            pltpu.VMEM((2, C_ROPE, keys), F32),
            pltpu.SemaphoreType.DMA((2, 2)),
        ],
    )
    return pl.pallas_call(
        kern,
        grid_spec=grid_spec,
        out_shape=jax.ShapeDtypeStruct((nb, rows, C_KVL), BF16),
        compiler_params=_cparams("arbitrary"),
        name="attn_s",
    )(page_table, q, k_new, cache_latent, cache_krope_t)


def _merge_kernel(ya_ref, yb_ref, oc_ref, gt_ref, x_ref, g1_ref, wb_ref, wo_ref, *refs, latent_out):
    if latent_out:
        wv_ref, o_ref = refs
        yc = _bf(_dot(oc_ref[...], wv_ref[...]))
    else:
        (o_ref,) = refs
        yc = oc_ref[...]
    m = gt_ref[:, 0:D_MODEL] * _dot(ya_ref[...], wb_ref[0])
    m = m + gt_ref[:, D_MODEL:2 * D_MODEL] * _dot(yb_ref[...], wb_ref[1])
    m = m + gt_ref[:, 2 * D_MODEL:3 * D_MODEL] * _dot(yc, wb_ref[2])
    o_ref[...] = x_ref[...] + g1_ref[0] * _dot(_bf(m), wo_ref[...])


def _merge(ya, yb, oc, gt, x, g1, wb, wo, wv, *, layer, nb, tm):
    n = x.shape[0]
    nt = n // (nb * tm)
    mod_rows = g1.shape[1]
    tok = lambda b, i: (b * nt + i, 0)
    extra = [] if wv is None else [wv]
    return pl.pallas_call(
        functools.partial(_merge_kernel, latent_out=wv is not None),
        grid=(nb, nt),
        in_specs=[
            pl.BlockSpec((tm, BW), tok),
            pl.BlockSpec((tm, BW), tok),
            pl.BlockSpec((tm, oc.shape[1]), tok),
            pl.BlockSpec((tm, 3 * D_MODEL), tok),
            pl.BlockSpec((tm, D_MODEL), tok),
            pl.BlockSpec((1, mod_rows, D_MODEL), lambda b, i: (b, 0, 0)),
            _resident_layer(wb, layer),
            _resident_layer(wo, layer),
        ] + [_resident_layer(a, layer) for a in extra],
        out_specs=pl.BlockSpec((tm, D_MODEL), tok),
        out_shape=jax.ShapeDtypeStruct((n, D_MODEL), F32),
        compiler_params=_cparams("arbitrary", "arbitrary"),
        name="merge",
    )(ya, yb, oc, gt, x, g1, wb, wo, *extra)


def _conv_gate(up_a, up_b, prev_a, prev_b, cw_a, cw_b, cb_a, cb_b, keep1, keep2):
    def conv(up, prev, cw, cb):
        s1 = jnp.where(keep1, pltpu.roll(up, 1, 0), prev[0])
        s2 = jnp.where(keep2, pltpu.roll(up, 2, 0), prev[1])
        return cb + cw[0:1, :] * s2 + cw[1:2, :] * s1 + cw[2:3, :] * up

    return _bf(_silu(conv(up_a, prev_a, cw_a, cb_a)) * conv(up_b, prev_b, cw_b, cb_b))


def _ffn_up_prompt_kernel(x_ref, sc_ref, sh_ref, g_ref, w_ref, cw_ref, cb_ref, c0_ref, act_ref, cs_ref, stage, *, tm):
    @pl.when(pl.program_id(1) == 0)
    def _():
        stage[0:8, :] = c0_ref[0]

    hb = _bf(_norm_mod(x_ref[...], g_ref[...], sc_ref[0], sh_ref[0]))

    def conv(cols):
        up = _dot(hb, w_ref[:, cols])
        stage[8:8 + tm, cols] = up
        y = (cb_ref[:, cols] + cw_ref[0:1, cols] * stage[6:6 + tm, cols] + cw_ref[1:2, cols] * stage[7:7 + tm, cols]
             + cw_ref[2:3, cols] * up)
        stage[0:8, cols] = up[tm - 8:, :]
        return y

    for c in range(N_FF_CHUNKS):
        ca = slice(c * FF_COL, (c + 1) * FF_COL)
        cb = slice(D_FF + c * FF_COL, D_FF + (c + 1) * FF_COL)
        act_ref[:, ca] = _bf(_silu(conv(ca)) * conv(cb))

    @pl.when(pl.program_id(1) == pl.num_programs(1) - 1)
    def _():
        cs_ref[0] = stage[0:8, :]


def _ffn_up_prompt(x, sc, sh, g, w_up, conv_w, conv_b, conv0, *, layer, nb, tm):
    n = x.shape[0]
    nt = n // (nb * tm)
    tok = lambda b, i: (b * nt + i, 0)
    kern = functools.partial(_ffn_up_prompt_kernel, tm=tm)
    return pl.pallas_call(
        kern,
        grid=(nb, nt),
        in_specs=[
            pl.BlockSpec((tm, D_MODEL), tok),
            pl.BlockSpec((1, 1, D_MODEL), lambda b, i: (b, 0, 0)),
            pl.BlockSpec((1, 1, D_MODEL), lambda b, i: (b, 0, 0)),
            _resident((1, D_MODEL)),
            _resident_layer(w_up, layer),
            _resident((3, 2 * D_FF)),
            _resident((1, 2 * D_FF)),
            pl.BlockSpec((1, 8, 2 * D_FF), lambda b, i: (b, 0, 0)),
        ],
        out_specs=[
            pl.BlockSpec((tm, D_FF), tok),
            pl.BlockSpec((1, 8, 2 * D_FF), lambda b, i: (b, 0, 0)),
        ],
        out_shape=[
            jax.ShapeDtypeStruct((n, D_FF), BF16),
            jax.ShapeDtypeStruct((nb, 8, 2 * D_FF), F32),
        ],
        scratch_shapes=[pltpu.VMEM((8 + tm, 2 * D_FF), F32)],
        compiler_params=_cparams("arbitrary", "arbitrary"),
        name="ffn_up_p",
    )(x, sc, sh, g, w_up, conv_w, conv_b, conv0)


def _ffn_up_sample_kernel(x_ref, sc_ref, sh_ref, g_ref, wa_ref, wb_ref, cwa_ref, cwb_ref, cba_ref, cbb_ref,
                          pa_ref, pb_ref, act_ref, upa_ref, upb_ref, *, n, seq):
    hb = _bf(_norm_mod(x_ref[...], g_ref[...], sc_ref[0], sh_ref[0]))
    tpos = lax.broadcasted_iota(jnp.int32, (n, 1), 0) % seq
    up_a = _dot(hb, wa_ref[...])
    up_b = _dot(hb, wb_ref[...])
    upa_ref[...] = up_a
    upb_ref[...] = up_b
    act_ref[...] = _conv_gate(up_a, up_b, (pa_ref[0], pa_ref[1]), (pb_ref[0], pb_ref[1]), cwa_ref[...], cwb_ref[...],
                              cba_ref[...], cbb_ref[...], tpos >= 1, tpos >= 2)


def _ffn_up_sample(x, sc, sh, g, w_up, conv_w, conv_b, prev, *, layer, seq):
    n = x.shape[0]
    col_a = lambda j: (0, j)
    col_b = lambda j: (0, N_FF_CHUNKS + j)
    kern = functools.partial(_ffn_up_sample_kernel, n=n, seq=seq)
    return pl.pallas_call(
        kern,
        grid=(N_FF_CHUNKS,),
        in_specs=[
            _resident((n, D_MODEL)),
            _resident((1, n, D_MODEL)),
            _resident((1, n, D_MODEL)),
            _resident((1, D_MODEL)),
            pl.BlockSpec((None, D_MODEL, FF_COL), lambda j: (layer, 0, j)),
            pl.BlockSpec((None, D_MODEL, FF_COL), lambda j: (layer, 0, N_FF_CHUNKS + j)),
            pl.BlockSpec((3, FF_COL), col_a),
            pl.BlockSpec((3, FF_COL), col_b),
            pl.BlockSpec((1, FF_COL), col_a),
            pl.BlockSpec((1, FF_COL), col_b),
            pl.BlockSpec((None, 2, n, FF_COL), lambda j: (layer, 0, 0, j)),
            pl.BlockSpec((None, 2, n, FF_COL), lambda j: (layer, 0, 0, N_FF_CHUNKS + j)),
        ],
        out_specs=[
            pl.BlockSpec((n, FF_COL), col_a),
            pl.BlockSpec((n, FF_COL), col_a),
            pl.BlockSpec((n, FF_COL), col_a),
        ],
        out_shape=[
            jax.ShapeDtypeStruct((n, D_FF), BF16),
            jax.ShapeDtypeStruct((n, D_FF), F32),
            jax.ShapeDtypeStruct((n, D_FF), F32),
        ],
        compiler_params=_cparams("arbitrary"),
        name="ffn_up_s",
    )(x, sc, sh, g, w_up, w_up, conv_w, conv_w, conv_b, conv_b, prev, prev)


def _ffn_down_kernel(act_ref, x_ref, g2_ref, w_ref, fg_ref, o_ref, *, final):
    xn = x_ref[...] + g2_ref[0] * _dot(act_ref[...], w_ref[...])
    o_ref[...] = _rms(xn, fg_ref[...]) if final else xn


def _ffn_down(act, x, g2, w_down, final_g, *, layer, nb, tm, final):
    n = x.shape[0]
    nt = n // (nb * tm)
    mod_rows = g2.shape[1]
    tok = lambda b, i: (b * nt + i, 0)
    return pl.pallas_call(
        functools.partial(_ffn_down_kernel, final=final),
        grid=(nb, nt),
        in_specs=[
            pl.BlockSpec((tm, D_FF), tok),
            pl.BlockSpec((tm, D_MODEL), tok),
            pl.BlockSpec((1, mod_rows, D_MODEL), lambda b, i: (b, 0, 0)),
            _resident_layer(w_down, layer),
            _resident((1, D_MODEL)),
        ],
        out_specs=pl.BlockSpec((tm, D_MODEL), tok),
        out_shape=jax.ShapeDtypeStruct((n, D_MODEL), F32),
        compiler_params=_cparams("arbitrary", "arbitrary"),
        name="ffn_down",
    )(act, x, g2, w_down, final_g)


def _rope_tables(pos):
    half = C_ROPE // 2
    inv = ROPE_THETA ** (-jnp.arange(half, dtype=F32) / half)
    ang = pos.astype(F32)[:, None] * inv[None, :]
    cos, sin = jnp.cos(ang), jnp.sin(ang)
    return _rope_block(jnp.concatenate([cos, cos], -1)), _rope_block(jnp.concatenate([-sin, sin], -1))


def _rope_block(a):
    lead = a.shape[:-1]
    return jnp.concatenate([jnp.zeros(lead + (ROPE_LANE,), a.dtype), a,
                            jnp.zeros(lead + (LANE - ROPE_LANE - C_ROPE,), a.dtype)], axis=-1)


def _swap_halves(w):
    half = C_ROPE // 2
    return jnp.concatenate([w[..., half:], w[..., :half]], axis=-1)


def _matmul_weights(w_in, w_uq, w_ukv, w_branch, w_o, w_up, w_down):
    rope_at = O_KRA
    krr = w_in[..., rope_at:rope_at + C_ROPE]
    w_in_b = _bf(jnp.concatenate([w_in[..., :rope_at], _rope_block(krr), _rope_block(_swap_halves(krr)),
                                  w_in[..., rope_at + C_ROPE:]], axis=-1))

    def head_blocks(a):
        pad = jnp.zeros(a.shape[:-1] + (LANE - a.shape[-1],), F32)
        return jnp.concatenate([a, pad], -1).reshape(a.shape[:2] + (C_HEADS * LANE,))

    wq = w_uq.reshape(DEPTH, C_QL, C_HEADS, C_NOPE + C_ROPE)
    q_nope, q_rope = wq[..., :C_NOPE], wq[..., C_NOPE:]
    wqn = head_blocks(q_nope)
    wqa = _rope_block(q_rope).reshape(DEPTH, C_QL, C_HEADS * LANE)
    wqb = _rope_block(_swap_halves(q_rope)).reshape(DEPTH, C_QL, C_HEADS * LANE)

    wkv = w_ukv.reshape(DEPTH, C_KVL, C_HEADS, C_NOPE + C_V)
    wk = jnp.transpose(wkv[..., :C_NOPE], (0, 2, 3, 1))
    wk = jnp.concatenate([wk, jnp.zeros((DEPTH, C_HEADS, LANE - C_NOPE, C_KVL), F32)], axis=2)
    wv = jnp.transpose(wkv[..., C_NOPE:], (0, 2, 1, 3))
    eye = jnp.eye(C_HEADS, dtype=F32)
    wv_bd = (wv[:, :, :, None, :] * eye[None, :, None, :, None]).reshape(DEPTH, C_HEADS * C_KVL, C_HEADS * C_V)
    attn_sample = [_bf(wqn), _bf(wqa), _bf(wqb), _bf(wk)]
    attn_prompt = [_bf(wqn + wqa), _bf(wqb), _bf(head_blocks(wkv[..., :C_NOPE])), _bf(head_blocks(wkv[..., C_NOPE:]))]
    return dict(w_in=w_in_b, attn_s=attn_sample, attn_p=attn_prompt, wv=_bf(wv_bd), wb=_bf(w_branch), wo=_bf(w_o),
                w_up=_bf(w_up), w_down=_bf(w_down))


def _sample_mix_tables(w_s_l, b_s_l, seq):
    t = np.arange(8) % seq
    coef = []
    for j in range(seq):
        src = t - j
        wj = w_s_l[:, t, np.maximum(src, 0)]
        wj = jnp.where(jnp.asarray(src >= 0)[None, :], wj, 0.0)
        coef.append(jnp.repeat(wj.T, LANE, axis=1))
    bias = jnp.repeat(b_s_l[:, t].T, LANE, axis=1)
    return jnp.stack(coef), bias


def _trunk(x, mods, lw, *, sample, nb, t, pos, s0, conv0, cache_latent, cache_krope, page_table,
           w_s, b_s, lnv_g, lnv_b, lb_logits, hgrn_g, q_a_g, kv_a_g, norm1_g, norm2_g, final_g, conv_w, conv_b):
    n = nb * t
    cos, sin = _rope_tables(pos)
    w = lw
    if sample:
        zero = jnp.zeros((DEPTH, nb, t - 1, 2 * D_FF), F32)
        prev1 = jnp.concatenate([conv0[:, :, 1:2], zero], axis=2)
        prev2 = jnp.concatenate([conv0[:, :, 0:2], zero[:, :, 1:]], axis=2)
        prev = jnp.stack([prev1.reshape(DEPTH, n, 2 * D_FF), prev2.reshape(DEPTH, n, 2 * D_FF)], axis=1)
    gv_rows = n if sample else A_CHUNK
    stacks = (jnp.zeros((DEPTH, 1 if sample else nb, gv_rows, BW), F32), jnp.zeros((DEPTH, n, C_KVL), F32),
              jnp.zeros((DEPTH, n, C_ROPE), F32))
    s_stack = jnp.zeros(s0.shape, F32) if sample else None
    s_out, conv_out = [], []
    for l in range(DEPTH):
        sh1, sc1, g1, sh2, sc2, g2 = mods[l]
        row = lambda a: a[l].reshape(1, -1)
        if sample:
            ws, bs = _sample_mix_tables(w_s[l], b_s[l], t)
            kw = dict(nb=1, tm=n, gv_rows=n, sample=True)
        else:
            ws, bs = w_s[l], jnp.repeat(b_s[l].T, LANE, axis=1)
            kw = dict(nb=nb, tm=256, gv_rows=A_CHUNK, sample=False)
        outs = _in_proj(
            x, sc1, sh1, row(norm1_g), w["w_in"], ws, bs, row(lnv_g), row(lnv_b), row(q_a_g), row(kv_a_g),
            cos, sin, w["attn_s"] if sample else w["attn_p"], stacks, layer=l, **kw)
        stacks = (outs[1], outs[-2], outs[-1])
        if sample:
            ya, _, zb, gt, q, k, _, _ = outs
            yb, s_stack = _hgrn_sample(zb, lb_logits, row(hgrn_g), s0, s_stack, layer=l, seq=t, rows=64)
            q_b = q.reshape(nb, t, C_HEADS, QK_W).transpose(0, 2, 1, 3).reshape(nb, C_HEADS * t, QK_W)
            oc = _attn_sample(page_table, q_b, k.reshape(nb, t, QK_W), cache_latent, cache_krope, layer=l, seq=t)
            oc = oc.reshape(nb, C_HEADS, t, C_KVL).transpose(0, 2, 1, 3).reshape(n, C_HEADS * C_KVL)
            mkw = dict(nb=1, tm=n)
            wv = w["wv"]
        else:
            ya, _, zb, gt, q, k, v, _, _ = outs
            yb, s_new = _hgrn_prompt(zb, lb_logits, row(hgrn_g), s0[l], layer=l, nb=nb, tc=512, chunk=128)
            s_out.append(s_new)
            oc = _attn_prompt(q, k, v, nb=nb, t=t, tb=512)
            mkw = dict(nb=nb, tm=512)
            wv = None
        x = _merge(ya, yb, oc, gt, x, g1, w["wb"], w["wo"], wv, layer=l, **mkw)
        if sample:
            act, up_a, up_b = _ffn_up_sample(x, sc2, sh2, row(norm2_g), w["w_up"], conv_w[l], row(conv_b), prev,
                                             layer=l, seq=t)
            conv_out.append(jnp.concatenate([up_a.reshape(nb, t, D_FF)[:, t - 2:], up_b.reshape(nb, t, D_FF)[:, t - 2:]],
                                            axis=-1))
        else:
            c0 = jnp.concatenate([jnp.zeros((nb, 6, 2 * D_FF), F32), conv0[l]], axis=1)
            act, cs = _ffn_up_prompt(x, sc2, sh2, row(norm2_g), w["w_up"], conv_w[l], row(conv_b), c0,
                                     layer=l, nb=nb, tm=512)
            conv_out.append(cs[:, 6:, :])
        x = _ffn_down(act, x, g2, w["w_down"], final_g.reshape(1, -1), layer=l, final=(l == DEPTH - 1), **mkw)
    gv, lat, kr = stacks
    states = s_stack if sample else jnp.stack(s_out)
    return (x.reshape(nb, t, D_MODEL), lat.reshape(DEPTH, nb, t, C_KVL), kr.reshape(DEPTH, nb, t, C_ROPE),
            gv.reshape(DEPTH, nb, -1, BW), states, jnp.stack(conv_out))


def kernel(x_prompt, x_sample, c_prompt, c_sample, cache_latent, cache_krope, state_hgrn, state_conv, page_table,
           norm1_g, norm2_g, final_g, w_ada, b_ada, w_in, w_s, b_s, lnv_g, lnv_b, lb_logits, hgrn_g, q_a_g, kv_a_g,
           w_uq, w_ukv, w_branch, w_o, w_up, conv_w, conv_b, w_down):
    bp, tp, _ = x_prompt.shape
    bs_, ts, _ = x_sample.shape
    past = page_table.shape[1] * cache_latent.shape[2]

    rows = bp + bs_
    rows_pad = -(-rows // 8) * 8
    c_all = jnp.concatenate([c_prompt, c_sample, jnp.zeros((rows_pad - rows, D_MODEL), F32)], axis=0)
    mod = _ada(c_all, w_ada, b_ada)
    mods_p, mods_s = [], []
    for l in range(DEPTH):
        chunks = [mod[l, :, i * D_MODEL:(i + 1) * D_MODEL] for i in range(6)]
        mods_p.append([m[:bp].reshape(bp, 1, D_MODEL) for m in chunks])
        mods_s.append([jnp.repeat(m[bp:rows], ts, axis=0).reshape(1, bs_ * ts, D_MODEL) for m in chunks])

    lw = _matmul_weights(w_in, w_uq, w_ukv, w_branch, w_o, w_up, w_down)
    shared = dict(w_s=w_s, b_s=b_s, lnv_g=lnv_g, lnv_b=lnv_b, lb_logits=lb_logits, hgrn_g=hgrn_g, q_a_g=q_a_g,
                  kv_a_g=kv_a_g, norm1_g=norm1_g, norm2_g=norm2_g, final_g=final_g, conv_w=conv_w, conv_b=conv_b)

    s0_p = jnp.zeros((DEPTH, bp, HG_HEADS, HG_D, HG_D), F32)
    conv0_p = jnp.zeros((DEPTH, bp, 2, 2 * D_FF), F32)
    out_p = _trunk(x_prompt.reshape(bp * tp, D_MODEL), mods_p, lw, sample=False, nb=bp, t=tp, pos=jnp.arange(tp),
                   s0=s0_p, conv0=conv0_p, cache_latent=None, cache_krope=None, page_table=None, **shared)
    pos_s = jnp.tile(past + jnp.arange(ts), bs_)
    out_s = _trunk(x_sample.reshape(bs_ * ts, D_MODEL), mods_s, lw, sample=True, nb=bs_, t=ts, pos=pos_s,
                   s0=state_hgrn, conv0=state_conv, cache_latent=cache_latent,
                   cache_krope=jnp.swapaxes(cache_krope, 2, 3),
                   page_table=page_table, **shared)
    y_p, lat_p, kr_p, gv_p, hgrn_p, conv_p = out_p
    y_s, lat_s, kr_s, gv_s, hgrn_s, conv_s = out_s
    return (y_p, y_s, lat_p, kr_p, gv_p, hgrn_p, conv_p, lat_s, kr_s, gv_s, hgrn_s, conv_s)
```

```python
import functools
import math

import numpy as np
import jax
import jax.numpy as jnp
from jax import lax
from jax.experimental import pallas as pl
from jax.experimental.pallas import tpu as pltpu

F32 = jnp.float32
BF16 = jnp.bfloat16

D_MODEL = 1024
DEPTH = 4
PAGE = 128
EPS = 1e-6
NEG_BIG = -1e30
LB_FLOOR = 1e-30
BW = 512
A_CHUNK = 128
A_GROUPS = 4
HG_HEADS = 4
HG_D = 128
C_HEADS = 8
C_NOPE = 64
C_ROPE = 32
C_V = 64
C_QL = 384
C_KVL = 256
QK_W = 384
D_FF = 2816
ROPE_THETA = 10000.0
ATT_SCALE = (C_NOPE + C_ROPE) ** -0.5
LOG2E = math.log2(math.e)
ROPE_LANE = C_NOPE

O_UA, O_VA, O_B, O_CQ, O_CKV, O_KRA, O_KRB, O_GT, W_IN_COLS = 0, 512, 1024, 3072, 3456, 3712, 3840, 3968, 7040

VMEM_LIMIT_BYTES = 58 * 1024 * 1024
LANE = 128
FF_COL = 256
N_FF_CHUNKS = D_FF // FF_COL


def _cparams(*sem):
    return pltpu.CompilerParams(dimension_semantics=sem, vmem_limit_bytes=VMEM_LIMIT_BYTES)


def _resident(shape):
    nd = len(shape)
    return pl.BlockSpec(shape, lambda *_: (0,) * nd, pipeline_mode=pl.Buffered(1))


def _resident_layer(stacked, layer):
    shape = stacked.shape[1:]
    return pl.BlockSpec((None,) + shape, lambda *_: (layer,) + (0,) * len(shape), pipeline_mode=pl.Buffered(1))


def _bf(x):
    return x.astype(BF16)


def _dot(a, b):
    return jnp.dot(a, b, preferred_element_type=F32)


def _dot_nt(a, b):
    return lax.dot_general(a, b, (((1,), (1,)), ((), ())), preferred_element_type=F32)


def _dot_tn(a, b):
    return lax.dot_general(a, b, (((0,), (0,)), ((), ())), preferred_element_type=F32)


def _sigmoid(x):
    return 1.0 / (1.0 + jnp.exp(-x))


def _silu(x):
    return x * _sigmoid(x)


def _gelu_tanh(x):
    return 0.5 * x * (1.0 + jnp.tanh(math.sqrt(2.0 / math.pi) * (x + 0.044715 * (x * x * x))))


def _softplus(x):
    return jnp.maximum(x, 0.0) + jnp.log1p(jnp.exp(-jnp.abs(x)))


def _logaddexp(a, b):
    return jnp.maximum(a, b) + jnp.log1p(jnp.exp(-jnp.abs(a - b)))


def _rms(x, g):
    return x * lax.rsqrt(jnp.mean(x * x, axis=-1, keepdims=True) + EPS) * g


def _norm_mod(x, g, sc, sh):
    return _rms(x, g) * (1.0 + sc) + sh


def _split_bf16(x):
    hi = _bf(x)
    lo = _bf(x - hi.astype(F32))
    return hi, lo


def _ada_kernel(c_ref, w_ref, b_ref, o_ref):
    c = c_ref[...]
    o_ref[0] = _dot(_bf(_silu(c)), _bf(w_ref[0])) + b_ref[0]


def _ada(c_all, w_ada, b_ada):
    rows = c_all.shape[0]
    tn = 1536
    return pl.pallas_call(
        _ada_kernel,
        grid=(DEPTH, 6 * D_MODEL // tn),
        in_specs=[
            pl.BlockSpec((rows, D_MODEL), lambda l, j: (0, 0)),
            pl.BlockSpec((1, D_MODEL, tn), lambda l, j: (l, 0, j)),
            pl.BlockSpec((1, 1, tn), lambda l, j: (l, 0, j)),
        ],
        out_specs=pl.BlockSpec((1, rows, tn), lambda l, j: (l, 0, j)),
        out_shape=jax.ShapeDtypeStruct((DEPTH, rows, 6 * D_MODEL), F32),
        compiler_params=_cparams("arbitrary", "arbitrary"),
        name="ada",
    )(c_all, w_ada, b_ada.reshape(DEPTH, 1, 6 * D_MODEL))


def _in_proj_kernel(x_ref, sc_ref, sh_ref, g_ref, w_ref, ws_ref, bs_ref, lng_ref, lnb_ref, qag_ref, kvg_ref,
                    cos_ref, sin_ref, *refs, tm, gv_rows, sample, n_unused):
    refs = refs[:4] + refs[4 + n_unused:]
    if sample:
        wqn_ref, wqa_ref, wqb_ref, wk_ref, ya_ref, gv_ref, zb_ref, gt_ref, q_ref, k_ref, lat_ref, kr_ref = refs
    else:
        wq1_ref, wqb_ref, wkn_ref, wvv_ref, ya_ref, gv_ref, zb_ref, gt_ref, q_ref, k_ref, v_ref, lat_ref, kr_ref = refs
    hb = _bf(_norm_mod(x_ref[...], g_ref[...], sc_ref[0], sh_ref[0]))

    u = _gelu_tanh(_dot(hb, w_ref[:, O_UA:O_UA + BW]))
    va = _gelu_tanh(_dot(hb, w_ref[:, O_VA:O_VA + BW]))
    mu = jnp.mean(va, axis=-1, keepdims=True)
    var = jnp.mean(jnp.square(va - mu), axis=-1, keepdims=True)
    v = (va - mu) * lax.rsqrt(var + EPS) * lng_ref[...] + lnb_ref[...]
    if sample:
        v3 = v.reshape(tm // 8, 8, BW)
        mixed = v3 * ws_ref[0][None] + bs_ref[...][None]
        for j in range(1, 4):
            vj = pltpu.roll(v, j, 0).reshape(tm // 8, 8, BW)
            mixed = mixed + vj * ws_ref[j][None]
        ya_ref[...] = _bf(u * mixed.reshape(tm, BW))
    else:
        row = lax.broadcasted_iota(jnp.int32, (A_CHUNK, A_CHUNK), 0)
        col = lax.broadcasted_iota(jnp.int32, (A_CHUNK, A_CHUNK), 1)
        wc = [_bf(jnp.where(col <= row, ws_ref[g], 0.0)) for g in range(A_GROUPS)]
        for c in range(tm // A_CHUNK):
            rs = slice(c * A_CHUNK, (c + 1) * A_CHUNK)
            for g in range(A_GROUPS):
                cs = slice(g * LANE, (g + 1) * LANE)
                mixed = _dot(wc[g], _bf(v[rs, cs])) + bs_ref[:, cs]
                ya_ref[rs, cs] = _bf(u[rs, cs] * mixed)

    @pl.when(pl.program_id(1) == pl.num_programs(1) - 1)
    def _():
        gv_ref[0] = v[tm - gv_rows:, :]

    zb_ref[...] = _dot(hb, w_ref[:, O_B:O_B + 4 * BW])
    gt_ref[...] = _sigmoid(_dot(hb, w_ref[:, O_GT:O_GT + 3 * D_MODEL]))

    cos = cos_ref[...]
    sin = sin_ref[...]
    qn = _bf(_rms(_dot(hb, w_ref[:, O_CQ:O_CQ + C_QL]), qag_ref[...]))
    lat = _rms(_dot(hb, w_ref[:, O_CKV:O_CKV + C_KVL]), kvg_ref[...])
    kr = _dot(hb, w_ref[:, O_KRA:O_KRA + LANE]) * cos + _dot(hb, w_ref[:, O_KRB:O_KRB + LANE]) * sin
    lat_ref[...] = lat
    kr_ref[...] = kr[:, ROPE_LANE:ROPE_LANE + C_ROPE]
    if sample:
        for h in range(C_HEADS):
            hs = slice(h * LANE, (h + 1) * LANE)
            q_nope = _bf(_dot(qn, wqn_ref[:, hs]))
            q_lat = _dot(q_nope, wk_ref[h])
            q_rope = _dot(qn, wqa_ref[:, hs]) * cos + _dot(qn, wqb_ref[:, hs]) * sin
            q_ref[:, h * QK_W:h * QK_W + C_KVL] = _bf(q_lat * ATT_SCALE)
            q_ref[:, h * QK_W + C_KVL:(h + 1) * QK_W] = _bf(q_rope * ATT_SCALE)
        k_ref[:, :C_KVL] = _bf(lat)
        k_ref[:, C_KVL:] = _bf(kr)
    else:
        lane = lax.broadcasted_iota(jnp.int32, (1, LANE), 1)
        cos_q = cos + jnp.where(lane < C_NOPE, 1.0, 0.0)
        latb = _bf(lat)
        for h in range(C_HEADS):
            hs = slice(h * LANE, (h + 1) * LANE)
            q = _dot(qn, wq1_ref[:, hs]) * cos_q + _dot(qn, wqb_ref[:, hs]) * sin
            q_ref[:, hs] = _bf(q * (ATT_SCALE * LOG2E))
            k_ref[:, hs] = _bf(_dot(latb, wkn_ref[:, hs]) + kr)
        v_ref[...] = _bf(_dot_nt(wvv_ref[...], latb))


def _in_proj(x, sc, sh, g, w, ws, bs, lng, lnb, qag, kvg, cos, sin, attn_w, stacks, *, layer, nb, tm, gv_rows, sample):
    n = x.shape[0]
    nt = n // (nb * tm)
    mod_rows = sc.shape[1]
    tok = lambda b, i: (b * nt + i, 0)
    tok_l = lambda b, i: (layer, b * nt + i, 0)
    stacks = list(stacks)
    kern = functools.partial(_in_proj_kernel, tm=tm, gv_rows=gv_rows, sample=sample, n_unused=len(stacks))
    if sample:
        qkv_specs = [pl.BlockSpec((tm, C_HEADS * QK_W), tok), pl.BlockSpec((tm, QK_W), tok)]
        qkv_shapes = [jax.ShapeDtypeStruct((n, C_HEADS * QK_W), BF16), jax.ShapeDtypeStruct((n, QK_W), BF16)]
    else:
        qkv_specs = [pl.BlockSpec((tm, C_HEADS * LANE), tok)] * 2 + [
            pl.BlockSpec((C_HEADS * C_V, tm), lambda b, i: (0, b * nt + i))]
        qkv_shapes = [jax.ShapeDtypeStruct((n, C_HEADS * LANE), BF16)] * 2 + [
            jax.ShapeDtypeStruct((C_HEADS * C_V, n), BF16)]
    n_in = 13 + len(attn_w)
    n_out = 6 + len(qkv_specs)
    return pl.pallas_call(
        kern,
        grid=(nb, nt),
        in_specs=[
            pl.BlockSpec((tm, D_MODEL), tok),
            pl.BlockSpec((1, mod_rows, D_MODEL), lambda b, i: (b, 0, 0)),
            pl.BlockSpec((1, mod_rows, D_MODEL), lambda b, i: (b, 0, 0)),
            _resident((1, D_MODEL)),
            _resident_layer(w, layer),
            _resident(ws.shape),
            _resident(bs.shape),
            _resident((1, BW)),
            _resident((1, BW)),
            _resident((1, C_QL)),
            _resident((1, C_KVL)),
            pl.BlockSpec((tm, LANE), lambda b, i: (i, 0)),
            pl.BlockSpec((tm, LANE), lambda b, i: (i, 0)),
        ] + [_resident_layer(a, layer) for a in attn_w] + [pl.BlockSpec(memory_space=pl.ANY)] * len(stacks),
        out_specs=[
            pl.BlockSpec((tm, BW), tok),
            pl.BlockSpec((None, 1, gv_rows, BW), lambda b, i: (layer, b, 0, 0)),
            pl.BlockSpec((tm, 4 * BW), tok),
            pl.BlockSpec((tm, 3 * D_MODEL), tok),
        ] + qkv_specs + [
            pl.BlockSpec((None, tm, C_KVL), tok_l),
            pl.BlockSpec((None, tm, C_ROPE), tok_l),
        ],
        out_shape=[
            jax.ShapeDtypeStruct((n, BW), BF16),
            jax.ShapeDtypeStruct((DEPTH, nb, gv_rows, BW), F32),
            jax.ShapeDtypeStruct((n, 4 * BW), F32),
            jax.ShapeDtypeStruct((n, 3 * D_MODEL), F32),
        ] + qkv_shapes + [
            jax.ShapeDtypeStruct((DEPTH, n, C_KVL), F32),
            jax.ShapeDtypeStruct((DEPTH, n, C_ROPE), F32),
        ],
        input_output_aliases={n_in: 1, n_in + 1: n_out - 2, n_in + 2: n_out - 1},
        compiler_params=_cparams("arbitrary", "arbitrary"),
        name="in_proj_s" if sample else "in_proj_p",
    )(x, sc, sh, g, w, ws, bs, lng, lnb, qag, kvg, cos, sin, *attn_w, *stacks)


def _hgrn_consts(chunk, seg):
    t = np.arange(chunk)
    u = t[None, :]
    tt = t[:, None]
    same_seg = (t // seg)[:, None] == (t // seg)[None, :]
    blocks = [same_seg & (u <= tt), same_seg & (u > tt)]
    masks = [np.eye(chunk, dtype=bool)]
    b = seg // 2
    while b >= 1:
        pos = t % (2 * b)
        p = t - pos
        m = (p + b - 1)[:, None]
        second = pos >= b
        blocks.append(np.where(second[:, None], (u > m) & (u <= tt), (u > tt) & (u <= m)))
        masks.append((p[:, None] == p[None, :]) & second[:, None] & (~second)[None, :])
        b //= 2
    sums = np.concatenate(blocks, 0).astype(np.float32)
    return jnp.asarray(sums, BF16), jnp.asarray(np.stack(masks).astype(np.float32))


def _hgrn_lb(lbl_ref, layer):
    logits = lbl_ref[...]
    e = jnp.exp(logits - jnp.max(logits, axis=0, keepdims=True))
    p = e / jnp.sum(e, axis=0, keepdims=True)
    lb = jnp.zeros((1, HG_HEADS * HG_D), F32)
    for i in range(1, layer + 1):
        lb = lb + p[i:i + 1, :]
    return lb


def _hgrn_chunk(zb, lb, hg, sums_ref, masks_ref, sel, states, row_masks, chunk):
    nlev = masks_ref.shape[0] - 1
    q = _silu(zb[:, 0:BW])
    zf = zb[:, BW:2 * BW]
    vb = zb[:, 2 * BW:3 * BW]
    gate = _silu(zb[:, 3 * BW:4 * BW])
    log_lb = jnp.log(jnp.maximum(lb, LB_FLOOR))
    logf = _logaddexp(log_lb, jnp.log1p(-lb) - _softplus(-zf))
    k = (1.0 - lb) * _sigmoid(-zf)

    parts = _split_bf16(logf)
    sums = sums_ref[...]
    x = _dot(sums, parts[0]) + _dot(sums, parts[1])
    e = jnp.exp(x)
    e_cum = e[0:chunk]
    e_end = e[chunk:2 * chunk]
    nseq = len(states)

    outs = []
    new_states = [[None] * HG_HEADS for _ in range(nseq)]
    for h in range(HG_HEADS):
        hs = slice(h * HG_D, (h + 1) * HG_D)
        qh, kh, vh = q[:, hs], k[:, hs], _bf(vb[:, hs])
        scores = masks_ref[0] * _dot_nt(_bf(qh), _bf(kh))
        for lev in range(nlev):
            el = e[(2 + lev) * chunk:(3 + lev) * chunk, hs]
            scores = scores + masks_ref[1 + lev] * _dot_nt(_bf(qh * el), _bf(kh * el))
        o = _dot(_bf(scores), vh)
        q_in = _bf(qh * e_cum[:, hs])
        k_out = kh * e_end[:, hs]
        decay = jnp.exp(_dot_tn(parts[0][:, hs], sel) + _dot_tn(parts[1][:, hs], sel))
        for j in range(nseq):
            s_prev = states[j][h]
            if nseq == 1:
                o = o + _dot(q_in, _bf(s_prev))
                kj = _bf(k_out)
            else:
                o = o + row_masks[j] * _dot(q_in, _bf(s_prev))
                kj = _bf(k_out * row_masks[j])
            new_states[j][h] = decay[:, j * HG_D:(j + 1) * HG_D] * s_prev + _dot_tn(kj, vh)
        y = o * lax.rsqrt(jnp.mean(o * o, axis=-1, keepdims=True) + EPS) * hg[:, hs]
        outs.append(_bf(y * gate[:, hs]))
    return outs, new_states


def _hgrn_prompt_kernel(z_ref, lbl_ref, hg_ref, sums_ref, masks_ref, sel_ref, s0_ref, y_ref, sout_ref, state, *,
                        layer, tc, chunk):
    @pl.when(pl.program_id(1) == 0)
    def _():
        state[...] = s0_ref[0]

    lb = _hgrn_lb(lbl_ref, layer)
    hg = hg_ref[...]
    sel = sel_ref[...]
    for c in range(tc // chunk):
        rs = slice(c * chunk, (c + 1) * chunk)
        states = [[state[h] for h in range(HG_HEADS)]]
        outs, new_states = _hgrn_chunk(z_ref[rs, :], lb, hg, sums_ref, masks_ref, sel, states, None, chunk)
        for h in range(HG_HEADS):
            y_ref[rs, h * HG_D:(h + 1) * HG_D] = outs[h]
            state[h] = new_states[0][h]

    @pl.when(pl.program_id(1) == pl.num_programs(1) - 1)
    def _():
        sout_ref[0] = state[...]


def _hgrn_prompt(zb, lb_logits, hgrn_g, s0, *, layer, nb, tc, chunk):
    n = zb.shape[0]
    nt = n // (nb * tc)
    sums, masks = _hgrn_consts(chunk, chunk)
    sel = jnp.ones((chunk, HG_D), BF16)
    tok = lambda b, i: (b * nt + i, 0)
    kern = functools.partial(_hgrn_prompt_kernel, layer=layer, tc=tc, chunk=chunk)
    return pl.pallas_call(
        kern,
        grid=(nb, nt),
        in_specs=[
            pl.BlockSpec((tc, 4 * BW), tok),
            _resident(lb_logits.shape),
            _resident((1, BW)),
            _resident(sums.shape),
            _resident(masks.shape),
            _resident(sel.shape),
            pl.BlockSpec((1, HG_HEADS, HG_D, HG_D), lambda b, i: (b, 0, 0, 0)),
        ],
        out_specs=[
            pl.BlockSpec((tc, BW), tok),
            pl.BlockSpec((1, HG_HEADS, HG_D, HG_D), lambda b, i: (b, 0, 0, 0)),
        ],
        out_shape=[
            jax.ShapeDtypeStruct((n, BW), BF16),
            jax.ShapeDtypeStruct((nb, HG_HEADS, HG_D, HG_D), F32),
        ],
        scratch_shapes=[pltpu.VMEM((HG_HEADS, HG_D, HG_D), F32)],
        compiler_params=_cparams("arbitrary", "arbitrary"),
        name="hgrn_p",
    )(zb, lb_logits, hgrn_g, sums, masks, sel, s0)


def _hgrn_sample_kernel(z_ref, lbl_ref, hg_ref, sums_ref, masks_ref, sel_ref, s0_ref, *refs, layer, rows, seq):
    y_ref, sout_ref = refs[-2:]
    nseq = rows // seq
    lb = _hgrn_lb(lbl_ref, layer)
    ridx = lax.broadcasted_iota(jnp.int32, (rows, 1), 0)
    row_masks = [jnp.where((ridx >= j * seq) & (ridx < (j + 1) * seq), 1.0, 0.0) for j in range(nseq)]
    states = [[s0_ref[j, h] for h in range(HG_HEADS)] for j in range(nseq)]
    outs, new_states = _hgrn_chunk(z_ref[...], lb, hg_ref[...], sums_ref, masks_ref, sel_ref[...], states,
                                   row_masks, rows)
    for h in range(HG_HEADS):
        y_ref[:, h * HG_D:(h + 1) * HG_D] = outs[h]
        for j in range(nseq):
            sout_ref[j, h] = new_states[j][h]


def _hgrn_sample(zb, lb_logits, hgrn_g, s0_all, s_stack, *, layer, seq, rows):
    n = zb.shape[0]
    nseq = rows // seq
    sums, masks = _hgrn_consts(rows, seq)
    sel = np.zeros((rows, nseq * HG_D), np.float32)
    for j in range(nseq):
        sel[j * seq:(j + 1) * seq, j * HG_D:(j + 1) * HG_D] = 1.0
    sel = jnp.asarray(sel, BF16)
    kern = functools.partial(_hgrn_sample_kernel, layer=layer, rows=rows, seq=seq)
    state_spec = pl.BlockSpec((None, nseq, HG_HEADS, HG_D, HG_D), lambda i: (layer, i, 0, 0, 0))
    return pl.pallas_call(
        kern,
        grid=(n // rows,),
        in_specs=[
            pl.BlockSpec((rows, 4 * BW), lambda i: (i, 0)),
            _resident(lb_logits.shape),
            _resident((1, BW)),
            _resident(sums.shape),
            _resident(masks.shape),
            _resident(sel.shape),
            state_spec,
            pl.BlockSpec(memory_space=pl.ANY),
        ],
        out_specs=[pl.BlockSpec((rows, BW), lambda i: (i, 0)), state_spec],
        out_shape=[
            jax.ShapeDtypeStruct((n, BW), BF16),
            jax.ShapeDtypeStruct(s0_all.shape, F32),
        ],
        input_output_aliases={7: 1},
        compiler_params=_cparams("arbitrary"),
        name="hgrn_s",
    )(zb, lb_logits, hgrn_g, sums, masks, sel, s0_all, s_stack)


def _attn_prompt_kernel(qi_ref, kj_ref, q_ref, k_ref, v_ref, o_ref, m_sc, l_sc, acc_sc, *, tb):
    step = pl.program_id(1)
    i = qi_ref[step]
    j = kj_ref[step]

    @pl.when(j == 0)
    def _():
        m_sc[...] = jnp.full(m_sc.shape, NEG_BIG, F32)
        l_sc[...] = jnp.zeros(l_sc.shape, F32)
        acc_sc[...] = jnp.zeros(acc_sc.shape, F32)

    def update(diagonal):
        if diagonal:
            keep = (lax.broadcasted_iota(jnp.int32, (tb, tb), 0) <= lax.broadcasted_iota(jnp.int32, (tb, tb), 1))
        for h in range(C_HEADS):
            hs = slice(h * LANE, (h + 1) * LANE)
            st = _dot_nt(k_ref[:, hs], q_ref[:, hs])
            if diagonal:
                st = jnp.where(keep, st, NEG_BIG)
            m_prev = m_sc[h]
            m_new = jnp.maximum(m_prev, jnp.max(st, axis=0, keepdims=True))
            alpha = jnp.exp2(m_prev - m_new)
            p = jnp.exp2(st - m_new)
            l_sc[h] = alpha * l_sc[h] + jnp.sum(p, axis=0, keepdims=True)
            acc_sc[h] = alpha * acc_sc[h] + _dot(v_ref[h * C_V:(h + 1) * C_V, :], _bf(p))
            m_sc[h] = m_new

    @pl.when(j < i)
    def _():
        update(False)

    @pl.when(j == i)
    def _():
        update(True)
        for h in range(C_HEADS):
            o_ref[h * C_V:(h + 1) * C_V, :] = _bf(acc_sc[h] / l_sc[h])


def _attn_prompt(q, k, vt, *, nb, t, tb):
    n = q.shape[0]
    nblk = t // tb
    pairs = [(i, j) for i in range(nblk) for j in range(i + 1)]
    qi = jnp.asarray([p[0] for p in pairs], jnp.int32)
    kj = jnp.asarray([p[1] for p in pairs], jnp.int32)
    q_map = lambda b, s, qi, kj: (b * nblk + qi[s], 0)
    k_map = lambda b, s, qi, kj: (b * nblk + kj[s], 0)
    grid_spec = pltpu.PrefetchScalarGridSpec(
        num_scalar_prefetch=2,
        grid=(nb, len(pairs)),
        in_specs=[
            pl.BlockSpec((tb, C_HEADS * LANE), q_map),
            pl.BlockSpec((tb, C_HEADS * LANE), k_map),
            pl.BlockSpec((C_HEADS * C_V, tb), lambda b, s, qi, kj: (0, b * nblk + kj[s])),
        ],
        out_specs=pl.BlockSpec((C_HEADS * C_V, tb), lambda b, s, qi, kj: (0, b * nblk + qi[s])),
        scratch_shapes=[
            pltpu.VMEM((C_HEADS, 1, tb), F32),
            pltpu.VMEM((C_HEADS, 1, tb), F32),
            pltpu.VMEM((C_HEADS, C_V, tb), F32),
        ],
    )
    return pl.pallas_call(
        functools.partial(_attn_prompt_kernel, tb=tb),
        grid_spec=grid_spec,
        out_shape=jax.ShapeDtypeStruct((C_HEADS * C_V, n), BF16),
        compiler_params=_cparams("arbitrary", "arbitrary"),
        name="attn_p",
    )(qi, kj, q, k, vt)


def _attn_sample_kernel(pt_ref, q_ref, kn_ref, lat_hbm, kr_hbm, o_ref, lat_buf, kr_buf, sem, *,
                        layer, pages, seq, key_chunk):
    b = pl.program_id(0)
    nb = pl.num_programs(0)
    slot = lax.rem(b, 2)
    rows = C_HEADS * seq

    def page_copies(seq_idx, slot_, with_source):
        copies = []
        for p in range(pages):
            page = pt_ref[seq_idx, p] if with_source else 0
            ks = pl.ds(p * PAGE, PAGE)
            copies.append(pltpu.make_async_copy(lat_hbm.at[layer, page], lat_buf.at[slot_, ks, :], sem.at[0, slot_]))
            copies.append(pltpu.make_async_copy(kr_hbm.at[layer, page], kr_buf.at[slot_, :, ks], sem.at[1, slot_]))
        return copies

    @pl.when(b == 0)
    def _():
        for c in page_copies(0, 0, True):
            c.start()

    for c in page_copies(b, slot, False):
        c.wait()
    nxt = jnp.minimum(b + 1, nb - 1)
    for c in page_copies(nxt, 1 - slot, True):
        c.start()

    q = q_ref[0]
    q_lat = q[:, :C_KVL]
    q_rope = q[:, C_KVL + ROPE_LANE:C_KVL + ROPE_LANE + C_ROPE]
    n_chunks = pages * PAGE // key_chunk
    lats, s = [], []
    for c in range(n_chunks):
        ks = pl.ds(c * key_chunk, key_chunk)
        lats.append(_bf(lat_buf[slot, ks, :]))
        s.append(_dot_nt(q_lat, lats[c]) + _dot(q_rope, _bf(kr_buf[slot, :, ks])))

    qf = q.astype(F32)
    kn = kn_ref[0].astype(F32)
    tpos = lax.broadcasted_iota(jnp.int32, (rows, 1), 0) % seq
    sn = [jnp.where(tpos >= t, jnp.sum(qf * kn[t:t + 1, :], axis=-1, keepdims=True), NEG_BIG) for t in range(seq)]

    m = jnp.max(s[0], axis=-1, keepdims=True)
    for c in range(1, n_chunks):
        m = jnp.maximum(m, jnp.max(s[c], axis=-1, keepdims=True))
    for t in range(seq):
        m = jnp.maximum(m, sn[t])

    l = jnp.zeros((rows, 1), F32)
    acc = jnp.zeros((rows, C_KVL), F32)
    for c in range(n_chunks):
        e = jnp.exp(s[c] - m)
        l = l + jnp.sum(e, axis=-1, keepdims=True)
        acc = acc + _dot(_bf(e), lats[c])
    for t in range(seq):
        e = jnp.exp(sn[t] - m)
        l = l + e
        acc = acc + e * kn[t:t + 1, :C_KVL]
    o_ref[0] = _bf(acc / l)

    @pl.when(b == nb - 1)
    def _():
        for c in page_copies(b, 1 - slot, False):
            c.wait()


def _attn_sample(page_table, q, k_new, cache_latent, cache_krope_t, *, layer, seq):
    nb, pages = page_table.shape
    rows = C_HEADS * seq
    keys = pages * PAGE
    kern = functools.partial(_attn_sample_kernel, layer=layer, pages=pages, seq=seq, key_chunk=min(keys, 2048))
    grid_spec = pltpu.PrefetchScalarGridSpec(
        num_scalar_prefetch=1,
        grid=(nb,),
        in_specs=[
            pl.BlockSpec((1, rows, QK_W), lambda b, pt: (b, 0, 0)),
            pl.BlockSpec((1, seq, QK_W), lambda b, pt: (b, 0, 0)),
            pl.BlockSpec(memory_space=pl.ANY),
            pl.BlockSpec(memory_space=pl.ANY),
        ],
        out_specs=pl.BlockSpec((1, rows, C_KVL), lambda b, pt: (b, 0, 0)),
        scratch_shapes=[
            pltpu.VMEM((2, keys, C_KVL), F32),
            pltpu.VMEM((2, C_ROPE, keys), F32),
            pltpu.SemaphoreType.DMA((2, 2)),
        ],
    )
    return pl.pallas_call(
        kern,
        grid_spec=grid_spec,
        out_shape=jax.ShapeDtypeStruct((nb, rows, C_KVL), BF16),
        compiler_params=_cparams("arbitrary"),
        name="attn_s",
    )(page_table, q, k_new, cache_latent, cache_krope_t)


def _merge_kernel(ya_ref, yb_ref, oc_ref, gt_ref, x_ref, g1_ref, wb_ref, wo_ref, *refs, latent_out):
    if latent_out:
        wv_ref, o_ref = refs
        yc = _bf(_dot(oc_ref[...], wv_ref[...]))
        br_c = _dot(yc, wb_ref[2])
    else:
        (o_ref,) = refs
        br_c = _dot_tn(oc_ref[...], wb_ref[2])
    m = gt_ref[:, 0:D_MODEL] * _dot(ya_ref[...], wb_ref[0])
    m = m + gt_ref[:, D_MODEL:2 * D_MODEL] * _dot(yb_ref[...], wb_ref[1])
    m = m + gt_ref[:, 2 * D_MODEL:3 * D_MODEL] * br_c
    o_ref[...] = x_ref[...] + g1_ref[0] * _dot(_bf(m), wo_ref[...])


def _merge(ya, yb, oc, gt, x, g1, wb, wo, wv, *, layer, nb, tm):
    n = x.shape[0]
    nt = n // (nb * tm)
    mod_rows = g1.shape[1]
    tok = lambda b, i: (b * nt + i, 0)
    extra = [] if wv is None else [wv]
    return pl.pallas_call(
        functools.partial(_merge_kernel, latent_out=wv is not None),
        grid=(nb, nt),
        in_specs=[
            pl.BlockSpec((tm, BW), tok),
            pl.BlockSpec((tm, BW), tok),
            (pl.BlockSpec((BW, tm), lambda b, i: (0, b * nt + i)) if wv is None
             else pl.BlockSpec((tm, oc.shape[1]), tok)),
            pl.BlockSpec((tm, 3 * D_MODEL), tok),
            pl.BlockSpec((tm, D_MODEL), tok),
            pl.BlockSpec((1, mod_rows, D_MODEL), lambda b, i: (b, 0, 0)),
            _resident_layer(wb, layer),
            _resident_layer(wo, layer),
        ] + [_resident_layer(a, layer) for a in extra],
        out_specs=pl.BlockSpec((tm, D_MODEL), tok),
        out_shape=jax.ShapeDtypeStruct((n, D_MODEL), F32),
        compiler_params=_cparams("arbitrary", "arbitrary"),
        name="merge",
    )(ya, yb, oc, gt, x, g1, wb, wo, *extra)


def _conv_gate(up_a, up_b, prev_a, prev_b, cw_a, cw_b, cb_a, cb_b, keep1, keep2):
    def conv(up, prev, cw, cb):
        s1 = jnp.where(keep1, pltpu.roll(up, 1, 0), prev[0])
        s2 = jnp.where(keep2, pltpu.roll(up, 2, 0), prev[1])
        return cb + cw[0:1, :] * s2 + cw[1:2, :] * s1 + cw[2:3, :] * up

    return _bf(_silu(conv(up_a, prev_a, cw_a, cb_a)) * conv(up_b, prev_b, cw_b, cb_b))


def _ffn_up_prompt_kernel(x_ref, sc_ref, sh_ref, g_ref, w_ref, cw_ref, cb_ref, c0_ref, act_ref, cs_ref, stage, *, tm):
    @pl.when(pl.program_id(1) == 0)
    def _():
        stage[0:8, :] = c0_ref[0]

    hb = _bf(_norm_mod(x_ref[...], g_ref[...], sc_ref[0], sh_ref[0]))

    def conv(cols):
        up = _dot(hb, w_ref[:, cols])
        stage[8:8 + tm, cols] = up
        y = (cb_ref[:, cols] + cw_ref[0:1, cols] * stage[6:6 + tm, cols] + cw_ref[1:2, cols] * stage[7:7 + tm, cols]
             + cw_ref[2:3, cols] * up)
        stage[0:8, cols] = up[tm - 8:, :]
        return y

    for c in range(N_FF_CHUNKS):
        ca = slice(c * FF_COL, (c + 1) * FF_COL)
        cb = slice(D_FF + c * FF_COL, D_FF + (c + 1) * FF_COL)
        act_ref[:, ca] = _bf(_silu(conv(ca)) * conv(cb))

    @pl.when(pl.program_id(1) == pl.num_programs(1) - 1)
    def _():
        cs_ref[0] = stage[0:8, :]


def _ffn_up_prompt(x, sc, sh, g, w_up, conv_w, conv_b, conv0, *, layer, nb, tm):
    n = x.shape[0]
    nt = n // (nb * tm)
    tok = lambda b, i: (b * nt + i, 0)
    kern = functools.partial(_ffn_up_prompt_kernel, tm=tm)
    return pl.pallas_call(
        kern,
        grid=(nb, nt),
        in_specs=[
            pl.BlockSpec((tm, D_MODEL), tok),
            pl.BlockSpec((1, 1, D_MODEL), lambda b, i: (b, 0, 0)),
            pl.BlockSpec((1, 1, D_MODEL), lambda b, i: (b, 0, 0)),
            _resident((1, D_MODEL)),
            _resident_layer(w_up, layer),
            _resident((3, 2 * D_FF)),
            _resident((1, 2 * D_FF)),
            pl.BlockSpec((1, 8, 2 * D_FF), lambda b, i: (b, 0, 0)),
        ],
        out_specs=[
            pl.BlockSpec((tm, D_FF), tok),
            pl.BlockSpec((1, 8, 2 * D_FF), lambda b, i: (b, 0, 0)),
        ],
        out_shape=[
            jax.ShapeDtypeStruct((n, D_FF), BF16),
            jax.ShapeDtypeStruct((nb, 8, 2 * D_FF), F32),
        ],
        scratch_shapes=[pltpu.VMEM((8 + tm, 2 * D_FF), F32)],
        compiler_params=_cparams("arbitrary", "arbitrary"),
        name="ffn_up_p",
    )(x, sc, sh, g, w_up, conv_w, conv_b, conv0)


def _ffn_up_sample_kernel(x_ref, sc_ref, sh_ref, g_ref, wa_ref, wb_ref, cwa_ref, cwb_ref, cba_ref, cbb_ref,
                          pa_ref, pb_ref, act_ref, upa_ref, upb_ref, *, n, seq):
    hb = _bf(_norm_mod(x_ref[...], g_ref[...], sc_ref[0], sh_ref[0]))
    tpos = lax.broadcasted_iota(jnp.int32, (n, 1), 0) % seq
    up_a = _dot(hb, wa_ref[...])
    up_b = _dot(hb, wb_ref[...])
    upa_ref[...] = up_a
    upb_ref[...] = up_b
    act_ref[...] = _conv_gate(up_a, up_b, (pa_ref[0], pa_ref[1]), (pb_ref[0], pb_ref[1]), cwa_ref[...], cwb_ref[...],
                              cba_ref[...], cbb_ref[...], tpos >= 1, tpos >= 2)


def _ffn_up_sample(x, sc, sh, g, w_up, conv_w, conv_b, prev, *, layer, seq):
    n = x.shape[0]
    col_a = lambda j: (0, j)
    col_b = lambda j: (0, N_FF_CHUNKS + j)
    kern = functools.partial(_ffn_up_sample_kernel, n=n, seq=seq)
    return pl.pallas_call(
        kern,
        grid=(N_FF_CHUNKS,),
        in_specs=[
            _resident((n, D_MODEL)),
            _resident((1, n, D_MODEL)),
            _resident((1, n, D_MODEL)),
            _resident((1, D_MODEL)),
            pl.BlockSpec((None, D_MODEL, FF_COL), lambda j: (layer, 0, j)),
            pl.BlockSpec((None, D_MODEL, FF_COL), lambda j: (layer, 0, N_FF_CHUNKS + j)),
            pl.BlockSpec((3, FF_COL), col_a),
            pl.BlockSpec((3, FF_COL), col_b),
            pl.BlockSpec((1, FF_COL), col_a),
            pl.BlockSpec((1, FF_COL), col_b),
            pl.BlockSpec((None, 2, n, FF_COL), lambda j: (layer, 0, 0, j)),
            pl.BlockSpec((None, 2, n, FF_COL), lambda j: (layer, 0, 0, N_FF_CHUNKS + j)),
        ],
        out_specs=[
            pl.BlockSpec((n, FF_COL), col_a),
            pl.BlockSpec((n, FF_COL), col_a),
            pl.BlockSpec((n, FF_COL), col_a),
        ],
        out_shape=[
            jax.ShapeDtypeStruct((n, D_FF), BF16),
            jax.ShapeDtypeStruct((n, D_FF), F32),
            jax.ShapeDtypeStruct((n, D_FF), F32),
        ],
        compiler_params=_cparams("arbitrary"),
        name="ffn_up_s",
    )(x, sc, sh, g, w_up, w_up, conv_w, conv_w, conv_b, conv_b, prev, prev)


def _ffn_down_kernel(act_ref, x_ref, g2_ref, w_ref, fg_ref, o_ref, *, final):
    xn = x_ref[...] + g2_ref[0] * _dot(act_ref[...], w_ref[...])
    o_ref[...] = _rms(xn, fg_ref[...]) if final else xn


def _ffn_down(act, x, g2, w_down, final_g, *, layer, nb, tm, final):
    n = x.shape[0]
    nt = n // (nb * tm)
    mod_rows = g2.shape[1]
    tok = lambda b, i: (b * nt + i, 0)
    return pl.pallas_call(
        functools.partial(_ffn_down_kernel, final=final),
        grid=(nb, nt),
        in_specs=[
            pl.BlockSpec((tm, D_FF), tok),
            pl.BlockSpec((tm, D_MODEL), tok),
            pl.BlockSpec((1, mod_rows, D_MODEL), lambda b, i: (b, 0, 0)),
            _resident_layer(w_down, layer),
            _resident((1, D_MODEL)),
        ],
        out_specs=pl.BlockSpec((tm, D_MODEL), tok),
        out_shape=jax.ShapeDtypeStruct((n, D_MODEL), F32),
        compiler_params=_cparams("arbitrary", "arbitrary"),
        name="ffn_down",
    )(act, x, g2, w_down, final_g)


def _rope_tables(pos):
    half = C_ROPE // 2
    inv = ROPE_THETA ** (-jnp.arange(half, dtype=F32) / half)
    ang = pos.astype(F32)[:, None] * inv[None, :]
    cos, sin = jnp.cos(ang), jnp.sin(ang)
    return _rope_block(jnp.concatenate([cos, cos], -1)), _rope_block(jnp.concatenate([-sin, sin], -1))


def _rope_block(a):
    lead = a.shape[:-1]
    return jnp.concatenate([jnp.zeros(lead + (ROPE_LANE,), a.dtype), a,
                            jnp.zeros(lead + (LANE - ROPE_LANE - C_ROPE,), a.dtype)], axis=-1)


def _swap_halves(w):
    half = C_ROPE // 2
    return jnp.concatenate([w[..., half:], w[..., :half]], axis=-1)


def _matmul_weights(w_in, w_uq, w_ukv, w_branch, w_o, w_up, w_down):
    rope_at = O_KRA
    krr = w_in[..., rope_at:rope_at + C_ROPE]
    w_in_b = _bf(jnp.concatenate([w_in[..., :rope_at], _rope_block(krr), _rope_block(_swap_halves(krr)),
                                  w_in[..., rope_at + C_ROPE:]], axis=-1))

    def head_blocks(a):
        pad = jnp.zeros(a.shape[:-1] + (LANE - a.shape[-1],), F32)
        return jnp.concatenate([a, pad], -1).reshape(a.shape[:2] + (C_HEADS * LANE,))

    wq = w_uq.reshape(DEPTH, C_QL, C_HEADS, C_NOPE + C_ROPE)
    q_nope, q_rope = wq[..., :C_NOPE], wq[..., C_NOPE:]
    wqn = head_blocks(q_nope)
    wqa = _rope_block(q_rope).reshape(DEPTH, C_QL, C_HEADS * LANE)
    wqb = _rope_block(_swap_halves(q_rope)).reshape(DEPTH, C_QL, C_HEADS * LANE)

    wkv = w_ukv.reshape(DEPTH, C_KVL, C_HEADS, C_NOPE + C_V)
    wk = jnp.transpose(wkv[..., :C_NOPE], (0, 2, 3, 1))
    wk = jnp.concatenate([wk, jnp.zeros((DEPTH, C_HEADS, LANE - C_NOPE, C_KVL), F32)], axis=2)
    wv = jnp.transpose(wkv[..., C_NOPE:], (0, 2, 1, 3))
    eye = jnp.eye(C_HEADS, dtype=F32)
    wv_bd = (wv[:, :, :, None, :] * eye[None, :, None, :, None]).reshape(DEPTH, C_HEADS * C_KVL, C_HEADS * C_V)
    attn_sample = [_bf(wqn), _bf(wqa), _bf(wqb), _bf(wk)]
    wv_t = jnp.transpose(wkv[..., C_NOPE:], (0, 2, 3, 1)).reshape(DEPTH, C_HEADS * C_V, C_KVL)
    attn_prompt = [_bf(wqn + wqa), _bf(wqb), _bf(head_blocks(wkv[..., :C_NOPE])), _bf(wv_t)]
    return dict(w_in=w_in_b, attn_s=attn_sample, attn_p=attn_prompt, wv=_bf(wv_bd), wb=_bf(w_branch), wo=_bf(w_o),
                w_up=_bf(w_up), w_down=_bf(w_down))


def _sample_mix_tables(w_s_l, b_s_l, seq):
    t = np.arange(8) % seq
    coef = []
    for j in range(seq):
        src = t - j
        wj = w_s_l[:, t, np.maximum(src, 0)]
        wj = jnp.where(jnp.asarray(src >= 0)[None, :], wj, 0.0)
        coef.append(jnp.repeat(wj.T, LANE, axis=1))
    bias = jnp.repeat(b_s_l[:, t].T, LANE, axis=1)
    return jnp.stack(coef), bias


def _trunk(x, mods, lw, *, sample, nb, t, pos, s0, conv0, cache_latent, cache_krope, page_table,
           w_s, b_s, lnv_g, lnv_b, lb_logits, hgrn_g, q_a_g, kv_a_g, norm1_g, norm2_g, final_g, conv_w, conv_b):
    n = nb * t
    cos, sin = _rope_tables(pos)
    w = lw
    if sample:
        zero = jnp.zeros((DEPTH, nb, t - 1, 2 * D_FF), F32)
        prev1 = jnp.concatenate([conv0[:, :, 1:2], zero], axis=2)
        prev2 = jnp.concatenate([conv0[:, :, 0:2], zero[:, :, 1:]], axis=2)
        prev = jnp.stack([prev1.reshape(DEPTH, n, 2 * D_FF), prev2.reshape(DEPTH, n, 2 * D_FF)], axis=1)
    gv_rows = n if sample else A_CHUNK
    stacks = (jnp.zeros((DEPTH, 1 if sample else nb, gv_rows, BW), F32), jnp.zeros((DEPTH, n, C_KVL), F32),
              jnp.zeros((DEPTH, n, C_ROPE), F32))
    s_stack = jnp.zeros(s0.shape, F32) if sample else None
    s_out, conv_out = [], []
    for l in range(DEPTH):
        sh1, sc1, g1, sh2, sc2, g2 = mods[l]
        row = lambda a: a[l].reshape(1, -1)
        if sample:
            ws, bs = _sample_mix_tables(w_s[l], b_s[l], t)
            kw = dict(nb=1, tm=n, gv_rows=n, sample=True)
        else:
            ws, bs = w_s[l], jnp.repeat(b_s[l].T, LANE, axis=1)
            kw = dict(nb=nb, tm=256, gv_rows=A_CHUNK, sample=False)
        outs = _in_proj(
            x, sc1, sh1, row(norm1_g), w["w_in"], ws, bs, row(lnv_g), row(lnv_b), row(q_a_g), row(kv_a_g),
            cos, sin, w["attn_s"] if sample else w["attn_p"], stacks, layer=l, **kw)
        stacks = (outs[1], outs[-2], outs[-1])
        if sample:
            ya, _, zb, gt, q, k, _, _ = outs
            yb, s_stack = _hgrn_sample(zb, lb_logits, row(hgrn_g), s0, s_stack, layer=l, seq=t, rows=64)
            q_b = q.reshape(nb, t, C_HEADS, QK_W).transpose(0, 2, 1, 3).reshape(nb, C_HEADS * t, QK_W)
            oc = _attn_sample(page_table, q_b, k.reshape(nb, t, QK_W), cache_latent, cache_krope, layer=l, seq=t)
            oc = oc.reshape(nb, C_HEADS, t, C_KVL).transpose(0, 2, 1, 3).reshape(n, C_HEADS * C_KVL)
            mkw = dict(nb=1, tm=n)
            wv = w["wv"]
        else:
            ya, _, zb, gt, q, k, v, _, _ = outs
            yb, s_new = _hgrn_prompt(zb, lb_logits, row(hgrn_g), s0[l], layer=l, nb=nb, tc=512, chunk=128)
            s_out.append(s_new)
            oc = _attn_prompt(q, k, v, nb=nb, t=t, tb=512)
            mkw = dict(nb=nb, tm=512)
            wv = None
        x = _merge(ya, yb, oc, gt, x, g1, w["wb"], w["wo"], wv, layer=l, **mkw)
        if sample:
            act, up_a, up_b = _ffn_up_sample(x, sc2, sh2, row(norm2_g), w["w_up"], conv_w[l], row(conv_b), prev,
                                             layer=l, seq=t)
            conv_out.append(jnp.concatenate([up_a.reshape(nb, t, D_FF)[:, t - 2:], up_b.reshape(nb, t, D_FF)[:, t - 2:]],
                                            axis=-1))
        else:
            c0 = jnp.concatenate([jnp.zeros((nb, 6, 2 * D_FF), F32), conv0[l]], axis=1)
            act, cs = _ffn_up_prompt(x, sc2, sh2, row(norm2_g), w["w_up"], conv_w[l], row(conv_b), c0,
                                     layer=l, nb=nb, tm=512)
            conv_out.append(cs[:, 6:, :])
        x = _ffn_down(act, x, g2, w["w_down"], final_g.reshape(1, -1), layer=l, final=(l == DEPTH - 1), **mkw)
    gv, lat, kr = stacks
    states = s_stack if sample else jnp.stack(s_out)
    return (x.reshape(nb, t, D_MODEL), lat.reshape(DEPTH, nb, t, C_KVL), kr.reshape(DEPTH, nb, t, C_ROPE),
            gv.reshape(DEPTH, nb, -1, BW), states, jnp.stack(conv_out))


def kernel(x_prompt, x_sample, c_prompt, c_sample, cache_latent, cache_krope, state_hgrn, state_conv, page_table,
           norm1_g, norm2_g, final_g, w_ada, b_ada, w_in, w_s, b_s, lnv_g, lnv_b, lb_logits, hgrn_g, q_a_g, kv_a_g,
           w_uq, w_ukv, w_branch, w_o, w_up, conv_w, conv_b, w_down):
    bp, tp, _ = x_prompt.shape
    bs_, ts, _ = x_sample.shape
    past = page_table.shape[1] * cache_latent.shape[2]

    rows = bp + bs_
    rows_pad = -(-rows // 8) * 8
    c_all = jnp.concatenate([c_prompt, c_sample, jnp.zeros((rows_pad - rows, D_MODEL), F32)], axis=0)
    mod = _ada(c_all, w_ada, b_ada)
    mods_p, mods_s = [], []
    for l in range(DEPTH):
        chunks = [mod[l, :, i * D_MODEL:(i + 1) * D_MODEL] for i in range(6)]
        mods_p.append([m[:bp].reshape(bp, 1, D_MODEL) for m in chunks])
        mods_s.append([jnp.repeat(m[bp:rows], ts, axis=0).reshape(1, bs_ * ts, D_MODEL) for m in chunks])

    lw = _matmul_weights(w_in, w_uq, w_ukv, w_branch, w_o, w_up, w_down)
    shared = dict(w_s=w_s, b_s=b_s, lnv_g=lnv_g, lnv_b=lnv_b, lb_logits=lb_logits, hgrn_g=hgrn_g, q_a_g=q_a_g,
                  kv_a_g=kv_a_g, norm1_g=norm1_g, norm2_g=norm2_g, final_g=final_g, conv_w=conv_w, conv_b=conv_b)

    s0_p = jnp.zeros((DEPTH, bp, HG_HEADS, HG_D, HG_D), F32)
    conv0_p = jnp.zeros((DEPTH, bp, 2, 2 * D_FF), F32)
    out_p = _trunk(x_prompt.reshape(bp * tp, D_MODEL), mods_p, lw, sample=False, nb=bp, t=tp, pos=jnp.arange(tp),
                   s0=s0_p, conv0=conv0_p, cache_latent=None, cache_krope=None, page_table=None, **shared)
    pos_s = jnp.tile(past + jnp.arange(ts), bs_)
    out_s = _trunk(x_sample.reshape(bs_ * ts, D_MODEL), mods_s, lw, sample=True, nb=bs_, t=ts, pos=pos_s,
                   s0=state_hgrn, conv0=state_conv, cache_latent=cache_latent,
                   cache_krope=jnp.swapaxes(cache_krope, 2, 3),
                   page_table=page_table, **shared)
    y_p, lat_p, kr_p, gv_p, hgrn_p, conv_p = out_p
    y_s, lat_s, kr_s, gv_s, hgrn_s, conv_s = out_s
    return (y_p, y_s, lat_p, kr_p, gv_p, hgrn_p, conv_p, lat_s, kr_s, gv_s, hgrn_s, conv_s)
```

```python
import functools
import math

import numpy as np
import jax
import jax.numpy as jnp
from jax import lax
from jax.experimental import pallas as pl
from jax.experimental.pallas import tpu as pltpu

F32 = jnp.float32
BF16 = jnp.bfloat16

D_MODEL = 1024
DEPTH = 4
PAGE = 128
EPS = 1e-6
NEG_BIG = -1e30
LB_FLOOR = 1e-30
BW = 512
A_CHUNK = 128
A_GROUPS = 4
HG_HEADS = 4
HG_D = 128
C_HEADS = 8
C_NOPE = 64
C_ROPE = 32
C_V = 64
C_QL = 384
C_KVL = 256
QK_W = 384
D_FF = 2816
ROPE_THETA = 10000.0
ATT_SCALE = (C_NOPE + C_ROPE) ** -0.5
LOG2E = math.log2(math.e)
ROPE_LANE = C_NOPE

O_UA, O_VA, O_B, O_CQ, O_CKV, W_HEAD_COLS = 0, 512, 1024, 3072, 3456, 3712
O_KRA, O_KRB, O_GT = 0, 128, 256

VMEM_LIMIT_BYTES = 58 * 1024 * 1024
LANE = 128
FF_COL = 256
N_FF_CHUNKS = D_FF // FF_COL


def _cparams(*sem):
    return pltpu.CompilerParams(dimension_semantics=sem, vmem_limit_bytes=VMEM_LIMIT_BYTES)


def _resident(shape):
    nd = len(shape)
    return pl.BlockSpec(shape, lambda *_: (0,) * nd, pipeline_mode=pl.Buffered(1))


def _resident_layer(stacked, layer):
    shape = stacked.shape[1:]
    return pl.BlockSpec((None,) + shape, lambda *_: (layer,) + (0,) * len(shape), pipeline_mode=pl.Buffered(1))


def _bf(x):
    return x.astype(BF16)


def _dot(a, b):
    return jnp.dot(a, b, preferred_element_type=F32)


def _dot_nt(a, b):
    return lax.dot_general(a, b, (((1,), (1,)), ((), ())), preferred_element_type=F32)


def _dot_tn(a, b):
    return lax.dot_general(a, b, (((0,), (0,)), ((), ())), preferred_element_type=F32)


def _sigmoid(x):
    return 1.0 / (1.0 + jnp.exp(-x))


def _silu(x):
    return x * _sigmoid(x)


def _gelu_tanh(x):
    return 0.5 * x * (1.0 + jnp.tanh(math.sqrt(2.0 / math.pi) * (x + 0.044715 * (x * x * x))))


def _softplus(x):
    return jnp.maximum(x, 0.0) + jnp.log1p(jnp.exp(-jnp.abs(x)))


def _logaddexp(a, b):
    return jnp.maximum(a, b) + jnp.log1p(jnp.exp(-jnp.abs(a - b)))


def _rms(x, g):
    return x * lax.rsqrt(jnp.mean(x * x, axis=-1, keepdims=True) + EPS) * g


def _norm_mod(x, g, sc, sh):
    return _rms(x, g) * (1.0 + sc) + sh


def _split_bf16(x):
    hi = _bf(x)
    lo = _bf(x - hi.astype(F32))
    return hi, lo


def _ada_kernel(c_ref, w_ref, b_ref, o_ref):
    c = c_ref[...]
    o_ref[0] = _dot(_bf(_silu(c)), _bf(w_ref[0])) + b_ref[0]


def _ada(c_all, w_ada, b_ada):
    rows = c_all.shape[0]
    tn = 1536
    return pl.pallas_call(
        _ada_kernel,
        grid=(DEPTH, 6 * D_MODEL // tn),
        in_specs=[
            pl.BlockSpec((rows, D_MODEL), lambda l, j: (0, 0)),
            pl.BlockSpec((1, D_MODEL, tn), lambda l, j: (l, 0, j)),
            pl.BlockSpec((1, 1, tn), lambda l, j: (l, 0, j)),
        ],
        out_specs=pl.BlockSpec((1, rows, tn), lambda l, j: (l, 0, j)),
        out_shape=jax.ShapeDtypeStruct((DEPTH, rows, 6 * D_MODEL), F32),
        compiler_params=_cparams("arbitrary", "arbitrary"),
        name="ada",
    )(c_all, w_ada, b_ada.reshape(DEPTH, 1, 6 * D_MODEL))


def _in_proj_kernel(x_ref, sc_ref, sh_ref, g_ref, w_ref, wt_ref, ws_ref, bs_ref, lng_ref, lnb_ref, qag_ref, kvg_ref,
                    cos_ref, sin_ref, *refs, tm, gv_rows, sample, n_unused):
    refs = refs[:4] + refs[4 + n_unused:]
    if sample:
        wqn_ref, wqa_ref, wqb_ref, wk_ref, ya_ref, gv_ref, zb_ref, gt_ref, q_ref, k_ref, lat_ref, kr_ref = refs
    else:
        wq1_ref, wqb_ref, wkn_ref, wvv_ref, ya_ref, gv_ref, zb_ref, gt_ref, q_ref, k_ref, v_ref, lat_ref, kr_ref = refs
    hb = _bf(_norm_mod(x_ref[...], g_ref[...], sc_ref[0], sh_ref[0]))

    u = _gelu_tanh(_dot(hb, w_ref[:, O_UA:O_UA + BW]))
    va = _gelu_tanh(_dot(hb, w_ref[:, O_VA:O_VA + BW]))
    mu = jnp.mean(va, axis=-1, keepdims=True)
    var = jnp.mean(jnp.square(va - mu), axis=-1, keepdims=True)
    v = (va - mu) * lax.rsqrt(var + EPS) * lng_ref[...] + lnb_ref[...]
    if sample:
        v3 = v.reshape(tm // 8, 8, BW)
        mixed = v3 * ws_ref[0][None] + bs_ref[...][None]
        for j in range(1, 4):
            vj = pltpu.roll(v, j, 0).reshape(tm // 8, 8, BW)
            mixed = mixed + vj * ws_ref[j][None]
        ya_ref[...] = _bf(u * mixed.reshape(tm, BW))
    else:
        row = lax.broadcasted_iota(jnp.int32, (A_CHUNK, A_CHUNK), 0)
        col = lax.broadcasted_iota(jnp.int32, (A_CHUNK, A_CHUNK), 1)
        wc = [_bf(jnp.where(col <= row, ws_ref[g], 0.0)) for g in range(A_GROUPS)]
        for c in range(tm // A_CHUNK):
            rs = slice(c * A_CHUNK, (c + 1) * A_CHUNK)
            for g in range(A_GROUPS):
                cs = slice(g * LANE, (g + 1) * LANE)
                mixed = _dot(wc[g], _bf(v[rs, cs])) + bs_ref[:, cs]
                ya_ref[rs, cs] = _bf(u[rs, cs] * mixed)

    @pl.when(pl.program_id(1) == pl.num_programs(1) - 1)
    def _():
        gv_ref[0] = v[tm - gv_rows:, :]

    zb_ref[...] = _dot(hb, w_ref[:, O_B:O_B + 4 * BW])
    gt_ref[...] = _sigmoid(_dot(hb, wt_ref[:, O_GT:O_GT + 3 * D_MODEL]))

    cos = cos_ref[...]
    sin = sin_ref[...]
    qn = _bf(_rms(_dot(hb, w_ref[:, O_CQ:O_CQ + C_QL]), qag_ref[...]))
    lat = _rms(_dot(hb, w_ref[:, O_CKV:O_CKV + C_KVL]), kvg_ref[...])
    kr = _dot(hb, wt_ref[:, O_KRA:O_KRA + LANE]) * cos + _dot(hb, wt_ref[:, O_KRB:O_KRB + LANE]) * sin
    lat_ref[...] = lat
    kr_ref[...] = kr[:, ROPE_LANE:ROPE_LANE + C_ROPE]
    if sample:
        for h in range(C_HEADS):
            hs = slice(h * LANE, (h + 1) * LANE)
            q_nope = _bf(_dot(qn, wqn_ref[:, hs]))
            q_lat = _dot(q_nope, wk_ref[h])
            q_rope = _dot(qn, wqa_ref[:, hs]) * cos + _dot(qn, wqb_ref[:, hs]) * sin
            q_ref[:, h * QK_W:h * QK_W + C_KVL] = _bf(q_lat * ATT_SCALE)
            q_ref[:, h * QK_W + C_KVL:(h + 1) * QK_W] = _bf(q_rope * ATT_SCALE)
        k_ref[:, :C_KVL] = _bf(lat)
        k_ref[:, C_KVL:] = _bf(kr)
    else:
        lane = lax.broadcasted_iota(jnp.int32, (1, LANE), 1)
        cos_q = cos + jnp.where(lane < C_NOPE, 1.0, 0.0)
        latb = _bf(lat)
        for h in range(C_HEADS):
            hs = slice(h * LANE, (h + 1) * LANE)
            q = _dot(qn, wq1_ref[:, hs]) * cos_q + _dot(qn, wqb_ref[:, hs]) * sin
            q_ref[:, hs] = _bf(q * (ATT_SCALE * LOG2E))
            k_ref[:, hs] = _bf(_dot(latb, wkn_ref[:, hs]) + kr)
        v_ref[...] = _bf(_dot_nt(wvv_ref[...], latb))


def _in_proj(x, sc, sh, g, w, wt, ws, bs, lng, lnb, qag, kvg, cos, sin, attn_w, stacks, *, layer, nb, tm, gv_rows,
             sample):
    n = x.shape[0]
    nt = n // (nb * tm)
    mod_rows = sc.shape[1]
    tok = lambda b, i: (b * nt + i, 0)
    tok_l = lambda b, i: (layer, b * nt + i, 0)
    stacks = list(stacks)
    kern = functools.partial(_in_proj_kernel, tm=tm, gv_rows=gv_rows, sample=sample, n_unused=len(stacks))
    if sample:
        qkv_specs = [pl.BlockSpec((tm, C_HEADS * QK_W), tok), pl.BlockSpec((tm, QK_W), tok)]
        qkv_shapes = [jax.ShapeDtypeStruct((n, C_HEADS * QK_W), BF16), jax.ShapeDtypeStruct((n, QK_W), BF16)]
    else:
        qkv_specs = [pl.BlockSpec((tm, C_HEADS * LANE), tok)] * 2 + [
            pl.BlockSpec((C_HEADS * C_V, tm), lambda b, i: (0, b * nt + i))]
        qkv_shapes = [jax.ShapeDtypeStruct((n, C_HEADS * LANE), BF16)] * 2 + [
            jax.ShapeDtypeStruct((C_HEADS * C_V, n), BF16)]
    n_in = 14 + len(attn_w)
    n_out = 6 + len(qkv_specs)
    return pl.pallas_call(
        kern,
        grid=(nb, nt),
        in_specs=[
            pl.BlockSpec((tm, D_MODEL), tok),
            pl.BlockSpec((1, mod_rows, D_MODEL), lambda b, i: (b, 0, 0)),
            pl.BlockSpec((1, mod_rows, D_MODEL), lambda b, i: (b, 0, 0)),
            _resident((1, D_MODEL)),
            _resident_layer(w, layer),
            _resident_layer(wt, layer),
            _resident(ws.shape),
            _resident(bs.shape),
            _resident((1, BW)),
            _resident((1, BW)),
            _resident((1, C_QL)),
            _resident((1, C_KVL)),
            pl.BlockSpec((tm, LANE), lambda b, i: (i, 0)),
            pl.BlockSpec((tm, LANE), lambda b, i: (i, 0)),
        ] + [_resident_layer(a, layer) for a in attn_w] + [pl.BlockSpec(memory_space=pl.ANY)] * len(stacks),
        out_specs=[
            pl.BlockSpec((tm, BW), tok),
            pl.BlockSpec((None, 1, gv_rows, BW), lambda b, i: (layer, b, 0, 0)),
            pl.BlockSpec((tm, 4 * BW), tok),
            pl.BlockSpec((tm, 3 * D_MODEL), tok),
        ] + qkv_specs + [
            pl.BlockSpec((None, tm, C_KVL), tok_l),
            pl.BlockSpec((None, tm, C_ROPE), tok_l),
        ],
        out_shape=[
            jax.ShapeDtypeStruct((n, BW), BF16),
            jax.ShapeDtypeStruct((DEPTH, nb, gv_rows, BW), F32),
            jax.ShapeDtypeStruct((n, 4 * BW), F32),
            jax.ShapeDtypeStruct((n, 3 * D_MODEL), F32),
        ] + qkv_shapes + [
            jax.ShapeDtypeStruct((DEPTH, n, C_KVL), F32),
            jax.ShapeDtypeStruct((DEPTH, n, C_ROPE), F32),
        ],
        input_output_aliases={n_in: 1, n_in + 1: n_out - 2, n_in + 2: n_out - 1},
        compiler_params=_cparams("arbitrary", "arbitrary"),
        name="in_proj_s" if sample else "in_proj_p",
    )(x, sc, sh, g, w, wt, ws, bs, lng, lnb, qag, kvg, cos, sin, *attn_w, *stacks)


def _hgrn_consts(chunk, seg):
    t = np.arange(chunk)
    u = t[None, :]
    tt = t[:, None]
    same_seg = (t // seg)[:, None] == (t // seg)[None, :]
    blocks = [same_seg & (u <= tt), same_seg & (u > tt)]
    masks = [np.eye(chunk, dtype=bool)]
    b = seg // 2
    while b >= 1:
        pos = t % (2 * b)
        p = t - pos
        m = (p + b - 1)[:, None]
        second = pos >= b
        blocks.append(np.where(second[:, None], (u > m) & (u <= tt), (u > tt) & (u <= m)))
        masks.append((p[:, None] == p[None, :]) & second[:, None] & (~second)[None, :])
        b //= 2
    sums = np.concatenate(blocks, 0).astype(np.float32)
    return jnp.asarray(sums, BF16), jnp.asarray(np.stack(masks).astype(np.float32))


def _hgrn_lb(lbl_ref, layer):
    logits = lbl_ref[...]
    e = jnp.exp(logits - jnp.max(logits, axis=0, keepdims=True))
    p = e / jnp.sum(e, axis=0, keepdims=True)
    lb = jnp.zeros((1, HG_HEADS * HG_D), F32)
    for i in range(1, layer + 1):
        lb = lb + p[i:i + 1, :]
    return lb


def _hgrn_chunk(zb, lb, hg, sums_ref, masks_ref, sel, states, row_masks, chunk):
    nlev = masks_ref.shape[0] - 1
    q = _silu(zb[:, 0:BW])
    zf = zb[:, BW:2 * BW]
    vb = zb[:, 2 * BW:3 * BW]
    gate = _silu(zb[:, 3 * BW:4 * BW])
    log_lb = jnp.log(jnp.maximum(lb, LB_FLOOR))
    logf = _logaddexp(log_lb, jnp.log1p(-lb) - _softplus(-zf))
    k = (1.0 - lb) * _sigmoid(-zf)

    parts = _split_bf16(logf)
    sums = sums_ref[...]
    x = _dot(sums, parts[0]) + _dot(sums, parts[1])
    e = jnp.exp(x)
    e_cum = e[0:chunk]
    e_end = e[chunk:2 * chunk]
    nseq = len(states)

    outs = []
    new_states = [[None] * HG_HEADS for _ in range(nseq)]
    for h in range(HG_HEADS):
        hs = slice(h * HG_D, (h + 1) * HG_D)
        qh, kh, vh = q[:, hs], k[:, hs], _bf(vb[:, hs])
        scores = masks_ref[0] * _dot_nt(_bf(qh), _bf(kh))
        for lev in range(nlev):
            el = e[(2 + lev) * chunk:(3 + lev) * chunk, hs]
            scores = scores + masks_ref[1 + lev] * _dot_nt(_bf(qh * el), _bf(kh * el))
        o = _dot(_bf(scores), vh)
        q_in = _bf(qh * e_cum[:, hs])
        k_out = kh * e_end[:, hs]
        decay = jnp.exp(_dot_tn(parts[0][:, hs], sel) + _dot_tn(parts[1][:, hs], sel))
        for j in range(nseq):
            s_prev = states[j][h]
            if nseq == 1:
                o = o + _dot(q_in, _bf(s_prev))
                kj = _bf(k_out)
            else:
                o = o + row_masks[j] * _dot(q_in, _bf(s_prev))
                kj = _bf(k_out * row_masks[j])
            new_states[j][h] = decay[:, j * HG_D:(j + 1) * HG_D] * s_prev + _dot_tn(kj, vh)
        y = o * lax.rsqrt(jnp.mean(o * o, axis=-1, keepdims=True) + EPS) * hg[:, hs]
        outs.append(_bf(y * gate[:, hs]))
    return outs, new_states


def _hgrn_prompt_kernel(z_ref, lbl_ref, hg_ref, sums_ref, masks_ref, sel_ref, s0_ref, y_ref, sout_ref, state, *,
                        layer, tc, chunk):
    @pl.when(pl.program_id(1) == 0)
    def _():
        state[...] = s0_ref[0]

    lb = _hgrn_lb(lbl_ref, layer)
    hg = hg_ref[...]
    sel = sel_ref[...]
    for c in range(tc // chunk):
        rs = slice(c * chunk, (c + 1) * chunk)
        states = [[state[h] for h in range(HG_HEADS)]]
        outs, new_states = _hgrn_chunk(z_ref[rs, :], lb, hg, sums_ref, masks_ref, sel, states, None, chunk)
        for h in range(HG_HEADS):
            y_ref[rs, h * HG_D:(h + 1) * HG_D] = outs[h]
            state[h] = new_states[0][h]

    @pl.when(pl.program_id(1) == pl.num_programs(1) - 1)
    def _():
        sout_ref[0] = state[...]


def _hgrn_prompt(zb, lb_logits, hgrn_g, s0, *, layer, nb, tc, chunk):
    n = zb.shape[0]
    nt = n // (nb * tc)
    sums, masks = _hgrn_consts(chunk, chunk)
    sel = jnp.ones((chunk, HG_D), BF16)
    tok = lambda b, i: (b * nt + i, 0)
    kern = functools.partial(_hgrn_prompt_kernel, layer=layer, tc=tc, chunk=chunk)
    return pl.pallas_call(
        kern,
        grid=(nb, nt),
        in_specs=[
            pl.BlockSpec((tc, 4 * BW), tok),
            _resident(lb_logits.shape),
            _resident((1, BW)),
            _resident(sums.shape),
            _resident(masks.shape),
            _resident(sel.shape),
            pl.BlockSpec((1, HG_HEADS, HG_D, HG_D), lambda b, i: (b, 0, 0, 0)),
        ],
        out_specs=[
            pl.BlockSpec((tc, BW), tok),
            pl.BlockSpec((1, HG_HEADS, HG_D, HG_D), lambda b, i: (b, 0, 0, 0)),
        ],
        out_shape=[
            jax.ShapeDtypeStruct((n, BW), BF16),
            jax.ShapeDtypeStruct((nb, HG_HEADS, HG_D, HG_D), F32),
        ],
        scratch_shapes=[pltpu.VMEM((HG_HEADS, HG_D, HG_D), F32)],
        compiler_params=_cparams("arbitrary", "arbitrary"),
        name="hgrn_p",
    )(zb, lb_logits, hgrn_g, sums, masks, sel, s0)


def _hgrn_sample_kernel(z_ref, lbl_ref, hg_ref, sums_ref, masks_ref, sel_ref, s0_ref, *refs, layer, rows, seq):
    y_ref, sout_ref = refs[-2:]
    nseq = rows // seq
    lb = _hgrn_lb(lbl_ref, layer)
    ridx = lax.broadcasted_iota(jnp.int32, (rows, 1), 0)
    row_masks = [jnp.where((ridx >= j * seq) & (ridx < (j + 1) * seq), 1.0, 0.0) for j in range(nseq)]
    states = [[s0_ref[j, h] for h in range(HG_HEADS)] for j in range(nseq)]
    outs, new_states = _hgrn_chunk(z_ref[...], lb, hg_ref[...], sums_ref, masks_ref, sel_ref[...], states,
                                   row_masks, rows)
    for h in range(HG_HEADS):
        y_ref[:, h * HG_D:(h + 1) * HG_D] = outs[h]
        for j in range(nseq):
            sout_ref[j, h] = new_states[j][h]


def _hgrn_sample(zb, lb_logits, hgrn_g, s0_all, s_stack, *, layer, seq, rows):
    n = zb.shape[0]
    nseq = rows // seq
    sums, masks = _hgrn_consts(rows, seq)
    sel = np.zeros((rows, nseq * HG_D), np.float32)
    for j in range(nseq):
        sel[j * seq:(j + 1) * seq, j * HG_D:(j + 1) * HG_D] = 1.0
    sel = jnp.asarray(sel, BF16)
    kern = functools.partial(_hgrn_sample_kernel, layer=layer, rows=rows, seq=seq)
    state_spec = pl.BlockSpec((None, nseq, HG_HEADS, HG_D, HG_D), lambda i: (layer, i, 0, 0, 0))
    return pl.pallas_call(
        kern,
        grid=(n // rows,),
        in_specs=[
            pl.BlockSpec((rows, 4 * BW), lambda i: (i, 0)),
            _resident(lb_logits.shape),
            _resident((1, BW)),
            _resident(sums.shape),
            _resident(masks.shape),
            _resident(sel.shape),
            state_spec,
            pl.BlockSpec(memory_space=pl.ANY),
        ],
        out_specs=[pl.BlockSpec((rows, BW), lambda i: (i, 0)), state_spec],
        out_shape=[
            jax.ShapeDtypeStruct((n, BW), BF16),
            jax.ShapeDtypeStruct(s0_all.shape, F32),
        ],
        input_output_aliases={7: 1},
        compiler_params=_cparams("arbitrary"),
        name="hgrn_s",
    )(zb, lb_logits, hgrn_g, sums, masks, sel, s0_all, s_stack)


def _attn_prompt_kernel(qi_ref, kj_ref, q_ref, k_ref, v_ref, o_ref, m_sc, l_sc, acc_sc, *, tb):
    step = pl.program_id(1)
    i = qi_ref[step]
    j = kj_ref[step]

    @pl.when(j == 0)
    def _():
        m_sc[...] = jnp.full(m_sc.shape, NEG_BIG, F32)
        l_sc[...] = jnp.zeros(l_sc.shape, F32)
        acc_sc[...] = jnp.zeros(acc_sc.shape, F32)

    def update(diagonal):
        if diagonal:
            keep = (lax.broadcasted_iota(jnp.int32, (tb, tb), 0) <= lax.broadcasted_iota(jnp.int32, (tb, tb), 1))
        for h in range(C_HEADS):
            hs = slice(h * LANE, (h + 1) * LANE)
            st = _dot_nt(k_ref[:, hs], q_ref[:, hs])
            if diagonal:
                st = jnp.where(keep, st, NEG_BIG)
            m_prev = m_sc[h]
            m_new = jnp.maximum(m_prev, jnp.max(st, axis=0, keepdims=True))
            alpha = jnp.exp2(m_prev - m_new)
            p = jnp.exp2(st - m_new)
            l_sc[h] = alpha * l_sc[h] + jnp.sum(p, axis=0, keepdims=True)
            acc_sc[h] = alpha * acc_sc[h] + _dot(v_ref[h * C_V:(h + 1) * C_V, :], _bf(p))
            m_sc[h] = m_new

    @pl.when(j < i)
    def _():
        update(False)

    @pl.when(j == i)
    def _():
        update(True)
        for h in range(C_HEADS):
            o_ref[h * C_V:(h + 1) * C_V, :] = _bf(acc_sc[h] / l_sc[h])


def _attn_prompt(q, k, vt, *, nb, t, tb):
    n = q.shape[0]
    nblk = t // tb
    pairs = [(i, j) for i in range(nblk) for j in range(i + 1)]
    qi = jnp.asarray([p[0] for p in pairs], jnp.int32)
    kj = jnp.asarray([p[1] for p in pairs], jnp.int32)
    q_map = lambda b, s, qi, kj: (b * nblk + qi[s], 0)
    k_map = lambda b, s, qi, kj: (b * nblk + kj[s], 0)
    grid_spec = pltpu.PrefetchScalarGridSpec(
        num_scalar_prefetch=2,
        grid=(nb, len(pairs)),
        in_specs=[
            pl.BlockSpec((tb, C_HEADS * LANE), q_map),
            pl.BlockSpec((tb, C_HEADS * LANE), k_map),
            pl.BlockSpec((C_HEADS * C_V, tb), lambda b, s, qi, kj: (0, b * nblk + kj[s])),
        ],
        out_specs=pl.BlockSpec((C_HEADS * C_V, tb), lambda b, s, qi, kj: (0, b * nblk + qi[s])),
        scratch_shapes=[
            pltpu.VMEM((C_HEADS, 1, tb), F32),
            pltpu.VMEM((C_HEADS, 1, tb), F32),
            pltpu.VMEM((C_HEADS, C_V, tb), F32),
        ],
    )
    return pl.pallas_call(
        functools.partial(_attn_prompt_kernel, tb=tb),
        grid_spec=grid_spec,
        out_shape=jax.ShapeDtypeStruct((C_HEADS * C_V, n), BF16),
        compiler_params=_cparams("arbitrary", "arbitrary"),
        name="attn_p",
    )(qi, kj, q, k, vt)


def _attn_sample_kernel(pt_ref, q_ref, kn_ref, lat_hbm, kr_hbm, o_ref, lat_buf, kr_buf, sem, *,
                        layer, pages, seq, key_chunk):
    b = pl.program_id(0)
    nb = pl.num_programs(0)
    slot = lax.rem(b, 2)
    rows = C_HEADS * seq

    def page_copies(seq_idx, slot_, with_source):
        copies = []
        for p in range(pages):
            page = pt_ref[seq_idx, p] if with_source else 0
            ks = pl.ds(p * PAGE, PAGE)
            copies.append(pltpu.make_async_copy(lat_hbm.at[layer, page], lat_buf.at[slot_, ks, :], sem.at[0, slot_]))
            copies.append(pltpu.make_async_copy(kr_hbm.at[layer, page], kr_buf.at[slot_, :, ks], sem.at[1, slot_]))
        return copies

    @pl.when(b == 0)
    def _():
        for c in page_copies(0, 0, True):
            c.start()

    for c in page_copies(b, slot, False):
        c.wait()
    nxt = jnp.minimum(b + 1, nb - 1)
    for c in page_copies(nxt, 1 - slot, True):
        c.start()

    q = q_ref[0]
    q_lat = q[:, :C_KVL]
    q_rope = q[:, C_KVL + ROPE_LANE:C_KVL + ROPE_LANE + C_ROPE]
    n_chunks = pages * PAGE // key_chunk
    lats, s = [], []
    for c in range(n_chunks):
        ks = pl.ds(c * key_chunk, key_chunk)
        lats.append(_bf(lat_buf[slot, ks, :]))
        s.append(_dot_nt(q_lat, lats[c]) + _dot(q_rope, _bf(kr_buf[slot, :, ks])))

    qf = q.astype(F32)
    kn = kn_ref[0].astype(F32)
    tpos = lax.broadcasted_iota(jnp.int32, (rows, 1), 0) % seq
    sn = [jnp.where(tpos >= t, jnp.sum(qf * kn[t:t + 1, :], axis=-1, keepdims=True), NEG_BIG) for t in range(seq)]

    m = jnp.max(s[0], axis=-1, keepdims=True)
    for c in range(1, n_chunks):
        m = jnp.maximum(m, jnp.max(s[c], axis=-1, keepdims=True))
    for t in range(seq):
        m = jnp.maximum(m, sn[t])

    l = jnp.zeros((rows, 1), F32)
    acc = jnp.zeros((rows, C_KVL), F32)
    for c in range(n_chunks):
        e = jnp.exp(s[c] - m)
        l = l + jnp.sum(e, axis=-1, keepdims=True)
        acc = acc + _dot(_bf(e), lats[c])
    for t in range(seq):
        e = jnp.exp(sn[t] - m)
        l = l + e
        acc = acc + e * kn[t:t + 1, :C_KVL]
    o_ref[0] = _bf(acc / l)

    @pl.when(b == nb - 1)
    def _():
        for c in page_copies(b, 1 - slot, False):
            c.wait()


def _attn_sample(page_table, q, k_new, cache_latent, cache_krope_t, *, layer, seq):
    nb, pages = page_table.shape
    rows = C_HEADS * seq
    keys = pages * PAGE
    kern = functools.partial(_attn_sample_kernel, layer=layer, pages=pages, seq=seq, key_chunk=min(keys, 2048))
    grid_spec = pltpu.PrefetchScalarGridSpec(
        num_scalar_prefetch=1,
        grid=(nb,),
        in_specs=[
            pl.BlockSpec((1, rows, QK_W), lambda b, pt: (b, 0, 0)),
            pl.BlockSpec((1, seq, QK_W), lambda b, pt: (b, 0, 0)),
            pl.BlockSpec(memory_space=pl.ANY),
            pl.BlockSpec(memory_space=pl.ANY),
        ],
        out_specs=pl.BlockSpec((1, rows, C_KVL), lambda b, pt: (b, 0, 0)),
        scratch_shapes=[
            pltpu.VMEM((2, keys, C_KVL), F32),
            pltpu.VMEM((2, C_ROPE, keys), F32),
            pltpu.SemaphoreType.DMA((2, 2)),
        ],
    )
    return pl.pallas_call(
        kern,
        grid_spec=grid_spec,
        out_shape=jax.ShapeDtypeStruct((nb, rows, C_KVL), BF16),
        compiler_params=_cparams("arbitrary"),
        name="attn_s",
    )(page_table, q, k_new, cache_latent, cache_krope_t)


def _merge_kernel(ya_ref, yb_ref, oc_ref, gt_ref, x_ref, g1_ref, wb_ref, wo_ref, *refs, latent_out):
    if latent_out:
        wv_ref, o_ref = refs
        yc = _bf(_dot(oc_ref[...], wv_ref[...]))
        br_c = _dot(yc, wb_ref[2])
    else:
        (o_ref,) = refs
        br_c = _dot_tn(oc_ref[...], wb_ref[2])
    m = gt_ref[:, 0:D_MODEL] * _dot(ya_ref[...], wb_ref[0])
    m = m + gt_ref[:, D_MODEL:2 * D_MODEL] * _dot(yb_ref[...], wb_ref[1])
    m = m + gt_ref[:, 2 * D_MODEL:3 * D_MODEL] * br_c
    o_ref[...] = x_ref[...] + g1_ref[0] * _dot(_bf(m), wo_ref[...])


def _merge(ya, yb, oc, gt, x, g1, wb, wo, wv, *, layer, nb, tm):
    n = x.shape[0]
    nt = n // (nb * tm)
    mod_rows = g1.shape[1]
    tok = lambda b, i: (b * nt + i, 0)
    extra = [] if wv is None else [wv]
    return pl.pallas_call(
        functools.partial(_merge_kernel, latent_out=wv is not None),
        grid=(nb, nt),
        in_specs=[
            pl.BlockSpec((tm, BW), tok),
            pl.BlockSpec((tm, BW), tok),
            (pl.BlockSpec((BW, tm), lambda b, i: (0, b * nt + i)) if wv is None
             else pl.BlockSpec((tm, oc.shape[1]), tok)),
            pl.BlockSpec((tm, 3 * D_MODEL), tok),
            pl.BlockSpec((tm, D_MODEL), tok),
            pl.BlockSpec((1, mod_rows, D_MODEL), lambda b, i: (b, 0, 0)),
            _resident_layer(wb, layer),
            _resident_layer(wo, layer),
        ] + [_resident_layer(a, layer) for a in extra],
        out_specs=pl.BlockSpec((tm, D_MODEL), tok),
        out_shape=jax.ShapeDtypeStruct((n, D_MODEL), F32),
        compiler_params=_cparams("arbitrary", "arbitrary"),
        name="merge",
    )(ya, yb, oc, gt, x, g1, wb, wo, *extra)


def _ffn_up_prompt_kernel(x_ref, sc_ref, sh_ref, g_ref, w_ref, cw_ref, cb_ref, c0_ref, act_ref, cs_ref, stage, *, tm):
    @pl.when(pl.program_id(1) == 0)
    def _():
        stage[0:8, :] = c0_ref[0]

    hb = _bf(_norm_mod(x_ref[...], g_ref[...], sc_ref[0], sh_ref[0]))

    def conv(cols):
        up = _dot(hb, w_ref[:, cols])
        stage[8:8 + tm, cols] = up
        y = (cb_ref[:, cols] + cw_ref[0:1, cols] * stage[6:6 + tm, cols] + cw_ref[1:2, cols] * stage[7:7 + tm, cols]
             + cw_ref[2:3, cols] * up)
        stage[0:8, cols] = up[tm - 8:, :]
        return y

    for c in range(N_FF_CHUNKS):
        ca = slice(c * FF_COL, (c + 1) * FF_COL)
        cb = slice(D_FF + c * FF_COL, D_FF + (c + 1) * FF_COL)
        act_ref[:, ca] = _bf(_silu(conv(ca)) * conv(cb))

    @pl.when(pl.program_id(1) == pl.num_programs(1) - 1)
    def _():
        cs_ref[0] = stage[0:8, :]


def _ffn_up_prompt(x, sc, sh, g, w_up, conv_w, conv_b, conv0, *, layer, nb, tm):
    n = x.shape[0]
    nt = n // (nb * tm)
    tok = lambda b, i: (b * nt + i, 0)
    kern = functools.partial(_ffn_up_prompt_kernel, tm=tm)
    return pl.pallas_call(
        kern,
        grid=(nb, nt),
        in_specs=[
            pl.BlockSpec((tm, D_MODEL), tok),
            pl.BlockSpec((1, 1, D_MODEL), lambda b, i: (b, 0, 0)),
            pl.BlockSpec((1, 1, D_MODEL), lambda b, i: (b, 0, 0)),
            _resident((1, D_MODEL)),
            _resident_layer(w_up, layer),
            _resident((3, 2 * D_FF)),
            _resident((1, 2 * D_FF)),
            pl.BlockSpec((1, 8, 2 * D_FF), lambda b, i: (b, 0, 0)),
        ],
        out_specs=[
            pl.BlockSpec((tm, D_FF), tok),
            pl.BlockSpec((1, 8, 2 * D_FF), lambda b, i: (b, 0, 0)),
        ],
        out_shape=[
            jax.ShapeDtypeStruct((n, D_FF), BF16),
            jax.ShapeDtypeStruct((nb, 8, 2 * D_FF), F32),
        ],
        scratch_shapes=[pltpu.VMEM((8 + tm, 2 * D_FF), F32)],
        compiler_params=_cparams("arbitrary", "arbitrary"),
        name="ffn_up_p",
    )(x, sc, sh, g, w_up, conv_w, conv_b, conv0)


def _pick_rows(sel, a):
    hi = _bf(a)
    r1 = a - hi.astype(F32)
    mid = _bf(r1)
    lo = _bf(r1 - mid.astype(F32))
    return _dot(sel, hi) + _dot(sel, mid) + _dot(sel, lo)


def _ffn_up_sample_kernel(x_ref, sc_ref, sh_ref, g_ref, wa_ref, wb_ref, cwa_ref, cwb_ref, cba_ref, cbb_ref,
                          c0a_ref, c0b_ref, selp_ref, selo_ref, act_ref, csa_ref, csb_ref, *, n, seq):
    hb = _bf(_norm_mod(x_ref[...], g_ref[...], sc_ref[0], sh_ref[0]))
    tpos = lax.broadcasted_iota(jnp.int32, (n, 1), 0) % seq

    def conv(up, c0, cw, cb):
        s1 = jnp.where(tpos >= 1, pltpu.roll(up, 1, 0), _pick_rows(selp_ref[0], c0))
        s2 = jnp.where(tpos >= 2, pltpu.roll(up, 2, 0), _pick_rows(selp_ref[1], c0))
        return cb + cw[0:1, :] * s2 + cw[1:2, :] * s1 + cw[2:3, :] * up

    up_a = _dot(hb, wa_ref[...])
    up_b = _dot(hb, wb_ref[...])
    act_ref[...] = _bf(_silu(conv(up_a, c0a_ref[...], cwa_ref[...], cba_ref[...]))
                       * conv(up_b, c0b_ref[...], cwb_ref[...], cbb_ref[...]))
    csa_ref[...] = _pick_rows(selo_ref[...], up_a)
    csb_ref[...] = _pick_rows(selo_ref[...], up_b)


def _ffn_up_sample(x, sc, sh, g, w_up, conv_w, conv_b, c0, *, layer, seq):
    n = x.shape[0]
    nseq = n // seq
    t = np.arange(n)
    selp = np.zeros((2, n, 2 * nseq), np.float32)
    first, second = t[t % seq == 0], t[t % seq == 1]
    selp[0, first, 2 * (first // seq) + 1] = 1.0
    selp[1, first, 2 * (first // seq)] = 1.0
    selp[1, second, 2 * (second // seq) + 1] = 1.0
    selo = np.zeros((2 * nseq, n), np.float32)
    for j in range(2):
        selo[2 * np.arange(nseq) + j, np.arange(nseq) * seq + seq - 2 + j] = 1.0
    selp, selo = jnp.asarray(selp, BF16), jnp.asarray(selo, BF16)
    col_a = lambda j: (0, j)
    col_b = lambda j: (0, N_FF_CHUNKS + j)
    kern = functools.partial(_ffn_up_sample_kernel, n=n, seq=seq)
    return pl.pallas_call(
        kern,
        grid=(N_FF_CHUNKS,),
        in_specs=[
            _resident((n, D_MODEL)),
            _resident((1, n, D_MODEL)),
            _resident((1, n, D_MODEL)),
            _resident((1, D_MODEL)),
            pl.BlockSpec((None, D_MODEL, FF_COL), lambda j: (layer, 0, j)),
            pl.BlockSpec((None, D_MODEL, FF_COL), lambda j: (layer, 0, N_FF_CHUNKS + j)),
            pl.BlockSpec((3, FF_COL), col_a),
            pl.BlockSpec((3, FF_COL), col_b),
            pl.BlockSpec((1, FF_COL), col_a),
            pl.BlockSpec((1, FF_COL), col_b),
            pl.BlockSpec((None, 2 * nseq, FF_COL), lambda j: (layer, 0, j)),
            pl.BlockSpec((None, 2 * nseq, FF_COL), lambda j: (layer, 0, N_FF_CHUNKS + j)),
            _resident(selp.shape),
            _resident(selo.shape),
        ],
        out_specs=[
            pl.BlockSpec((n, FF_COL), col_a),
            pl.BlockSpec((2 * nseq, FF_COL), col_a),
            pl.BlockSpec((2 * nseq, FF_COL), col_a),
        ],
        out_shape=[
            jax.ShapeDtypeStruct((n, D_FF), BF16),
            jax.ShapeDtypeStruct((2 * nseq, D_FF), F32),
            jax.ShapeDtypeStruct((2 * nseq, D_FF), F32),
        ],
        compiler_params=_cparams("arbitrary"),
        name="ffn_up_s",
    )(x, sc, sh, g, w_up, w_up, conv_w, conv_w, conv_b, conv_b, c0, c0, selp, selo)


def _ffn_down_kernel(act_ref, x_ref, g2_ref, w_ref, fg_ref, o_ref, *, final):
    xn = x_ref[...] + g2_ref[0] * _dot(act_ref[...], w_ref[...])
    o_ref[...] = _rms(xn, fg_ref[...]) if final else xn


def _ffn_down(act, x, g2, w_down, final_g, *, layer, nb, tm, final):
    n = x.shape[0]
    nt = n // (nb * tm)
    mod_rows = g2.shape[1]
    tok = lambda b, i: (b * nt + i, 0)
    return pl.pallas_call(
        functools.partial(_ffn_down_kernel, final=final),
        grid=(nb, nt),
        in_specs=[
            pl.BlockSpec((tm, D_FF), tok),
            pl.BlockSpec((tm, D_MODEL), tok),
            pl.BlockSpec((1, mod_rows, D_MODEL), lambda b, i: (b, 0, 0)),
            _resident_layer(w_down, layer),
            _resident((1, D_MODEL)),
        ],
        out_specs=pl.BlockSpec((tm, D_MODEL), tok),
        out_shape=jax.ShapeDtypeStruct((n, D_MODEL), F32),
        compiler_params=_cparams("arbitrary", "arbitrary"),
        name="ffn_down",
    )(act, x, g2, w_down, final_g)


def _rope_tables(pos):
    half = C_ROPE // 2
    inv = ROPE_THETA ** (-jnp.arange(half, dtype=F32) / half)
    ang = pos.astype(F32)[:, None] * inv[None, :]
    cos, sin = jnp.cos(ang), jnp.sin(ang)
    return _rope_block(jnp.concatenate([cos, cos], -1)), _rope_block(jnp.concatenate([-sin, sin], -1))


def _rope_block(a):
    lead = a.shape[:-1]
    return jnp.concatenate([jnp.zeros(lead + (ROPE_LANE,), a.dtype), a,
                            jnp.zeros(lead + (LANE - ROPE_LANE - C_ROPE,), a.dtype)], axis=-1)


def _swap_halves(w):
    half = C_ROPE // 2
    return jnp.concatenate([w[..., half:], w[..., :half]], axis=-1)


def _matmul_weights(w_in, w_uq, w_ukv, w_branch, w_o, w_up, w_down):
    krr = w_in[..., W_HEAD_COLS:W_HEAD_COLS + C_ROPE]
    w_head = _bf(w_in[..., :W_HEAD_COLS])
    w_tail = _bf(jnp.concatenate([_rope_block(krr), _rope_block(_swap_halves(krr)),
                                  w_in[..., W_HEAD_COLS + C_ROPE:]], axis=-1))

    def head_blocks(a):
        pad = jnp.zeros(a.shape[:-1] + (LANE - a.shape[-1],), F32)
        return jnp.concatenate([a, pad], -1).reshape(a.shape[:2] + (C_HEADS * LANE,))

    wq = w_uq.reshape(DEPTH, C_QL, C_HEADS, C_NOPE + C_ROPE)
    q_nope, q_rope = wq[..., :C_NOPE], wq[..., C_NOPE:]
    wqn = head_blocks(q_nope)
    wqa = _rope_block(q_rope).reshape(DEPTH, C_QL, C_HEADS * LANE)
    wqb = _rope_block(_swap_halves(q_rope)).reshape(DEPTH, C_QL, C_HEADS * LANE)

    wkv = w_ukv.reshape(DEPTH, C_KVL, C_HEADS, C_NOPE + C_V)
    wk = jnp.transpose(wkv[..., :C_NOPE], (0, 2, 3, 1))
    wk = jnp.concatenate([wk, jnp.zeros((DEPTH, C_HEADS, LANE - C_NOPE, C_KVL), F32)], axis=2)
    wv = jnp.transpose(wkv[..., C_NOPE:], (0, 2, 1, 3))
    eye = jnp.eye(C_HEADS, dtype=F32)
    wv_bd = (wv[:, :, :, None, :] * eye[None, :, None, :, None]).reshape(DEPTH, C_HEADS * C_KVL, C_HEADS * C_V)
    attn_sample = [_bf(wqn), _bf(wqa), _bf(wqb), _bf(wk)]
    wv_t = jnp.transpose(wkv[..., C_NOPE:], (0, 2, 3, 1)).reshape(DEPTH, C_HEADS * C_V, C_KVL)
    attn_prompt = [_bf(wqn + wqa), _bf(wqb), _bf(head_blocks(wkv[..., :C_NOPE])), _bf(wv_t)]
    return dict(w_head=w_head, w_tail=w_tail, attn_s=attn_sample, attn_p=attn_prompt, wv=_bf(wv_bd), wb=_bf(w_branch), wo=_bf(w_o),
                w_up=_bf(w_up), w_down=_bf(w_down))


def _sample_mix_tables(w_s_l, b_s_l, seq):
    t = np.arange(8) % seq
    coef = []
    for j in range(seq):
        src = t - j
        wj = w_s_l[:, t, np.maximum(src, 0)]
        wj = jnp.where(jnp.asarray(src >= 0)[None, :], wj, 0.0)
        coef.append(jnp.repeat(wj.T, LANE, axis=1))
    bias = jnp.repeat(b_s_l[:, t].T, LANE, axis=1)
    return jnp.stack(coef), bias


def _trunk(x, mods, lw, *, sample, nb, t, pos, s0, conv0, cache_latent, cache_krope, page_table,
           w_s, b_s, lnv_g, lnv_b, lb_logits, hgrn_g, q_a_g, kv_a_g, norm1_g, norm2_g, final_g, conv_w, conv_b):
    n = nb * t
    cos, sin = _rope_tables(pos)
    w = lw
    if sample:
        conv0 = conv0.reshape(DEPTH, nb * 2, 2 * D_FF)
    gv_rows = n if sample else A_CHUNK
    stacks = (jnp.zeros((DEPTH, 1 if sample else nb, gv_rows, BW), F32), jnp.zeros((DEPTH, n, C_KVL), F32),
              jnp.zeros((DEPTH, n, C_ROPE), F32))
    s_stack = jnp.zeros(s0.shape, F32) if sample else None
    s_out, conv_out = [], []
    for l in range(DEPTH):
        sh1, sc1, g1, sh2, sc2, g2 = mods[l]
        row = lambda a: a[l].reshape(1, -1)
        if sample:
            ws, bs = _sample_mix_tables(w_s[l], b_s[l], t)
            kw = dict(nb=1, tm=n, gv_rows=n, sample=True)
        else:
            ws, bs = w_s[l], jnp.repeat(b_s[l].T, LANE, axis=1)
            kw = dict(nb=nb, tm=256, gv_rows=A_CHUNK, sample=False)
        outs = _in_proj(
            x, sc1, sh1, row(norm1_g), w["w_head"], w["w_tail"], ws, bs, row(lnv_g), row(lnv_b), row(q_a_g), row(kv_a_g),
            cos, sin, w["attn_s"] if sample else w["attn_p"], stacks, layer=l, **kw)
        stacks = (outs[1], outs[-2], outs[-1])
        if sample:
            ya, _, zb, gt, q, k, _, _ = outs
            yb, s_stack = _hgrn_sample(zb, lb_logits, row(hgrn_g), s0, s_stack, layer=l, seq=t, rows=64)
            q_b = q.reshape(nb, t, C_HEADS, QK_W).transpose(0, 2, 1, 3).reshape(nb, C_HEADS * t, QK_W)
            oc = _attn_sample(page_table, q_b, k.reshape(nb, t, QK_W), cache_latent, cache_krope, layer=l, seq=t)
            oc = oc.reshape(nb, C_HEADS, t, C_KVL).transpose(0, 2, 1, 3).reshape(n, C_HEADS * C_KVL)
            mkw = dict(nb=1, tm=n)
            wv = w["wv"]
        else:
            ya, _, zb, gt, q, k, v, _, _ = outs
            yb, s_new = _hgrn_prompt(zb, lb_logits, row(hgrn_g), s0[l], layer=l, nb=nb, tc=512, chunk=128)
            s_out.append(s_new)
            oc = _attn_prompt(q, k, v, nb=nb, t=t, tb=512)
            mkw = dict(nb=nb, tm=512)
            wv = None
        x = _merge(ya, yb, oc, gt, x, g1, w["wb"], w["wo"], wv, layer=l, **mkw)
        if sample:
            act, cs_a, cs_b = _ffn_up_sample(x, sc2, sh2, row(norm2_g), w["w_up"], conv_w[l], row(conv_b), conv0,
                                             layer=l, seq=t)
            conv_out.append(jnp.concatenate([cs_a, cs_b], axis=-1).reshape(nb, 2, 2 * D_FF))
        else:
            c0 = jnp.concatenate([jnp.zeros((nb, 6, 2 * D_FF), F32), conv0[l]], axis=1)
            act, cs = _ffn_up_prompt(x, sc2, sh2, row(norm2_g), w["w_up"], conv_w[l], row(conv_b), c0,
                                     layer=l, nb=nb, tm=512)
            conv_out.append(cs[:, 6:, :])
        x = _ffn_down(act, x, g2, w["w_down"], final_g.reshape(1, -1), layer=l, final=(l == DEPTH - 1), **mkw)
    gv, lat, kr = stacks
    states = s_stack if sample else jnp.stack(s_out)
    return (x.reshape(nb, t, D_MODEL), lat.reshape(DEPTH, nb, t, C_KVL), kr.reshape(DEPTH, nb, t, C_ROPE),
            gv.reshape(DEPTH, nb, -1, BW), states, jnp.stack(conv_out))


def kernel(x_prompt, x_sample, c_prompt, c_sample, cache_latent, cache_krope, state_hgrn, state_conv, page_table,
           norm1_g, norm2_g, final_g, w_ada, b_ada, w_in, w_s, b_s, lnv_g, lnv_b, lb_logits, hgrn_g, q_a_g, kv_a_g,
           w_uq, w_ukv, w_branch, w_o, w_up, conv_w, conv_b, w_down):
    bp, tp, _ = x_prompt.shape
    bs_, ts, _ = x_sample.shape
    past = page_table.shape[1] * cache_latent.shape[2]

    rows = bp + bs_
    rows_pad = -(-rows // 8) * 8
    c_all = jnp.concatenate([c_prompt, c_sample, jnp.zeros((rows_pad - rows, D_MODEL), F32)], axis=0)
    mod = _ada(c_all, w_ada, b_ada)
    mods_p, mods_s = [], []
    for l in range(DEPTH):
        chunks = [mod[l, :, i * D_MODEL:(i + 1) * D_MODEL] for i in range(6)]
        mods_p.append([m[:bp].reshape(bp, 1, D_MODEL) for m in chunks])
        mods_s.append([jnp.repeat(m[bp:rows], ts, axis=0).reshape(1, bs_ * ts, D_MODEL) for m in chunks])

    lw = _matmul_weights(w_in, w_uq, w_ukv, w_branch, w_o, w_up, w_down)
    shared = dict(w_s=w_s, b_s=b_s, lnv_g=lnv_g, lnv_b=lnv_b, lb_logits=lb_logits, hgrn_g=hgrn_g, q_a_g=q_a_g,
                  kv_a_g=kv_a_g, norm1_g=norm1_g, norm2_g=norm2_g, final_g=final_g, conv_w=conv_w, conv_b=conv_b)

    s0_p = jnp.zeros((DEPTH, bp, HG_HEADS, HG_D, HG_D), F32)
    conv0_p = jnp.zeros((DEPTH, bp, 2, 2 * D_FF), F32)
    out_p = _trunk(x_prompt.reshape(bp * tp, D_MODEL), mods_p, lw, sample=False, nb=bp, t=tp, pos=jnp.arange(tp),
                   s0=s0_p, conv0=conv0_p, cache_latent=None, cache_krope=None, page_table=None, **shared)
    pos_s = jnp.tile(past + jnp.arange(ts), bs_)
    out_s = _trunk(x_sample.reshape(bs_ * ts, D_MODEL), mods_s, lw, sample=True, nb=bs_, t=ts, pos=pos_s,
                   s0=state_hgrn, conv0=state_conv, cache_latent=cache_latent,
                   cache_krope=jnp.swapaxes(cache_krope, 2, 3),
                   page_table=page_table, **shared)
    y_p, lat_p, kr_p, gv_p, hgrn_p, conv_p = out_p
    y_s, lat_s, kr_s, gv_s, hgrn_s, conv_s = out_s
    return (y_p, y_s, lat_p, kr_p, gv_p, hgrn_p, conv_p, lat_s, kr_s, gv_s, hgrn_s, conv_s)
```

```python
import functools
import math

import numpy as np
import jax
import jax.numpy as jnp
from jax import lax
from jax.experimental import pallas as pl
from jax.experimental.pallas import tpu as pltpu

F32 = jnp.float32
BF16 = jnp.bfloat16

D_MODEL = 1024
DEPTH = 4
PAGE = 128
EPS = 1e-6
NEG_BIG = -1e30
LB_FLOOR = 1e-30
BW = 512
A_CHUNK = 128
A_GROUPS = 4
HG_HEADS = 4
HG_D = 128
C_HEADS = 8
C_NOPE = 64
C_ROPE = 32
C_V = 64
C_QL = 384
C_KVL = 256
QK_W = 384
D_FF = 2816
ROPE_THETA = 10000.0
ATT_SCALE = (C_NOPE + C_ROPE) ** -0.5
LOG2E = math.log2(math.e)
ROPE_LANE = C_NOPE

O_UA, O_VA, O_B, O_CQ, O_CKV, W_HEAD_COLS = 0, 512, 1024, 3072, 3456, 3712
O_KRA, O_KRB, O_GT = 0, 128, 256

VMEM_LIMIT_BYTES = 58 * 1024 * 1024
LANE = 128
FF_COL = 256
N_FF_CHUNKS = D_FF // FF_COL


def _cparams(*sem):
    return pltpu.CompilerParams(dimension_semantics=sem, vmem_limit_bytes=VMEM_LIMIT_BYTES)


def _resident(shape):
    nd = len(shape)
    return pl.BlockSpec(shape, lambda *_: (0,) * nd, pipeline_mode=pl.Buffered(1))


def _resident_layer(stacked, layer):
    shape = stacked.shape[1:]
    return pl.BlockSpec((None,) + shape, lambda *_: (layer,) + (0,) * len(shape), pipeline_mode=pl.Buffered(1))


def _bf(x):
    return x.astype(BF16)


def _dot(a, b):
    return jnp.dot(a, b, preferred_element_type=F32)


def _dot_nt(a, b):
    return lax.dot_general(a, b, (((1,), (1,)), ((), ())), preferred_element_type=F32)


def _dot_tn(a, b):
    return lax.dot_general(a, b, (((0,), (0,)), ((), ())), preferred_element_type=F32)


def _sigmoid(x):
    return 1.0 / (1.0 + jnp.exp(-x))


def _silu(x):
    return x * _sigmoid(x)


def _gelu_tanh(x):
    return 0.5 * x * (1.0 + jnp.tanh(math.sqrt(2.0 / math.pi) * (x + 0.044715 * (x * x * x))))


def _softplus(x):
    return jnp.maximum(x, 0.0) + jnp.log1p(jnp.exp(-jnp.abs(x)))


def _logaddexp(a, b):
    return jnp.maximum(a, b) + jnp.log1p(jnp.exp(-jnp.abs(a - b)))


def _rms(x, g):
    return x * lax.rsqrt(jnp.mean(x * x, axis=-1, keepdims=True) + EPS) * g


def _norm_mod(x, g, sc, sh):
    return _rms(x, g) * (1.0 + sc) + sh


def _split_bf16(x):
    hi = _bf(x)
    lo = _bf(x - hi.astype(F32))
    return hi, lo


def _ada_kernel(c_ref, w_ref, b_ref, o_ref):
    c = c_ref[...]
    o_ref[0] = _dot(_bf(_silu(c)), _bf(w_ref[0])) + b_ref[0]


def _ada(c_all, w_ada, b_ada):
    rows = c_all.shape[0]
    tn = 1536
    return pl.pallas_call(
        _ada_kernel,
        grid=(DEPTH, 6 * D_MODEL // tn),
        in_specs=[
            pl.BlockSpec((rows, D_MODEL), lambda l, j: (0, 0)),
            pl.BlockSpec((1, D_MODEL, tn), lambda l, j: (l, 0, j)),
            pl.BlockSpec((1, 1, tn), lambda l, j: (l, 0, j)),
        ],
        out_specs=pl.BlockSpec((1, rows, tn), lambda l, j: (l, 0, j)),
        out_shape=jax.ShapeDtypeStruct((DEPTH, rows, 6 * D_MODEL), F32),
        compiler_params=_cparams("arbitrary", "arbitrary"),
        name="ada",
    )(c_all, w_ada, b_ada.reshape(DEPTH, 1, 6 * D_MODEL))


def _in_proj_kernel(x_ref, sc_ref, sh_ref, g_ref, w_ref, wt_ref, ws_ref, bs_ref, lng_ref, lnb_ref, qag_ref, kvg_ref,
                    cos_ref, sin_ref, *refs, tm, gv_rows, sample, n_unused):
    refs = refs[:4] + refs[4 + n_unused:]
    if sample:
        wqn_ref, wqa_ref, wqb_ref, wk_ref, ya_ref, gv_ref, zb_ref, gt_ref, q_ref, k_ref, lat_ref, kr_ref = refs
    else:
        wq1_ref, wqb_ref, wkn_ref, wvv_ref, ya_ref, gv_ref, zb_ref, gt_ref, q_ref, k_ref, v_ref, lat_ref, kr_ref = refs
    hb = _bf(_norm_mod(x_ref[...], g_ref[...], sc_ref[0], sh_ref[0]))

    u = _gelu_tanh(_dot(hb, w_ref[:, O_UA:O_UA + BW]))
    va = _gelu_tanh(_dot(hb, w_ref[:, O_VA:O_VA + BW]))
    mu = jnp.mean(va, axis=-1, keepdims=True)
    var = jnp.mean(jnp.square(va - mu), axis=-1, keepdims=True)
    v = (va - mu) * lax.rsqrt(var + EPS) * lng_ref[...] + lnb_ref[...]
    if sample:
        v3 = v.reshape(tm // 8, 8, BW)
        mixed = v3 * ws_ref[0][None] + bs_ref[...][None]
        for j in range(1, 4):
            vj = pltpu.roll(v, j, 0).reshape(tm // 8, 8, BW)
            mixed = mixed + vj * ws_ref[j][None]
        ya_ref[...] = _bf(u * mixed.reshape(tm, BW))
    else:
        row = lax.broadcasted_iota(jnp.int32, (A_CHUNK, A_CHUNK), 0)
        col = lax.broadcasted_iota(jnp.int32, (A_CHUNK, A_CHUNK), 1)
        wc = [_bf(jnp.where(col <= row, ws_ref[g], 0.0)) for g in range(A_GROUPS)]
        for c in range(tm // A_CHUNK):
            rs = slice(c * A_CHUNK, (c + 1) * A_CHUNK)
            for g in range(A_GROUPS):
                cs = slice(g * LANE, (g + 1) * LANE)
                mixed = _dot(wc[g], _bf(v[rs, cs])) + bs_ref[:, cs]
                ya_ref[rs, cs] = _bf(u[rs, cs] * mixed)

    @pl.when(pl.program_id(1) == pl.num_programs(1) - 1)
    def _():
        gv_ref[0] = v[tm - gv_rows:, :]

    zb_ref[...] = _dot(hb, w_ref[:, O_B:O_B + 4 * BW])
    gt_ref[...] = _sigmoid(_dot(hb, wt_ref[:, O_GT:O_GT + 3 * D_MODEL]))

    cos = cos_ref[...]
    sin = sin_ref[...]
    qn = _bf(_rms(_dot(hb, w_ref[:, O_CQ:O_CQ + C_QL]), qag_ref[...]))
    lat = _rms(_dot(hb, w_ref[:, O_CKV:O_CKV + C_KVL]), kvg_ref[...])
    kr = _dot(hb, wt_ref[:, O_KRA:O_KRA + LANE]) * cos + _dot(hb, wt_ref[:, O_KRB:O_KRB + LANE]) * sin
    lat_ref[...] = lat
    kr_ref[...] = kr[:, ROPE_LANE:ROPE_LANE + C_ROPE]
    if sample:
        for h in range(C_HEADS):
            hs = slice(h * LANE, (h + 1) * LANE)
            q_nope = _bf(_dot(qn, wqn_ref[:, hs]))
            q_lat = _dot(q_nope, wk_ref[h])
            q_rope = _dot(qn, wqa_ref[:, hs]) * cos + _dot(qn, wqb_ref[:, hs]) * sin
            q_ref[:, h * QK_W:h * QK_W + C_KVL] = _bf(q_lat * ATT_SCALE)
            q_ref[:, h * QK_W + C_KVL:(h + 1) * QK_W] = _bf(q_rope * ATT_SCALE)
        k_ref[:, :C_KVL] = _bf(lat)
        k_ref[:, C_KVL:] = _bf(kr)
    else:
        lane = lax.broadcasted_iota(jnp.int32, (1, LANE), 1)
        cos_q = cos + jnp.where(lane < C_NOPE, 1.0, 0.0)
        latb = _bf(lat)
        for h in range(C_HEADS):
            hs = slice(h * LANE, (h + 1) * LANE)
            q = _dot(qn, wq1_ref[:, hs]) * cos_q + _dot(qn, wqb_ref[:, hs]) * sin
            q_ref[:, hs] = _bf(q * (ATT_SCALE * LOG2E))
            k_ref[:, hs] = _bf(_dot(latb, wkn_ref[:, hs]) + kr)
        v_ref[...] = _bf(_dot_nt(wvv_ref[...], latb))


def _in_proj(x, sc, sh, g, w, wt, ws, bs, lng, lnb, qag, kvg, cos, sin, attn_w, stacks, *, layer, nb, tm, gv_rows,
             sample):
    n = x.shape[0]
    nt = n // (nb * tm)
    mod_rows = sc.shape[1]
    tok = lambda b, i: (b * nt + i, 0)
    tok_l = lambda b, i: (layer, b * nt + i, 0)
    stacks = list(stacks)
    kern = functools.partial(_in_proj_kernel, tm=tm, gv_rows=gv_rows, sample=sample, n_unused=len(stacks))
    if sample:
        qkv_specs = [pl.BlockSpec((tm, C_HEADS * QK_W), tok), pl.BlockSpec((tm, QK_W), tok)]
        qkv_shapes = [jax.ShapeDtypeStruct((n, C_HEADS * QK_W), BF16), jax.ShapeDtypeStruct((n, QK_W), BF16)]
    else:
        qkv_specs = [pl.BlockSpec((tm, C_HEADS * LANE), tok)] * 2 + [
            pl.BlockSpec((C_HEADS * C_V, tm), lambda b, i: (0, b * nt + i))]
        qkv_shapes = [jax.ShapeDtypeStruct((n, C_HEADS * LANE), BF16)] * 2 + [
            jax.ShapeDtypeStruct((C_HEADS * C_V, n), BF16)]
    n_in = 14 + len(attn_w)
    n_out = 6 + len(qkv_specs)
    return pl.pallas_call(
        kern,
        grid=(nb, nt),
        in_specs=[
            pl.BlockSpec((tm, D_MODEL), tok),
            pl.BlockSpec((1, mod_rows, D_MODEL), lambda b, i: (b, 0, 0)),
            pl.BlockSpec((1, mod_rows, D_MODEL), lambda b, i: (b, 0, 0)),
            _resident((1, D_MODEL)),
            _resident_layer(w, layer),
            _resident_layer(wt, layer),
            _resident(ws.shape),
            _resident(bs.shape),
            _resident((1, BW)),
            _resident((1, BW)),
            _resident((1, C_QL)),
            _resident((1, C_KVL)),
            pl.BlockSpec((tm, LANE), lambda b, i: (i, 0)),
            pl.BlockSpec((tm, LANE), lambda b, i: (i, 0)),
        ] + [_resident_layer(a, layer) for a in attn_w] + [pl.BlockSpec(memory_space=pl.ANY)] * len(stacks),
        out_specs=[
            pl.BlockSpec((tm, BW), tok),
            pl.BlockSpec((None, 1, gv_rows, BW), lambda b, i: (layer, b, 0, 0)),
            pl.BlockSpec((tm, 4 * BW), tok),
            pl.BlockSpec((tm, 3 * D_MODEL), tok),
        ] + qkv_specs + [
            pl.BlockSpec((None, tm, C_KVL), tok_l),
            pl.BlockSpec((None, tm, C_ROPE), tok_l),
        ],
        out_shape=[
            jax.ShapeDtypeStruct((n, BW), BF16),
            jax.ShapeDtypeStruct((DEPTH, nb, gv_rows, BW), F32),
            jax.ShapeDtypeStruct((n, 4 * BW), F32),
            jax.ShapeDtypeStruct((n, 3 * D_MODEL), F32),
        ] + qkv_shapes + [
            jax.ShapeDtypeStruct((DEPTH, n, C_KVL), F32),
            jax.ShapeDtypeStruct((DEPTH, n, C_ROPE), F32),
        ],
        input_output_aliases={n_in: 1, n_in + 1: n_out - 2, n_in + 2: n_out - 1},
        compiler_params=_cparams("arbitrary", "arbitrary"),
        name="in_proj_s" if sample else "in_proj_p",
    )(x, sc, sh, g, w, wt, ws, bs, lng, lnb, qag, kvg, cos, sin, *attn_w, *stacks)


def _hgrn_consts(chunk, seg):
    t = np.arange(chunk)
    u = t[None, :]
    tt = t[:, None]
    same_seg = (t // seg)[:, None] == (t // seg)[None, :]
    blocks = [same_seg & (u <= tt), same_seg & (u > tt)]
    masks = [np.eye(chunk, dtype=bool)]
    b = seg // 2
    while b >= 1:
        pos = t % (2 * b)
        p = t - pos
        m = (p + b - 1)[:, None]
        second = pos >= b
        blocks.append(np.where(second[:, None], (u > m) & (u <= tt), (u > tt) & (u <= m)))
        masks.append((p[:, None] == p[None, :]) & second[:, None] & (~second)[None, :])
        b //= 2
    sums = np.concatenate(blocks, 0).astype(np.float32)
    return jnp.asarray(sums, BF16), jnp.asarray(np.stack(masks).astype(np.float32))


def _hgrn_lb(lbl_ref, layer):
    logits = lbl_ref[...]
    e = jnp.exp(logits - jnp.max(logits, axis=0, keepdims=True))
    p = e / jnp.sum(e, axis=0, keepdims=True)
    lb = jnp.zeros((1, HG_HEADS * HG_D), F32)
    for i in range(1, layer + 1):
        lb = lb + p[i:i + 1, :]
    return lb


def _hgrn_chunk(zb, lb, hg, sums_ref, masks_ref, sel, states, row_masks, chunk):
    nlev = masks_ref.shape[0] - 1
    q = _silu(zb[:, 0:BW])
    zf = zb[:, BW:2 * BW]
    vb = zb[:, 2 * BW:3 * BW]
    gate = _silu(zb[:, 3 * BW:4 * BW])
    log_lb = jnp.log(jnp.maximum(lb, LB_FLOOR))
    logf = _logaddexp(log_lb, jnp.log1p(-lb) - _softplus(-zf))
    k = (1.0 - lb) * _sigmoid(-zf)

    parts = _split_bf16(logf)
    sums = sums_ref[...]
    x = _dot(sums, parts[0]) + _dot(sums, parts[1])
    e = jnp.exp(x)
    e_cum = e[0:chunk]
    e_end = e[chunk:2 * chunk]
    nseq = len(states)

    outs = []
    new_states = [[None] * HG_HEADS for _ in range(nseq)]
    all_scores = []
    for h in range(HG_HEADS):
        hs = slice(h * HG_D, (h + 1) * HG_D)
        qh, kh = q[:, hs], k[:, hs]
        scores = masks_ref[0] * _dot_nt(_bf(qh), _bf(kh))
        for lev in range(nlev):
            el = e[(2 + lev) * chunk:(3 + lev) * chunk, hs]
            scores = scores + masks_ref[1 + lev] * _dot_nt(_bf(qh * el), _bf(kh * el))
        all_scores.append(_bf(scores))
    for h in range(HG_HEADS):
        hs = slice(h * HG_D, (h + 1) * HG_D)
        qh, kh, vh = q[:, hs], k[:, hs], _bf(vb[:, hs])
        o = _dot(all_scores[h], vh)
        q_in = _bf(qh * e_cum[:, hs])
        k_out = kh * e_end[:, hs]
        decay = jnp.exp(_dot_tn(parts[0][:, hs], sel) + _dot_tn(parts[1][:, hs], sel))
        for j in range(nseq):
            s_prev = states[j][h]
            if nseq == 1:
                o = o + _dot(q_in, _bf(s_prev))
                kj = _bf(k_out)
            else:
                o = o + row_masks[j] * _dot(q_in, _bf(s_prev))
                kj = _bf(k_out * row_masks[j])
            new_states[j][h] = decay[:, j * HG_D:(j + 1) * HG_D] * s_prev + _dot_tn(kj, vh)
        y = o * lax.rsqrt(jnp.mean(o * o, axis=-1, keepdims=True) + EPS) * hg[:, hs]
        outs.append(_bf(y * gate[:, hs]))
    return outs, new_states


def _hgrn_prompt_kernel(z_ref, lbl_ref, hg_ref, sums_ref, masks_ref, sel_ref, s0_ref, y_ref, sout_ref, state, *,
                        layer, tc, chunk):
    @pl.when(pl.program_id(1) == 0)
    def _():
        state[...] = s0_ref[0]

    lb = _hgrn_lb(lbl_ref, layer)
    hg = hg_ref[...]
    sel = sel_ref[...]
    for c in range(tc // chunk):
        rs = slice(c * chunk, (c + 1) * chunk)
        states = [[state[h] for h in range(HG_HEADS)]]
        outs, new_states = _hgrn_chunk(z_ref[rs, :], lb, hg, sums_ref, masks_ref, sel, states, None, chunk)
        for h in range(HG_HEADS):
            y_ref[rs, h * HG_D:(h + 1) * HG_D] = outs[h]
            state[h] = new_states[0][h]

    @pl.when(pl.program_id(1) == pl.num_programs(1) - 1)
    def _():
        sout_ref[0] = state[...]


def _hgrn_prompt(zb, lb_logits, hgrn_g, s0, *, layer, nb, tc, chunk):
    n = zb.shape[0]
    nt = n // (nb * tc)
    sums, masks = _hgrn_consts(chunk, chunk)
    sel = jnp.ones((chunk, HG_D), BF16)
    tok = lambda b, i: (b * nt + i, 0)
    kern = functools.partial(_hgrn_prompt_kernel, layer=layer, tc=tc, chunk=chunk)
    return pl.pallas_call(
        kern,
        grid=(nb, nt),
        in_specs=[
            pl.BlockSpec((tc, 4 * BW), tok),
            _resident(lb_logits.shape),
            _resident((1, BW)),
            _resident(sums.shape),
            _resident(masks.shape),
            _resident(sel.shape),
            pl.BlockSpec((1, HG_HEADS, HG_D, HG_D), lambda b, i: (b, 0, 0, 0)),
        ],
        out_specs=[
            pl.BlockSpec((tc, BW), tok),
            pl.BlockSpec((1, HG_HEADS, HG_D, HG_D), lambda b, i: (b, 0, 0, 0)),
        ],
        out_shape=[
            jax.ShapeDtypeStruct((n, BW), BF16),
            jax.ShapeDtypeStruct((nb, HG_HEADS, HG_D, HG_D), F32),
        ],
        scratch_shapes=[pltpu.VMEM((HG_HEADS, HG_D, HG_D), F32)],
        compiler_params=_cparams("arbitrary", "arbitrary"),
        name="hgrn_p",
    )(zb, lb_logits, hgrn_g, sums, masks, sel, s0)


def _hgrn_sample_kernel(z_ref, lbl_ref, hg_ref, sums_ref, masks_ref, sel_ref, s0_ref, *refs, layer, rows, seq):
    y_ref, sout_ref = refs[-2:]
    nseq = rows // seq
    lb = _hgrn_lb(lbl_ref, layer)
    ridx = lax.broadcasted_iota(jnp.int32, (rows, 1), 0)
    row_masks = [jnp.where((ridx >= j * seq) & (ridx < (j + 1) * seq), 1.0, 0.0) for j in range(nseq)]
    states = [[s0_ref[j, h] for h in range(HG_HEADS)] for j in range(nseq)]
    outs, new_states = _hgrn_chunk(z_ref[...], lb, hg_ref[...], sums_ref, masks_ref, sel_ref[...], states,
                                   row_masks, rows)
    for h in range(HG_HEADS):
        y_ref[:, h * HG_D:(h + 1) * HG_D] = outs[h]
        for j in range(nseq):
            sout_ref[j, h] = new_states[j][h]


def _hgrn_sample(zb, lb_logits, hgrn_g, s0_all, s_stack, *, layer, seq, rows):
    n = zb.shape[0]
    nseq = rows // seq
    sums, masks = _hgrn_consts(rows, seq)
    sel = np.zeros((rows, nseq * HG_D), np.float32)
    for j in range(nseq):
        sel[j * seq:(j + 1) * seq, j * HG_D:(j + 1) * HG_D] = 1.0
    sel = jnp.asarray(sel, BF16)
    kern = functools.partial(_hgrn_sample_kernel, layer=layer, rows=rows, seq=seq)
    state_spec = pl.BlockSpec((None, nseq, HG_HEADS, HG_D, HG_D), lambda i: (layer, i, 0, 0, 0))
    return pl.pallas_call(
        kern,
        grid=(n // rows,),
        in_specs=[
            pl.BlockSpec((rows, 4 * BW), lambda i: (i, 0)),
            _resident(lb_logits.shape),
            _resident((1, BW)),
            _resident(sums.shape),
            _resident(masks.shape),
            _resident(sel.shape),
            state_spec,
            pl.BlockSpec(memory_space=pl.ANY),
        ],
        out_specs=[pl.BlockSpec((rows, BW), lambda i: (i, 0)), state_spec],
        out_shape=[
            jax.ShapeDtypeStruct((n, BW), BF16),
            jax.ShapeDtypeStruct(s0_all.shape, F32),
        ],
        input_output_aliases={7: 1},
        compiler_params=_cparams("arbitrary"),
        name="hgrn_s",
    )(zb, lb_logits, hgrn_g, sums, masks, sel, s0_all, s_stack)


def _attn_prompt_kernel(qi_ref, kj_ref, q_ref, k_ref, v_ref, o_ref, m_sc, l_sc, acc_sc, *, tb):
    step = pl.program_id(1)
    i = qi_ref[step]
    j = kj_ref[step]

    @pl.when(j == 0)
    def _():
        m_sc[...] = jnp.full(m_sc.shape, NEG_BIG, F32)
        l_sc[...] = jnp.zeros(l_sc.shape, F32)
        acc_sc[...] = jnp.zeros(acc_sc.shape, F32)

    def update(diagonal):
        if diagonal:
            keep = (lax.broadcasted_iota(jnp.int32, (tb, tb), 0) <= lax.broadcasted_iota(jnp.int32, (tb, tb), 1))
        sts = []
        for h in range(C_HEADS):
            hs = slice(h * LANE, (h + 1) * LANE)
            st = _dot_nt(k_ref[:, hs], q_ref[:, hs])
            sts.append(jnp.where(keep, st, NEG_BIG) if diagonal else st)
        for h in range(C_HEADS):
            st = sts[h]
            m_prev = m_sc[h]
            m_new = jnp.maximum(m_prev, jnp.max(st, axis=0, keepdims=True))
            alpha = jnp.exp2(m_prev - m_new)
            p = jnp.exp2(st - m_new)
            l_sc[h] = alpha * l_sc[h] + jnp.sum(p, axis=0, keepdims=True)
            acc_sc[h] = alpha * acc_sc[h] + _dot(v_ref[h * C_V:(h + 1) * C_V, :], _bf(p))
            m_sc[h] = m_new

    @pl.when(j < i)
    def _():
        update(False)

    @pl.when(j == i)
    def _():
        update(True)
        for h in range(C_HEADS):
            o_ref[h * C_V:(h + 1) * C_V, :] = _bf(acc_sc[h] / l_sc[h])


def _attn_prompt(q, k, vt, *, nb, t, tb):
    n = q.shape[0]
    nblk = t // tb
    pairs = [(i, j) for i in range(nblk) for j in range(i + 1)]
    qi = jnp.asarray([p[0] for p in pairs], jnp.int32)
    kj = jnp.asarray([p[1] for p in pairs], jnp.int32)
    q_map = lambda b, s, qi, kj: (b * nblk + qi[s], 0)
    k_map = lambda b, s, qi, kj: (b * nblk + kj[s], 0)
    grid_spec = pltpu.PrefetchScalarGridSpec(
        num_scalar_prefetch=2,
        grid=(nb, len(pairs)),
        in_specs=[
            pl.BlockSpec((tb, C_HEADS * LANE), q_map),
            pl.BlockSpec((tb, C_HEADS * LANE), k_map),
            pl.BlockSpec((C_HEADS * C_V, tb), lambda b, s, qi, kj: (0, b * nblk + kj[s])),
        ],
        out_specs=pl.BlockSpec((C_HEADS * C_V, tb), lambda b, s, qi, kj: (0, b * nblk + qi[s])),
        scratch_shapes=[
            pltpu.VMEM((C_HEADS, 1, tb), F32),
            pltpu.VMEM((C_HEADS, 1, tb), F32),
            pltpu.VMEM((C_HEADS, C_V, tb), F32),
        ],
    )
    return pl.pallas_call(
        functools.partial(_attn_prompt_kernel, tb=tb),
        grid_spec=grid_spec,
        out_shape=jax.ShapeDtypeStruct((C_HEADS * C_V, n), BF16),
        compiler_params=_cparams("arbitrary", "arbitrary"),
        name="attn_p",
    )(qi, kj, q, k, vt)


def _attn_sample_kernel(pt_ref, q_ref, kn_ref, lat_hbm, kr_hbm, o_ref, lat_buf, kr_buf, sem, *,
                        layer, pages, seq, key_chunk):
    b = pl.program_id(0)
    nb = pl.num_programs(0)
    slot = lax.rem(b, 2)
    rows = C_HEADS * seq

    def page_copies(seq_idx, slot_, with_source):
        copies = []
        for p in range(pages):
            page = pt_ref[seq_idx, p] if with_source else 0
            ks = pl.ds(p * PAGE, PAGE)
            copies.append(pltpu.make_async_copy(lat_hbm.at[layer, page], lat_buf.at[slot_, ks, :], sem.at[0, slot_]))
            copies.append(pltpu.make_async_copy(kr_hbm.at[layer, page], kr_buf.at[slot_, :, ks], sem.at[1, slot_]))
        return copies

    @pl.when(b == 0)
    def _():
        for c in page_copies(0, 0, True):
            c.start()

    for c in page_copies(b, slot, False):
        c.wait()
    nxt = jnp.minimum(b + 1, nb - 1)
    for c in page_copies(nxt, 1 - slot, True):
        c.start()

    q = q_ref[0]
    q_lat = q[:, :C_KVL]
    q_rope = q[:, C_KVL + ROPE_LANE:C_KVL + ROPE_LANE + C_ROPE]
    n_chunks = pages * PAGE // key_chunk
    lats, s = [], []
    for c in range(n_chunks):
        ks = pl.ds(c * key_chunk, key_chunk)
        lats.append(_bf(lat_buf[slot, ks, :]))
        s.append(_dot_nt(q_lat, lats[c]) + _dot(q_rope, _bf(kr_buf[slot, :, ks])))

    qf = q.astype(F32)
    kn = kn_ref[0].astype(F32)
    tpos = lax.broadcasted_iota(jnp.int32, (rows, 1), 0) % seq
    sn = [jnp.where(tpos >= t, jnp.sum(qf * kn[t:t + 1, :], axis=-1, keepdims=True), NEG_BIG) for t in range(seq)]

    m = jnp.max(s[0], axis=-1, keepdims=True)
    for c in range(1, n_chunks):
        m = jnp.maximum(m, jnp.max(s[c], axis=-1, keepdims=True))
    for t in range(seq):
        m = jnp.maximum(m, sn[t])

    l = jnp.zeros((rows, 1), F32)
    acc = jnp.zeros((rows, C_KVL), F32)
    for c in range(n_chunks):
        e = jnp.exp(s[c] - m)
        l = l + jnp.sum(e, axis=-1, keepdims=True)
        acc = acc + _dot(_bf(e), lats[c])
    for t in range(seq):
        e = jnp.exp(sn[t] - m)
        l = l + e
        acc = acc + e * kn[t:t + 1, :C_KVL]
    o_ref[0] = _bf(acc / l)

    @pl.when(b == nb - 1)
    def _():
        for c in page_copies(b, 1 - slot, False):
            c.wait()


def _attn_sample(page_table, q, k_new, cache_latent, cache_krope_t, *, layer, seq):
    nb, pages = page_table.shape
    rows = C_HEADS * seq
    keys = pages * PAGE
    kern = functools.partial(_attn_sample_kernel, layer=layer, pages=pages, seq=seq, key_chunk=min(keys, 2048))
    grid_spec = pltpu.PrefetchScalarGridSpec(
        num_scalar_prefetch=1,
        grid=(nb,),
        in_specs=[
            pl.BlockSpec((1, rows, QK_W), lambda b, pt: (b, 0, 0)),
            pl.BlockSpec((1, seq, QK_W), lambda b, pt: (b, 0, 0)),
            pl.BlockSpec(memory_space=pl.ANY),
            pl.BlockSpec(memory_space=pl.ANY),
        ],
        out_specs=pl.BlockSpec((1, rows, C_KVL), lambda b, pt: (b, 0, 0)),
        scratch_shapes=[
            pltpu.VMEM((2, keys, C_KVL), F32),
            pltpu.VMEM((2, C_ROPE, keys), F32),
            pltpu.SemaphoreType.DMA((2, 2)),
        ],
    )
    return pl.pallas_call(
        kern,
        grid_spec=grid_spec,
        out_shape=jax.ShapeDtypeStruct((nb, rows, C_KVL), BF16),
        compiler_params=_cparams("arbitrary"),
        name="attn_s",
    )(page_table, q, k_new, cache_latent, cache_krope_t)


def _merge_kernel(ya_ref, yb_ref, oc_ref, gt_ref, x_ref, g1_ref, wb_ref, wo_ref, *refs, latent_out):
    if latent_out:
        wv_ref, o_ref = refs
        yc = _bf(_dot(oc_ref[...], wv_ref[...]))
        br_c = _dot(yc, wb_ref[2])
    else:
        (o_ref,) = refs
        br_c = _dot_tn(oc_ref[...], wb_ref[2])
    m = gt_ref[:, 0:D_MODEL] * _dot(ya_ref[...], wb_ref[0])
    m = m + gt_ref[:, D_MODEL:2 * D_MODEL] * _dot(yb_ref[...], wb_ref[1])
    m = m + gt_ref[:, 2 * D_MODEL:3 * D_MODEL] * br_c
    o_ref[...] = x_ref[...] + g1_ref[0] * _dot(_bf(m), wo_ref[...])


def _merge(ya, yb, oc, gt, x, g1, wb, wo, wv, *, layer, nb, tm):
    n = x.shape[0]
    nt = n // (nb * tm)
    mod_rows = g1.shape[1]
    tok = lambda b, i: (b * nt + i, 0)
    extra = [] if wv is None else [wv]
    return pl.pallas_call(
        functools.partial(_merge_kernel, latent_out=wv is not None),
        grid=(nb, nt),
        in_specs=[
            pl.BlockSpec((tm, BW), tok),
            pl.BlockSpec((tm, BW), tok),
            (pl.BlockSpec((BW, tm), lambda b, i: (0, b * nt + i)) if wv is None
             else pl.BlockSpec((tm, oc.shape[1]), tok)),
            pl.BlockSpec((tm, 3 * D_MODEL), tok),
            pl.BlockSpec((tm, D_MODEL), tok),
            pl.BlockSpec((1, mod_rows, D_MODEL), lambda b, i: (b, 0, 0)),
            _resident_layer(wb, layer),
            _resident_layer(wo, layer),
        ] + [_resident_layer(a, layer) for a in extra],
        out_specs=pl.BlockSpec((tm, D_MODEL), tok),
        out_shape=jax.ShapeDtypeStruct((n, D_MODEL), F32),
        compiler_params=_cparams("arbitrary", "arbitrary"),
        name="merge",
    )(ya, yb, oc, gt, x, g1, wb, wo, *extra)


def _ffn_up_prompt_kernel(x_ref, sc_ref, sh_ref, g_ref, w_ref, cw_ref, cb_ref, c0_ref, act_ref, cs_ref, stage, *, tm):
    @pl.when(pl.program_id(1) == 0)
    def _():
        stage[0:8, :] = c0_ref[0]

    hb = _bf(_norm_mod(x_ref[...], g_ref[...], sc_ref[0], sh_ref[0]))

    def conv(cols):
        up = _dot(hb, w_ref[:, cols])
        stage[8:8 + tm, cols] = up
        y = (cb_ref[:, cols] + cw_ref[0:1, cols] * stage[6:6 + tm, cols] + cw_ref[1:2, cols] * stage[7:7 + tm, cols]
             + cw_ref[2:3, cols] * up)
        stage[0:8, cols] = up[tm - 8:, :]
        return y

    for c in range(N_FF_CHUNKS):
        ca = slice(c * FF_COL, (c + 1) * FF_COL)
        cb = slice(D_FF + c * FF_COL, D_FF + (c + 1) * FF_COL)
        act_ref[:, ca] = _bf(_silu(conv(ca)) * conv(cb))

    @pl.when(pl.program_id(1) == pl.num_programs(1) - 1)
    def _():
        cs_ref[0] = stage[0:8, :]


def _ffn_up_prompt(x, sc, sh, g, w_up, conv_w, conv_b, conv0, *, layer, nb, tm):
    n = x.shape[0]
    nt = n // (nb * tm)
    tok = lambda b, i: (b * nt + i, 0)
    kern = functools.partial(_ffn_up_prompt_kernel, tm=tm)
    return pl.pallas_call(
        kern,
        grid=(nb, nt),
        in_specs=[
            pl.BlockSpec((tm, D_MODEL), tok),
            pl.BlockSpec((1, 1, D_MODEL), lambda b, i: (b, 0, 0)),
            pl.BlockSpec((1, 1, D_MODEL), lambda b, i: (b, 0, 0)),
            _resident((1, D_MODEL)),
            _resident_layer(w_up, layer),
            _resident((3, 2 * D_FF)),
            _resident((1, 2 * D_FF)),
            pl.BlockSpec((1, 8, 2 * D_FF), lambda b, i: (b, 0, 0)),
        ],
        out_specs=[
            pl.BlockSpec((tm, D_FF), tok),
            pl.BlockSpec((1, 8, 2 * D_FF), lambda b, i: (b, 0, 0)),
        ],
        out_shape=[
            jax.ShapeDtypeStruct((n, D_FF), BF16),
            jax.ShapeDtypeStruct((nb, 8, 2 * D_FF), F32),
        ],
        scratch_shapes=[pltpu.VMEM((8 + tm, 2 * D_FF), F32)],
        compiler_params=_cparams("arbitrary", "arbitrary"),
        name="ffn_up_p",
    )(x, sc, sh, g, w_up, conv_w, conv_b, conv0)


def _pick_rows(sel, a):
    hi = _bf(a)
    r1 = a - hi.astype(F32)
    mid = _bf(r1)
    lo = _bf(r1 - mid.astype(F32))
    return _dot(sel, hi) + _dot(sel, mid) + _dot(sel, lo)


def _ffn_up_sample_kernel(x_ref, sc_ref, sh_ref, g_ref, wa_ref, wb_ref, cwa_ref, cwb_ref, cba_ref, cbb_ref,
                          c0a_ref, c0b_ref, selp_ref, selo_ref, act_ref, csa_ref, csb_ref, *, n, seq):
    hb = _bf(_norm_mod(x_ref[...], g_ref[...], sc_ref[0], sh_ref[0]))
    tpos = lax.broadcasted_iota(jnp.int32, (n, 1), 0) % seq

    def conv(up, c0, cw, cb):
        s1 = jnp.where(tpos >= 1, pltpu.roll(up, 1, 0), _pick_rows(selp_ref[0], c0))
        s2 = jnp.where(tpos >= 2, pltpu.roll(up, 2, 0), _pick_rows(selp_ref[1], c0))
        return cb + cw[0:1, :] * s2 + cw[1:2, :] * s1 + cw[2:3, :] * up

    up_a = _dot(hb, wa_ref[...])
    up_b = _dot(hb, wb_ref[...])
    act_ref[...] = _bf(_silu(conv(up_a, c0a_ref[...], cwa_ref[...], cba_ref[...]))
                       * conv(up_b, c0b_ref[...], cwb_ref[...], cbb_ref[...]))
    csa_ref[...] = _pick_rows(selo_ref[...], up_a)
    csb_ref[...] = _pick_rows(selo_ref[...], up_b)


def _ffn_up_sample(x, sc, sh, g, w_up, conv_w, conv_b, c0, *, layer, seq):
    n = x.shape[0]
    nseq = n // seq
    t = np.arange(n)
    selp = np.zeros((2, n, 2 * nseq), np.float32)
    first, second = t[t % seq == 0], t[t % seq == 1]
    selp[0, first, 2 * (first // seq) + 1] = 1.0
    selp[1, first, 2 * (first // seq)] = 1.0
    selp[1, second, 2 * (second // seq) + 1] = 1.0
    selo = np.zeros((2 * nseq, n), np.float32)
    for j in range(2):
        selo[2 * np.arange(nseq) + j, np.arange(nseq) * seq + seq - 2 + j] = 1.0
    selp, selo = jnp.asarray(selp, BF16), jnp.asarray(selo, BF16)
    col_a = lambda j: (0, j)
    col_b = lambda j: (0, N_FF_CHUNKS + j)
    kern = functools.partial(_ffn_up_sample_kernel, n=n, seq=seq)
    return pl.pallas_call(
        kern,
        grid=(N_FF_CHUNKS,),
        in_specs=[
            _resident((n, D_MODEL)),
            _resident((1, n, D_MODEL)),
            _resident((1, n, D_MODEL)),
            _resident((1, D_MODEL)),
            pl.BlockSpec((None, D_MODEL, FF_COL), lambda j: (layer, 0, j)),
            pl.BlockSpec((None, D_MODEL, FF_COL), lambda j: (layer, 0, N_FF_CHUNKS + j)),
            pl.BlockSpec((3, FF_COL), col_a),
            pl.BlockSpec((3, FF_COL), col_b),
            pl.BlockSpec((1, FF_COL), col_a),
            pl.BlockSpec((1, FF_COL), col_b),
            pl.BlockSpec((None, 2 * nseq, FF_COL), lambda j: (layer, 0, j)),
            pl.BlockSpec((None, 2 * nseq, FF_COL), lambda j: (layer, 0, N_FF_CHUNKS + j)),
            _resident(selp.shape),
            _resident(selo.shape),
        ],
        out_specs=[
            pl.BlockSpec((n, FF_COL), col_a),
            pl.BlockSpec((2 * nseq, FF_COL), col_a),
            pl.BlockSpec((2 * nseq, FF_COL), col_a),
        ],
        out_shape=[
            jax.ShapeDtypeStruct((n, D_FF), BF16),
            jax.ShapeDtypeStruct((2 * nseq, D_FF), F32),
            jax.ShapeDtypeStruct((2 * nseq, D_FF), F32),
        ],
        compiler_params=_cparams("arbitrary"),
        name="ffn_up_s",
    )(x, sc, sh, g, w_up, w_up, conv_w, conv_w, conv_b, conv_b, c0, c0, selp, selo)


def _ffn_down_kernel(act_ref, x_ref, g2_ref, w_ref, fg_ref, o_ref, *, final):
    xn = x_ref[...] + g2_ref[0] * _dot(act_ref[...], w_ref[...])
    o_ref[...] = _rms(xn, fg_ref[...]) if final else xn


def _ffn_down(act, x, g2, w_down, final_g, *, layer, nb, tm, final):
    n = x.shape[0]
    nt = n // (nb * tm)
    mod_rows = g2.shape[1]
    tok = lambda b, i: (b * nt + i, 0)
    return pl.pallas_call(
        functools.partial(_ffn_down_kernel, final=final),
        grid=(nb, nt),
        in_specs=[
            pl.BlockSpec((tm, D_FF), tok),
            pl.BlockSpec((tm, D_MODEL), tok),
            pl.BlockSpec((1, mod_rows, D_MODEL), lambda b, i: (b, 0, 0)),
            _resident_layer(w_down, layer),
            _resident((1, D_MODEL)),
        ],
        out_specs=pl.BlockSpec((tm, D_MODEL), tok),
        out_shape=jax.ShapeDtypeStruct((n, D_MODEL), F32),
        compiler_params=_cparams("arbitrary", "arbitrary"),
        name="ffn_down",
    )(act, x, g2, w_down, final_g)


def _rope_tables(pos):
    half = C_ROPE // 2
    inv = ROPE_THETA ** (-jnp.arange(half, dtype=F32) / half)
    ang = pos.astype(F32)[:, None] * inv[None, :]
    cos, sin = jnp.cos(ang), jnp.sin(ang)
    return _rope_block(jnp.concatenate([cos, cos], -1)), _rope_block(jnp.concatenate([-sin, sin], -1))


def _rope_block(a):
    lead = a.shape[:-1]
    return jnp.concatenate([jnp.zeros(lead + (ROPE_LANE,), a.dtype), a,
                            jnp.zeros(lead + (LANE - ROPE_LANE - C_ROPE,), a.dtype)], axis=-1)


def _swap_halves(w):
    half = C_ROPE // 2
    return jnp.concatenate([w[..., half:], w[..., :half]], axis=-1)


def _matmul_weights(w_in, w_uq, w_ukv, w_branch, w_o, w_up, w_down):
    krr = w_in[..., W_HEAD_COLS:W_HEAD_COLS + C_ROPE]
    w_head = _bf(w_in[..., :W_HEAD_COLS])
    w_tail = _bf(jnp.concatenate([_rope_block(krr), _rope_block(_swap_halves(krr)),
                                  w_in[..., W_HEAD_COLS + C_ROPE:]], axis=-1))

    def head_blocks(a):
        pad = jnp.zeros(a.shape[:-1] + (LANE - a.shape[-1],), F32)
        return jnp.concatenate([a, pad], -1).reshape(a.shape[:2] + (C_HEADS * LANE,))

    wq = w_uq.reshape(DEPTH, C_QL, C_HEADS, C_NOPE + C_ROPE)
    q_nope, q_rope = wq[..., :C_NOPE], wq[..., C_NOPE:]
    wqn = head_blocks(q_nope)
    wqa = _rope_block(q_rope).reshape(DEPTH, C_QL, C_HEADS * LANE)
    wqb = _rope_block(_swap_halves(q_rope)).reshape(DEPTH, C_QL, C_HEADS * LANE)

    wkv = w_ukv.reshape(DEPTH, C_KVL, C_HEADS, C_NOPE + C_V)
    wk = jnp.transpose(wkv[..., :C_NOPE], (0, 2, 3, 1))
    wk = jnp.concatenate([wk, jnp.zeros((DEPTH, C_HEADS, LANE - C_NOPE, C_KVL), F32)], axis=2)
    wv = jnp.transpose(wkv[..., C_NOPE:], (0, 2, 1, 3))
    eye = jnp.eye(C_HEADS, dtype=F32)
    wv_bd = (wv[:, :, :, None, :] * eye[None, :, None, :, None]).reshape(DEPTH, C_HEADS * C_KVL, C_HEADS * C_V)
    attn_sample = [_bf(wqn), _bf(wqa), _bf(wqb), _bf(wk)]
    wv_t = jnp.transpose(wkv[..., C_NOPE:], (0, 2, 3, 1)).reshape(DEPTH, C_HEADS * C_V, C_KVL)
    attn_prompt = [_bf(wqn + wqa), _bf(wqb), _bf(head_blocks(wkv[..., :C_NOPE])), _bf(wv_t)]
    return dict(w_head=w_head, w_tail=w_tail, attn_s=attn_sample, attn_p=attn_prompt, wv=_bf(wv_bd), wb=_bf(w_branch), wo=_bf(w_o),
                w_up=_bf(w_up), w_down=_bf(w_down))


def _sample_mix_tables(w_s_l, b_s_l, seq):
    t = np.arange(8) % seq
    coef = []
    for j in range(seq):
        src = t - j
        wj = w_s_l[:, t, np.maximum(src, 0)]
        wj = jnp.where(jnp.asarray(src >= 0)[None, :], wj, 0.0)
        coef.append(jnp.repeat(wj.T, LANE, axis=1))
    bias = jnp.repeat(b_s_l[:, t].T, LANE, axis=1)
    return jnp.stack(coef), bias


def _trunk(x, mods, lw, *, sample, nb, t, pos, s0, conv0, cache_latent, cache_krope, page_table,
           w_s, b_s, lnv_g, lnv_b, lb_logits, hgrn_g, q_a_g, kv_a_g, norm1_g, norm2_g, final_g, conv_w, conv_b):
    n = nb * t
    cos, sin = _rope_tables(pos)
    w = lw
    if sample:
        conv0 = conv0.reshape(DEPTH, nb * 2, 2 * D_FF)
    gv_rows = n if sample else A_CHUNK
    stacks = (jnp.zeros((DEPTH, 1 if sample else nb, gv_rows, BW), F32), jnp.zeros((DEPTH, n, C_KVL), F32),
              jnp.zeros((DEPTH, n, C_ROPE), F32))
    s_stack = jnp.zeros(s0.shape, F32) if sample else None
    s_out, conv_out = [], []
    for l in range(DEPTH):
        sh1, sc1, g1, sh2, sc2, g2 = mods[l]
        row = lambda a: a[l].reshape(1, -1)
        if sample:
            ws, bs = _sample_mix_tables(w_s[l], b_s[l], t)
            kw = dict(nb=1, tm=n, gv_rows=n, sample=True)
        else:
            ws, bs = w_s[l], jnp.repeat(b_s[l].T, LANE, axis=1)
            kw = dict(nb=nb, tm=256, gv_rows=A_CHUNK, sample=False)
        outs = _in_proj(
            x, sc1, sh1, row(norm1_g), w["w_head"], w["w_tail"], ws, bs, row(lnv_g), row(lnv_b), row(q_a_g), row(kv_a_g),
            cos, sin, w["attn_s"] if sample else w["attn_p"], stacks, layer=l, **kw)
        stacks = (outs[1], outs[-2], outs[-1])
        if sample:
            ya, _, zb, gt, q, k, _, _ = outs
            yb, s_stack = _hgrn_sample(zb, lb_logits, row(hgrn_g), s0, s_stack, layer=l, seq=t, rows=64)
            q_b = q.reshape(nb, t, C_HEADS, QK_W).transpose(0, 2, 1, 3).reshape(nb, C_HEADS * t, QK_W)
            oc = _attn_sample(page_table, q_b, k.reshape(nb, t, QK_W), cache_latent, cache_krope, layer=l, seq=t)
            oc = oc.reshape(nb, C_HEADS, t, C_KVL).transpose(0, 2, 1, 3).reshape(n, C_HEADS * C_KVL)
            mkw = dict(nb=1, tm=n)
            wv = w["wv"]
        else:
            ya, _, zb, gt, q, k, v, _, _ = outs
            yb, s_new = _hgrn_prompt(zb, lb_logits, row(hgrn_g), s0[l], layer=l, nb=nb, tc=512, chunk=128)
            s_out.append(s_new)
            oc = _attn_prompt(q, k, v, nb=nb, t=t, tb=512)
            mkw = dict(nb=nb, tm=512)
            wv = None
        x = _merge(ya, yb, oc, gt, x, g1, w["wb"], w["wo"], wv, layer=l, **mkw)
        if sample:
            act, cs_a, cs_b = _ffn_up_sample(x, sc2, sh2, row(norm2_g), w["w_up"], conv_w[l], row(conv_b), conv0,
                                             layer=l, seq=t)
            conv_out.append(jnp.concatenate([cs_a, cs_b], axis=-1).reshape(nb, 2, 2 * D_FF))
        else:
            c0 = jnp.concatenate([jnp.zeros((nb, 6, 2 * D_FF), F32), conv0[l]], axis=1)
            act, cs = _ffn_up_prompt(x, sc2, sh2, row(norm2_g), w["w_up"], conv_w[l], row(conv_b), c0,
                                     layer=l, nb=nb, tm=512)
            conv_out.append(cs[:, 6:, :])
        x = _ffn_down(act, x, g2, w["w_down"], final_g.reshape(1, -1), layer=l, final=(l == DEPTH - 1), **mkw)
    gv, lat, kr = stacks
    states = s_stack if sample else jnp.stack(s_out)
    return (x.reshape(nb, t, D_MODEL), lat.reshape(DEPTH, nb, t, C_KVL), kr.reshape(DEPTH, nb, t, C_ROPE),
            gv.reshape(DEPTH, nb, -1, BW), states, jnp.stack(conv_out))


def kernel(x_prompt, x_sample, c_prompt, c_sample, cache_latent, cache_krope, state_hgrn, state_conv, page_table,
           norm1_g, norm2_g, final_g, w_ada, b_ada, w_in, w_s, b_s, lnv_g, lnv_b, lb_logits, hgrn_g, q_a_g, kv_a_g,
           w_uq, w_ukv, w_branch, w_o, w_up, conv_w, conv_b, w_down):
    bp, tp, _ = x_prompt.shape
    bs_, ts, _ = x_sample.shape
    past = page_table.shape[1] * cache_latent.shape[2]

    rows = bp + bs_
    rows_pad = -(-rows // 8) * 8
    c_all = jnp.concatenate([c_prompt, c_sample, jnp.zeros((rows_pad - rows, D_MODEL), F32)], axis=0)
    mod = _ada(c_all, w_ada, b_ada)
    mods_p, mods_s = [], []
    for l in range(DEPTH):
        chunks = [mod[l, :, i * D_MODEL:(i + 1) * D_MODEL] for i in range(6)]
        mods_p.append([m[:bp].reshape(bp, 1, D_MODEL) for m in chunks])
        mods_s.append([jnp.repeat(m[bp:rows], ts, axis=0).reshape(1, bs_ * ts, D_MODEL) for m in chunks])

    lw = _matmul_weights(w_in, w_uq, w_ukv, w_branch, w_o, w_up, w_down)
    shared = dict(w_s=w_s, b_s=b_s, lnv_g=lnv_g, lnv_b=lnv_b, lb_logits=lb_logits, hgrn_g=hgrn_g, q_a_g=q_a_g,
                  kv_a_g=kv_a_g, norm1_g=norm1_g, norm2_g=norm2_g, final_g=final_g, conv_w=conv_w, conv_b=conv_b)

    s0_p = jnp.zeros((DEPTH, bp, HG_HEADS, HG_D, HG_D), F32)
    conv0_p = jnp.zeros((DEPTH, bp, 2, 2 * D_FF), F32)
    out_p = _trunk(x_prompt.reshape(bp * tp, D_MODEL), mods_p, lw, sample=False, nb=bp, t=tp, pos=jnp.arange(tp),
                   s0=s0_p, conv0=conv0_p, cache_latent=None, cache_krope=None, page_table=None, **shared)
    pos_s = jnp.tile(past + jnp.arange(ts), bs_)
    out_s = _trunk(x_sample.reshape(bs_ * ts, D_MODEL), mods_s, lw, sample=True, nb=bs_, t=ts, pos=pos_s,
                   s0=state_hgrn, conv0=state_conv, cache_latent=cache_latent,
                   cache_krope=jnp.swapaxes(cache_krope, 2, 3),
                   page_table=page_table, **shared)
    y_p, lat_p, kr_p, gv_p, hgrn_p, conv_p = out_p
    y_s, lat_s, kr_s, gv_s, hgrn_s, conv_s = out_s
    return (y_p, y_s, lat_p, kr_p, gv_p, hgrn_p, conv_p, lat_s, kr_s, gv_s, hgrn_s, conv_s)
```

```python
import functools
import math

import numpy as np
import jax
import jax.numpy as jnp
from jax import lax
from jax.experimental import pallas as pl
from jax.experimental.pallas import tpu as pltpu

F32 = jnp.float32
BF16 = jnp.bfloat16

D_MODEL = 1024
DEPTH = 4
PAGE = 128
EPS = 1e-6
NEG_BIG = -1e30
LB_FLOOR = 1e-30
BW = 512
A_CHUNK = 128
A_GROUPS = 4
HG_HEADS = 4
HG_D = 128
C_HEADS = 8
C_NOPE = 64
C_ROPE = 32
C_V = 64
C_QL = 384
C_KVL = 256
QK_W = 384
D_FF = 2816
ROPE_THETA = 10000.0
ATT_SCALE = (C_NOPE + C_ROPE) ** -0.5
LOG2E = math.log2(math.e)
ROPE_LANE = C_NOPE

O_UA, O_VA, O_B, O_CQ, O_CKV, W_HEAD_COLS = 0, 512, 1024, 3072, 3456, 3712
O_KRA, O_KRB, O_GT = 0, 128, 256

VMEM_LIMIT_BYTES = 58 * 1024 * 1024
LANE = 128
FF_COL = 256
N_FF_CHUNKS = D_FF // FF_COL


def _cparams(*sem):
    return pltpu.CompilerParams(dimension_semantics=sem, vmem_limit_bytes=VMEM_LIMIT_BYTES)


def _resident(shape):
    nd = len(shape)
    return pl.BlockSpec(shape, lambda *_: (0,) * nd, pipeline_mode=pl.Buffered(1))


def _resident_layer(stacked, layer):
    shape = stacked.shape[1:]
    return pl.BlockSpec((None,) + shape, lambda *_: (layer,) + (0,) * len(shape), pipeline_mode=pl.Buffered(1))


def _bf(x):
    return x.astype(BF16)


def _dot(a, b):
    return jnp.dot(a, b, preferred_element_type=F32)


def _dot_nt(a, b):
    return lax.dot_general(a, b, (((1,), (1,)), ((), ())), preferred_element_type=F32)


def _dot_tn(a, b):
    return lax.dot_general(a, b, (((0,), (0,)), ((), ())), preferred_element_type=F32)


def _sigmoid(x):
    return 1.0 / (1.0 + jnp.exp(-x))


def _silu(x):
    return x * _sigmoid(x)


def _gelu_tanh(x):
    return 0.5 * x * (1.0 + jnp.tanh(math.sqrt(2.0 / math.pi) * (x + 0.044715 * (x * x * x))))


def _softplus(x):
    return jnp.maximum(x, 0.0) + jnp.log1p(jnp.exp(-jnp.abs(x)))


def _logaddexp(a, b):
    return jnp.maximum(a, b) + jnp.log1p(jnp.exp(-jnp.abs(a - b)))


def _rms(x, g):
    return x * lax.rsqrt(jnp.mean(x * x, axis=-1, keepdims=True) + EPS) * g


def _norm_mod(x, g, sc, sh):
    return _rms(x, g) * (1.0 + sc) + sh


def _split_bf16(x):
    hi = _bf(x)
    lo = _bf(x - hi.astype(F32))
    return hi, lo


def _ada_kernel(c_ref, w_ref, b_ref, o_ref):
    c = c_ref[...]
    o_ref[0] = _dot(_bf(_silu(c)), _bf(w_ref[0])) + b_ref[0]


def _ada(c_all, w_ada, b_ada):
    rows = c_all.shape[0]
    tn = 1536
    return pl.pallas_call(
        _ada_kernel,
        grid=(DEPTH, 6 * D_MODEL // tn),
        in_specs=[
            pl.BlockSpec((rows, D_MODEL), lambda l, j: (0, 0)),
            pl.BlockSpec((1, D_MODEL, tn), lambda l, j: (l, 0, j)),
            pl.BlockSpec((1, 1, tn), lambda l, j: (l, 0, j)),
        ],
        out_specs=pl.BlockSpec((1, rows, tn), lambda l, j: (l, 0, j)),
        out_shape=jax.ShapeDtypeStruct((DEPTH, rows, 6 * D_MODEL), F32),
        compiler_params=_cparams("arbitrary", "arbitrary"),
        name="ada",
    )(c_all, w_ada, b_ada.reshape(DEPTH, 1, 6 * D_MODEL))


def _in_proj_kernel(x_ref, sc_ref, sh_ref, g_ref, w_ref, wt_ref, ws_ref, bs_ref, lng_ref, lnb_ref, qag_ref, kvg_ref,
                    cos_ref, sin_ref, *refs, tm, gv_rows, sample, n_unused):
    refs = refs[:4] + refs[4 + n_unused:]
    if sample:
        wqn_ref, wqa_ref, wqb_ref, wk_ref, ya_ref, gv_ref, zb_ref, gt_ref, q_ref, k_ref, lat_ref, kr_ref = refs
    else:
        wq1_ref, wqb_ref, wkn_ref, wvv_ref, ya_ref, gv_ref, zb_ref, gt_ref, q_ref, k_ref, v_ref, lat_ref, kr_ref = refs
    hb = _bf(_norm_mod(x_ref[...], g_ref[...], sc_ref[0], sh_ref[0]))

    u = _gelu_tanh(_dot(hb, w_ref[:, O_UA:O_UA + BW]))
    va = _gelu_tanh(_dot(hb, w_ref[:, O_VA:O_VA + BW]))
    mu = jnp.mean(va, axis=-1, keepdims=True)
    var = jnp.mean(jnp.square(va - mu), axis=-1, keepdims=True)
    v = (va - mu) * lax.rsqrt(var + EPS) * lng_ref[...] + lnb_ref[...]
    if sample:
        v3 = v.reshape(tm // 8, 8, BW)
        mixed = v3 * ws_ref[0][None] + bs_ref[...][None]
        for j in range(1, 4):
            vj = pltpu.roll(v, j, 0).reshape(tm // 8, 8, BW)
            mixed = mixed + vj * ws_ref[j][None]
        ya_ref[...] = _bf(u * mixed.reshape(tm, BW))
    else:
        row = lax.broadcasted_iota(jnp.int32, (A_CHUNK, A_CHUNK), 0)
        col = lax.broadcasted_iota(jnp.int32, (A_CHUNK, A_CHUNK), 1)
        wc = [_bf(jnp.where(col <= row, ws_ref[g], 0.0)) for g in range(A_GROUPS)]
        for c in range(tm // A_CHUNK):
            rs = slice(c * A_CHUNK, (c + 1) * A_CHUNK)
            for g in range(A_GROUPS):
                cs = slice(g * LANE, (g + 1) * LANE)
                mixed = _dot(wc[g], _bf(v[rs, cs])) + bs_ref[:, cs]
                ya_ref[rs, cs] = _bf(u[rs, cs] * mixed)

    @pl.when(pl.program_id(1) == pl.num_programs(1) - 1)
    def _():
        gv_ref[0] = v[tm - gv_rows:, :]

    zb_ref[...] = _dot(hb, w_ref[:, O_B:O_B + 4 * BW])
    gt_ref[...] = _sigmoid(_dot(hb, wt_ref[:, O_GT:O_GT + 3 * D_MODEL]))

    cos = cos_ref[...]
    sin = sin_ref[...]
    qn = _bf(_rms(_dot(hb, w_ref[:, O_CQ:O_CQ + C_QL]), qag_ref[...]))
    lat = _rms(_dot(hb, w_ref[:, O_CKV:O_CKV + C_KVL]), kvg_ref[...])
    kr = _dot(hb, wt_ref[:, O_KRA:O_KRA + LANE]) * cos + _dot(hb, wt_ref[:, O_KRB:O_KRB + LANE]) * sin
    lat_ref[...] = lat
    kr_ref[...] = kr[:, ROPE_LANE:ROPE_LANE + C_ROPE]
    if sample:
        for h in range(C_HEADS):
            hs = slice(h * LANE, (h + 1) * LANE)
            q_nope = _bf(_dot(qn, wqn_ref[:, hs]))
            q_lat = _dot(q_nope, wk_ref[h])
            q_rope = _dot(qn, wqa_ref[:, hs]) * cos + _dot(qn, wqb_ref[:, hs]) * sin
            q_ref[:, h * QK_W:h * QK_W + C_KVL] = _bf(q_lat * ATT_SCALE)
            q_ref[:, h * QK_W + C_KVL:(h + 1) * QK_W] = _bf(q_rope * ATT_SCALE)
        k_ref[:, :C_KVL] = _bf(lat)
        k_ref[:, C_KVL:] = _bf(kr)
    else:
        lane = lax.broadcasted_iota(jnp.int32, (1, LANE), 1)
        cos_q = cos + jnp.where(lane < C_NOPE, 1.0, 0.0)
        latb = _bf(lat)
        for h in range(C_HEADS):
            hs = slice(h * LANE, (h + 1) * LANE)
            q = _dot(qn, wq1_ref[:, hs]) * cos_q + _dot(qn, wqb_ref[:, hs]) * sin
            q_ref[:, hs] = _bf(q * (ATT_SCALE * LOG2E))
            k_ref[:, hs] = _bf(_dot(latb, wkn_ref[:, hs]) + kr)
        v_ref[...] = _bf(_dot_nt(wvv_ref[...], latb))


def _in_proj(x, sc, sh, g, w, wt, ws, bs, lng, lnb, qag, kvg, cos, sin, attn_w, stacks, *, layer, nb, tm, gv_rows,
             sample):
    n = x.shape[0]
    nt = n // (nb * tm)
    mod_rows = sc.shape[1]
    tok = lambda b, i: (b * nt + i, 0)
    tok_l = lambda b, i: (layer, b * nt + i, 0)
    stacks = list(stacks)
    kern = functools.partial(_in_proj_kernel, tm=tm, gv_rows=gv_rows, sample=sample, n_unused=len(stacks))
    if sample:
        qkv_specs = [pl.BlockSpec((tm, C_HEADS * QK_W), tok), pl.BlockSpec((tm, QK_W), tok)]
        qkv_shapes = [jax.ShapeDtypeStruct((n, C_HEADS * QK_W), BF16), jax.ShapeDtypeStruct((n, QK_W), BF16)]
    else:
        qkv_specs = [pl.BlockSpec((tm, C_HEADS * LANE), tok)] * 2 + [
            pl.BlockSpec((C_HEADS * C_V, tm), lambda b, i: (0, b * nt + i))]
        qkv_shapes = [jax.ShapeDtypeStruct((n, C_HEADS * LANE), BF16)] * 2 + [
            jax.ShapeDtypeStruct((C_HEADS * C_V, n), BF16)]
    n_in = 14 + len(attn_w)
    n_out = 6 + len(qkv_specs)
    return pl.pallas_call(
        kern,
        grid=(nb, nt),
        in_specs=[
            pl.BlockSpec((tm, D_MODEL), tok),
            pl.BlockSpec((1, mod_rows, D_MODEL), lambda b, i: (b, 0, 0)),
            pl.BlockSpec((1, mod_rows, D_MODEL), lambda b, i: (b, 0, 0)),
            _resident((1, D_MODEL)),
            _resident_layer(w, layer),
            _resident_layer(wt, layer),
            _resident(ws.shape),
            _resident(bs.shape),
            _resident((1, BW)),
            _resident((1, BW)),
            _resident((1, C_QL)),
            _resident((1, C_KVL)),
            pl.BlockSpec((tm, LANE), lambda b, i: (i, 0)),
            pl.BlockSpec((tm, LANE), lambda b, i: (i, 0)),
        ] + [_resident_layer(a, layer) for a in attn_w] + [pl.BlockSpec(memory_space=pl.ANY)] * len(stacks),
        out_specs=[
            pl.BlockSpec((tm, BW), tok),
            pl.BlockSpec((None, 1, gv_rows, BW), lambda b, i: (layer, b, 0, 0)),
            pl.BlockSpec((tm, 4 * BW), tok),
            pl.BlockSpec((tm, 3 * D_MODEL), tok),
        ] + qkv_specs + [
            pl.BlockSpec((None, tm, C_KVL), tok_l),
            pl.BlockSpec((None, tm, C_ROPE), tok_l),
        ],
        out_shape=[
            jax.ShapeDtypeStruct((n, BW), BF16),
            jax.ShapeDtypeStruct((DEPTH, nb, gv_rows, BW), F32),
            jax.ShapeDtypeStruct((n, 4 * BW), F32),
            jax.ShapeDtypeStruct((n, 3 * D_MODEL), F32),
        ] + qkv_shapes + [
            jax.ShapeDtypeStruct((DEPTH, n, C_KVL), F32),
            jax.ShapeDtypeStruct((DEPTH, n, C_ROPE), F32),
        ],
        input_output_aliases={n_in: 1, n_in + 1: n_out - 2, n_in + 2: n_out - 1},
        compiler_params=_cparams("arbitrary", "arbitrary"),
        name="in_proj_s" if sample else "in_proj_p",
    )(x, sc, sh, g, w, wt, ws, bs, lng, lnb, qag, kvg, cos, sin, *attn_w, *stacks)


def _hgrn_consts(chunk, seg):
    t = np.arange(chunk)
    u = t[None, :]
    tt = t[:, None]
    same_seg = (t // seg)[:, None] == (t // seg)[None, :]
    blocks = [same_seg & (u <= tt), same_seg & (u > tt)]
    masks = [np.eye(chunk, dtype=bool)]
    b = seg // 2
    while b >= 1:
        pos = t % (2 * b)
        p = t - pos
        m = (p + b - 1)[:, None]
        second = pos >= b
        blocks.append(np.where(second[:, None], (u > m) & (u <= tt), (u > tt) & (u <= m)))
        masks.append((p[:, None] == p[None, :]) & second[:, None] & (~second)[None, :])
        b //= 2
    sums = np.concatenate(blocks, 0).astype(np.float32)
    return jnp.asarray(sums, BF16), jnp.asarray(np.stack(masks).astype(np.float32))


def _hgrn_lb(lbl_ref, layer):
    logits = lbl_ref[...]
    e = jnp.exp(logits - jnp.max(logits, axis=0, keepdims=True))
    p = e / jnp.sum(e, axis=0, keepdims=True)
    lb = jnp.zeros((1, HG_HEADS * HG_D), F32)
    for i in range(1, layer + 1):
        lb = lb + p[i:i + 1, :]
    return lb


def _hgrn_chunk(zb, lb, hg, sums_ref, masks_ref, sel, states, row_masks, chunk):
    nlev = masks_ref.shape[0] - 1
    q = _silu(zb[:, 0:BW])
    zf = zb[:, BW:2 * BW]
    vb = zb[:, 2 * BW:3 * BW]
    gate = _silu(zb[:, 3 * BW:4 * BW])
    log_lb = jnp.log(jnp.maximum(lb, LB_FLOOR))
    logf = _logaddexp(log_lb, jnp.log1p(-lb) - _softplus(-zf))
    k = (1.0 - lb) * _sigmoid(-zf)

    parts = _split_bf16(logf)
    sums = sums_ref[...]
    x = _dot(sums, parts[0]) + _dot(sums, parts[1])
    e = jnp.exp(x)
    e_cum = e[0:chunk]
    e_end = e[chunk:2 * chunk]
    nseq = len(states)

    outs = []
    new_states = [[None] * HG_HEADS for _ in range(nseq)]
    all_scores = []
    for h in range(HG_HEADS):
        hs = slice(h * HG_D, (h + 1) * HG_D)
        qh, kh = q[:, hs], k[:, hs]
        scores = masks_ref[0] * _dot_nt(_bf(qh), _bf(kh))
        for lev in range(nlev):
            el = e[(2 + lev) * chunk:(3 + lev) * chunk, hs]
            scores = scores + masks_ref[1 + lev] * _dot_nt(_bf(qh * el), _bf(kh * el))
        all_scores.append(_bf(scores))
    for h in range(HG_HEADS):
        hs = slice(h * HG_D, (h + 1) * HG_D)
        qh, kh, vh = q[:, hs], k[:, hs], _bf(vb[:, hs])
        o = _dot(all_scores[h], vh)
        q_in = _bf(qh * e_cum[:, hs])
        k_out = kh * e_end[:, hs]
        decay = jnp.exp(_dot_tn(parts[0][:, hs], sel) + _dot_tn(parts[1][:, hs], sel))
        for j in range(nseq):
            s_prev = states[j][h]
            if nseq == 1:
                o = o + _dot(q_in, _bf(s_prev))
                kj = _bf(k_out)
            else:
                o = o + row_masks[j] * _dot(q_in, _bf(s_prev))
                kj = _bf(k_out * row_masks[j])
            new_states[j][h] = decay[:, j * HG_D:(j + 1) * HG_D] * s_prev + _dot_tn(kj, vh)
        y = o * lax.rsqrt(jnp.mean(o * o, axis=-1, keepdims=True) + EPS) * hg[:, hs]
        outs.append(_bf(y * gate[:, hs]))
    return outs, new_states


def _hgrn_prompt_kernel(z_ref, lbl_ref, hg_ref, sums_ref, masks_ref, sel_ref, s0_ref, y_ref, sout_ref, state, *,
                        layer, tc, chunk):
    @pl.when(pl.program_id(1) == 0)
    def _():
        state[...] = s0_ref[0]

    lb = _hgrn_lb(lbl_ref, layer)
    hg = hg_ref[...]
    sel = sel_ref[...]
    for c in range(tc // chunk):
        rs = slice(c * chunk, (c + 1) * chunk)
        states = [[state[h] for h in range(HG_HEADS)]]
        outs, new_states = _hgrn_chunk(z_ref[rs, :], lb, hg, sums_ref, masks_ref, sel, states, None, chunk)
        for h in range(HG_HEADS):
            y_ref[rs, h * HG_D:(h + 1) * HG_D] = outs[h]
            state[h] = new_states[0][h]

    @pl.when(pl.program_id(1) == pl.num_programs(1) - 1)
    def _():
        sout_ref[0] = state[...]


def _hgrn_prompt(zb, lb_logits, hgrn_g, s0, *, layer, nb, tc, chunk):
    n = zb.shape[0]
    nt = n // (nb * tc)
    sums, masks = _hgrn_consts(chunk, chunk)
    sel = jnp.ones((chunk, HG_D), BF16)
    tok = lambda b, i: (b * nt + i, 0)
    kern = functools.partial(_hgrn_prompt_kernel, layer=layer, tc=tc, chunk=chunk)
    return pl.pallas_call(
        kern,
        grid=(nb, nt),
        in_specs=[
            pl.BlockSpec((tc, 4 * BW), tok),
            _resident(lb_logits.shape),
            _resident((1, BW)),
            _resident(sums.shape),
            _resident(masks.shape),
            _resident(sel.shape),
            pl.BlockSpec((1, HG_HEADS, HG_D, HG_D), lambda b, i: (b, 0, 0, 0)),
        ],
        out_specs=[
            pl.BlockSpec((tc, BW), tok),
            pl.BlockSpec((1, HG_HEADS, HG_D, HG_D), lambda b, i: (b, 0, 0, 0)),
        ],
        out_shape=[
            jax.ShapeDtypeStruct((n, BW), BF16),
            jax.ShapeDtypeStruct((nb, HG_HEADS, HG_D, HG_D), F32),
        ],
        scratch_shapes=[pltpu.VMEM((HG_HEADS, HG_D, HG_D), F32)],
        compiler_params=_cparams("arbitrary", "arbitrary"),
        name="hgrn_p",
    )(zb, lb_logits, hgrn_g, sums, masks, sel, s0)


def _hgrn_sample_kernel(z_ref, lbl_ref, hg_ref, sums_ref, masks_ref, sel_ref, s0_ref, *refs, layer, rows, seq):
    y_ref, sout_ref = refs[-2:]
    nseq = rows // seq
    lb = _hgrn_lb(lbl_ref, layer)
    ridx = lax.broadcasted_iota(jnp.int32, (rows, 1), 0)
    row_masks = [jnp.where((ridx >= j * seq) & (ridx < (j + 1) * seq), 1.0, 0.0) for j in range(nseq)]
    states = [[s0_ref[j, h] for h in range(HG_HEADS)] for j in range(nseq)]
    outs, new_states = _hgrn_chunk(z_ref[...], lb, hg_ref[...], sums_ref, masks_ref, sel_ref[...], states,
                                   row_masks, rows)
    for h in range(HG_HEADS):
        y_ref[:, h * HG_D:(h + 1) * HG_D] = outs[h]
        for j in range(nseq):
            sout_ref[j, h] = new_states[j][h]


def _hgrn_sample(zb, lb_logits, hgrn_g, s0_all, s_stack, *, layer, seq, rows):
    n = zb.shape[0]
    nseq = rows // seq
    sums, masks = _hgrn_consts(rows, seq)
    sel = np.zeros((rows, nseq * HG_D), np.float32)
    for j in range(nseq):
        sel[j * seq:(j + 1) * seq, j * HG_D:(j + 1) * HG_D] = 1.0
    sel = jnp.asarray(sel, BF16)
    kern = functools.partial(_hgrn_sample_kernel, layer=layer, rows=rows, seq=seq)
    state_spec = pl.BlockSpec((None, nseq, HG_HEADS, HG_D, HG_D), lambda i: (layer, i, 0, 0, 0))
    return pl.pallas_call(
        kern,
        grid=(n // rows,),
        in_specs=[
            pl.BlockSpec((rows, 4 * BW), lambda i: (i, 0)),
            _resident(lb_logits.shape),
            _resident((1, BW)),
            _resident(sums.shape),
            _resident(masks.shape),
            _resident(sel.shape),
            state_spec,
            pl.BlockSpec(memory_space=pl.ANY),
        ],
        out_specs=[pl.BlockSpec((rows, BW), lambda i: (i, 0)), state_spec],
        out_shape=[
            jax.ShapeDtypeStruct((n, BW), BF16),
            jax.ShapeDtypeStruct(s0_all.shape, F32),
        ],
        input_output_aliases={7: 1},
        compiler_params=_cparams("arbitrary"),
        name="hgrn_s",
    )(zb, lb_logits, hgrn_g, sums, masks, sel, s0_all, s_stack)


def _attn_prompt_kernel(qi_ref, kj_ref, q_ref, k_ref, v_ref, o_ref, m_sc, l_sc, acc_sc, *, tb):
    step = pl.program_id(1)
    i = qi_ref[step]
    j = kj_ref[step]

    @pl.when(j == 0)
    def _():
        m_sc[...] = jnp.full(m_sc.shape, NEG_BIG, F32)
        l_sc[...] = jnp.zeros(l_sc.shape, F32)
        acc_sc[...] = jnp.zeros(acc_sc.shape, F32)

    def update(diagonal):
        if diagonal:
            keep = (lax.broadcasted_iota(jnp.int32, (tb, tb), 0) <= lax.broadcasted_iota(jnp.int32, (tb, tb), 1))
        sts = []
        for h in range(C_HEADS):
            hs = slice(h * LANE, (h + 1) * LANE)
            st = _dot_nt(k_ref[:, hs], q_ref[:, hs])
            sts.append(jnp.where(keep, st, NEG_BIG) if diagonal else st)
        for h in range(C_HEADS):
            st = sts[h]
            m_prev = m_sc[h]
            m_new = jnp.maximum(m_prev, jnp.max(st, axis=0, keepdims=True))
            alpha = jnp.exp2(m_prev - m_new)
            p = jnp.exp2(st - m_new)
            l_sc[h] = alpha * l_sc[h] + jnp.sum(p, axis=0, keepdims=True)
            acc_sc[h] = alpha * acc_sc[h] + _dot(v_ref[h * C_V:(h + 1) * C_V, :], _bf(p))
            m_sc[h] = m_new

    @pl.when(j < i)
    def _():
        update(False)

    @pl.when(j == i)
    def _():
        update(True)
        for h in range(C_HEADS):
            o_ref[h * C_V:(h + 1) * C_V, :] = _bf(acc_sc[h] / l_sc[h])


def _attn_prompt(q, k, vt, *, nb, t, tb):
    n = q.shape[0]
    nblk = t // tb
    pairs = [(i, j) for i in range(nblk) for j in range(i + 1)]
    qi = jnp.asarray([p[0] for p in pairs], jnp.int32)
    kj = jnp.asarray([p[1] for p in pairs], jnp.int32)
    q_map = lambda b, s, qi, kj: (b * nblk + qi[s], 0)
    k_map = lambda b, s, qi, kj: (b * nblk + kj[s], 0)
    grid_spec = pltpu.PrefetchScalarGridSpec(
        num_scalar_prefetch=2,
        grid=(nb, len(pairs)),
        in_specs=[
            pl.BlockSpec((tb, C_HEADS * LANE), q_map),
            pl.BlockSpec((tb, C_HEADS * LANE), k_map),
            pl.BlockSpec((C_HEADS * C_V, tb), lambda b, s, qi, kj: (0, b * nblk + kj[s])),
        ],
        out_specs=pl.BlockSpec((C_HEADS * C_V, tb), lambda b, s, qi, kj: (0, b * nblk + qi[s])),
        scratch_shapes=[
            pltpu.VMEM((C_HEADS, 1, tb), F32),
            pltpu.VMEM((C_HEADS, 1, tb), F32),
            pltpu.VMEM((C_HEADS, C_V, tb), F32),
        ],
    )
    return pl.pallas_call(
        functools.partial(_attn_prompt_kernel, tb=tb),
        grid_spec=grid_spec,
        out_shape=jax.ShapeDtypeStruct((C_HEADS * C_V, n), BF16),
        compiler_params=_cparams("arbitrary", "arbitrary"),
        name="attn_p",
    )(qi, kj, q, k, vt)


def _attn_sample_kernel(pt_ref, q_ref, kn_ref, lat_hbm, kr_hbm, o_ref, lat_buf, kr_buf, sem, *,
                        layer, pages, seq, key_chunk):
    b = pl.program_id(0)
    nb = pl.num_programs(0)
    slot = lax.rem(b, 2)
    rows = C_HEADS * seq

    def page_copies(seq_idx, slot_, with_source):
        copies = []
        for p in range(pages):
            page = pt_ref[seq_idx, p] if with_source else 0
            ks = pl.ds(p * PAGE, PAGE)
            copies.append(pltpu.make_async_copy(lat_hbm.at[layer, page], lat_buf.at[slot_, ks, :], sem.at[0, slot_]))
            copies.append(pltpu.make_async_copy(kr_hbm.at[layer, page], kr_buf.at[slot_, :, ks], sem.at[1, slot_]))
        return copies

    @pl.when(b == 0)
    def _():
        for c in page_copies(0, 0, True):
            c.start()

    for c in page_copies(b, slot, False):
        c.wait()
    nxt = jnp.minimum(b + 1, nb - 1)
    for c in page_copies(nxt, 1 - slot, True):
        c.start()

    q = q_ref[0]
    q_lat = q[:, :C_KVL]
    q_rope = q[:, C_KVL + ROPE_LANE:C_KVL + ROPE_LANE + C_ROPE]
    n_chunks = pages * PAGE // key_chunk
    lats, s = [], []
    for c in range(n_chunks):
        ks = pl.ds(c * key_chunk, key_chunk)
        lats.append(_bf(lat_buf[slot, ks, :]))
        s.append(_dot_nt(q_lat, lats[c]) + _dot(q_rope, _bf(kr_buf[slot, :, ks])))

    qf = q.astype(F32)
    kn = kn_ref[0].astype(F32)
    tpos = lax.broadcasted_iota(jnp.int32, (rows, 1), 0) % seq
    sn = [jnp.where(tpos >= t, jnp.sum(qf * kn[t:t + 1, :], axis=-1, keepdims=True), NEG_BIG) for t in range(seq)]

    m = jnp.max(s[0], axis=-1, keepdims=True)
    for c in range(1, n_chunks):
        m = jnp.maximum(m, jnp.max(s[c], axis=-1, keepdims=True))
    for t in range(seq):
        m = jnp.maximum(m, sn[t])

    l = jnp.zeros((rows, 1), F32)
    acc = jnp.zeros((rows, C_KVL), F32)
    for c in range(n_chunks):
        e = jnp.exp(s[c] - m)
        l = l + jnp.sum(e, axis=-1, keepdims=True)
        acc = acc + _dot(_bf(e), lats[c])
    for t in range(seq):
        e = jnp.exp(sn[t] - m)
        l = l + e
        acc = acc + e * kn[t:t + 1, :C_KVL]
    o_ref[0] = _bf(acc / l)

    @pl.when(b == nb - 1)
    def _():
        for c in page_copies(b, 1 - slot, False):
            c.wait()


def _attn_sample(page_table, q, k_new, cache_latent, cache_krope_t, *, layer, seq):
    nb, pages = page_table.shape
    rows = C_HEADS * seq
    keys = pages * PAGE
    kern = functools.partial(_attn_sample_kernel, layer=layer, pages=pages, seq=seq, key_chunk=min(keys, 2048))
    grid_spec = pltpu.PrefetchScalarGridSpec(
        num_scalar_prefetch=1,
        grid=(nb,),
        in_specs=[
            pl.BlockSpec((1, rows, QK_W), lambda b, pt: (b, 0, 0)),
            pl.BlockSpec((1, seq, QK_W), lambda b, pt: (b, 0, 0)),
            pl.BlockSpec(memory_space=pl.ANY),
            pl.BlockSpec(memory_space=pl.ANY),
        ],
        out_specs=pl.BlockSpec((1, rows, C_KVL), lambda b, pt: (b, 0, 0)),
        scratch_shapes=[
            pltpu.VMEM((2, keys, C_KVL), F32),
            pltpu.VMEM((2, C_ROPE, keys), F32),
            pltpu.SemaphoreType.DMA((2, 2)),
        ],
    )
    return pl.pallas_call(
        kern,
        grid_spec=grid_spec,
        out_shape=jax.ShapeDtypeStruct((nb, rows, C_KVL), BF16),
        compiler_params=_cparams("arbitrary"),
        name="attn_s",
    )(page_table, q, k_new, cache_latent, cache_krope_t)


def _merge_kernel(ya_ref, yb_ref, oc_ref, gt_ref, x_ref, g1_ref, wb_ref, wo_ref, *refs, latent_out):
    if latent_out:
        wv_ref, o_ref = refs
        yc = _bf(_dot(oc_ref[...], wv_ref[...]))
        br_c = _dot(yc, wb_ref[2])
    else:
        (o_ref,) = refs
        br_c = _dot_tn(oc_ref[...], wb_ref[2])
    m = gt_ref[:, 0:D_MODEL] * _dot(ya_ref[...], wb_ref[0])
    m = m + gt_ref[:, D_MODEL:2 * D_MODEL] * _dot(yb_ref[...], wb_ref[1])
    m = m + gt_ref[:, 2 * D_MODEL:3 * D_MODEL] * br_c
    o_ref[...] = x_ref[...] + g1_ref[0] * _dot(_bf(m), wo_ref[...])


def _merge(ya, yb, oc, gt, x, g1, wb, wo, wv, *, layer, nb, tm):
    n = x.shape[0]
    nt = n // (nb * tm)
    mod_rows = g1.shape[1]
    tok = lambda b, i: (b * nt + i, 0)
    extra = [] if wv is None else [wv]
    return pl.pallas_call(
        functools.partial(_merge_kernel, latent_out=wv is not None),
        grid=(nb, nt),
        in_specs=[
            pl.BlockSpec((tm, BW), tok),
            pl.BlockSpec((tm, BW), tok),
            (pl.BlockSpec((BW, tm), lambda b, i: (0, b * nt + i)) if wv is None
             else pl.BlockSpec((tm, oc.shape[1]), tok)),
            pl.BlockSpec((tm, 3 * D_MODEL), tok),
            pl.BlockSpec((tm, D_MODEL), tok),
            pl.BlockSpec((1, mod_rows, D_MODEL), lambda b, i: (b, 0, 0)),
            _resident_layer(wb, layer),
            _resident_layer(wo, layer),
        ] + [_resident_layer(a, layer) for a in extra],
        out_specs=pl.BlockSpec((tm, D_MODEL), tok),
        out_shape=jax.ShapeDtypeStruct((n, D_MODEL), F32),
        compiler_params=_cparams("arbitrary", "arbitrary"),
        name="merge",
    )(ya, yb, oc, gt, x, g1, wb, wo, *extra)


def _ffn_up_prompt_kernel(x_ref, sc_ref, sh_ref, g_ref, w_ref, cw_ref, cb_ref, c0_ref, act_ref, cs_ref, stage, *, tm):
    @pl.when(pl.program_id(1) == 0)
    def _():
        stage[0:8, :] = c0_ref[0]

    hb = _bf(_norm_mod(x_ref[...], g_ref[...], sc_ref[0], sh_ref[0]))

    def conv(cols):
        up = _dot(hb, w_ref[:, cols])
        stage[8:8 + tm, cols] = up
        y = (cb_ref[:, cols] + cw_ref[0:1, cols] * stage[6:6 + tm, cols] + cw_ref[1:2, cols] * stage[7:7 + tm, cols]
             + cw_ref[2:3, cols] * up)
        stage[0:8, cols] = up[tm - 8:, :]
        return y

    for c in range(N_FF_CHUNKS):
        ca = slice(c * FF_COL, (c + 1) * FF_COL)
        cb = slice(D_FF + c * FF_COL, D_FF + (c + 1) * FF_COL)
        act_ref[:, ca] = _bf(_silu(conv(ca)) * conv(cb))

    @pl.when(pl.program_id(1) == pl.num_programs(1) - 1)
    def _():
        cs_ref[0] = stage[0:8, :]


def _ffn_up_prompt(x, sc, sh, g, w_up, conv_w, conv_b, conv0, *, layer, nb, tm):
    n = x.shape[0]
    nt = n // (nb * tm)
    tok = lambda b, i: (b * nt + i, 0)
    kern = functools.partial(_ffn_up_prompt_kernel, tm=tm)
    return pl.pallas_call(
        kern,
        grid=(nb, nt),
        in_specs=[
            pl.BlockSpec((tm, D_MODEL), tok),
            pl.BlockSpec((1, 1, D_MODEL), lambda b, i: (b, 0, 0)),
            pl.BlockSpec((1, 1, D_MODEL), lambda b, i: (b, 0, 0)),
            _resident((1, D_MODEL)),
            _resident_layer(w_up, layer),
            _resident((3, 2 * D_FF)),
            _resident((1, 2 * D_FF)),
            pl.BlockSpec((1, 8, 2 * D_FF), lambda b, i: (b, 0, 0)),
        ],
        out_specs=[
            pl.BlockSpec((tm, D_FF), tok),
            pl.BlockSpec((1, 8, 2 * D_FF), lambda b, i: (b, 0, 0)),
        ],
        out_shape=[
            jax.ShapeDtypeStruct((n, D_FF), BF16),
            jax.ShapeDtypeStruct((nb, 8, 2 * D_FF), F32),
        ],
        scratch_shapes=[pltpu.VMEM((8 + tm, 2 * D_FF), F32)],
        compiler_params=_cparams("arbitrary", "arbitrary"),
        name="ffn_up_p",
    )(x, sc, sh, g, w_up, conv_w, conv_b, conv0)


def _pick_rows(sel, a):
    hi = _bf(a)
    r1 = a - hi.astype(F32)
    mid = _bf(r1)
    lo = _bf(r1 - mid.astype(F32))
    return _dot(sel, hi) + _dot(sel, mid) + _dot(sel, lo)


def _ffn_up_sample_kernel(x_ref, sc_ref, sh_ref, g_ref, wa_ref, wb_ref, cwa_ref, cwb_ref, cba_ref, cbb_ref,
                          c0a_ref, c0b_ref, selp_ref, selo_ref, act_ref, csa_ref, csb_ref, *, n, seq):
    hb = _bf(_norm_mod(x_ref[...], g_ref[...], sc_ref[0], sh_ref[0]))
    tpos = lax.broadcasted_iota(jnp.int32, (n, 1), 0) % seq

    def conv(up, c0, cw, cb):
        s1 = jnp.where(tpos >= 1, pltpu.roll(up, 1, 0), _pick_rows(selp_ref[0], c0))
        s2 = jnp.where(tpos >= 2, pltpu.roll(up, 2, 0), _pick_rows(selp_ref[1], c0))
        return cb + cw[0:1, :] * s2 + cw[1:2, :] * s1 + cw[2:3, :] * up

    up_a = _dot(hb, wa_ref[...])
    up_b = _dot(hb, wb_ref[...])
    act_ref[...] = _bf(_silu(conv(up_a, c0a_ref[...], cwa_ref[...], cba_ref[...]))
                       * conv(up_b, c0b_ref[...], cwb_ref[...], cbb_ref[...]))
    csa_ref[...] = _pick_rows(selo_ref[...], up_a)
    csb_ref[...] = _pick_rows(selo_ref[...], up_b)


def _ffn_up_sample(x, sc, sh, g, w_up, conv_w, conv_b, c0, *, layer, seq):
    n = x.shape[0]
    nseq = n // seq
    t = np.arange(n)
    selp = np.zeros((2, n, 2 * nseq), np.float32)
    first, second = t[t % seq == 0], t[t % seq == 1]
    selp[0, first, 2 * (first // seq) + 1] = 1.0
    selp[1, first, 2 * (first // seq)] = 1.0
    selp[1, second, 2 * (second // seq) + 1] = 1.0
    selo = np.zeros((2 * nseq, n), np.float32)
    for j in range(2):
        selo[2 * np.arange(nseq) + j, np.arange(nseq) * seq + seq - 2 + j] = 1.0
    selp, selo = jnp.asarray(selp, BF16), jnp.asarray(selo, BF16)
    col_a = lambda j: (0, j)
    col_b = lambda j: (0, N_FF_CHUNKS + j)
    kern = functools.partial(_ffn_up_sample_kernel, n=n, seq=seq)
    return pl.pallas_call(
        kern,
        grid=(N_FF_CHUNKS,),
        in_specs=[
            _resident((n, D_MODEL)),
            _resident((1, n, D_MODEL)),
            _resident((1, n, D_MODEL)),
            _resident((1, D_MODEL)),
            pl.BlockSpec((None, D_MODEL, FF_COL), lambda j: (layer, 0, j)),
            pl.BlockSpec((None, D_MODEL, FF_COL), lambda j: (layer, 0, N_FF_CHUNKS + j)),
            pl.BlockSpec((3, FF_COL), col_a),
            pl.BlockSpec((3, FF_COL), col_b),
            pl.BlockSpec((1, FF_COL), col_a),
            pl.BlockSpec((1, FF_COL), col_b),
            pl.BlockSpec((None, 2 * nseq, FF_COL), lambda j: (layer, 0, j)),
            pl.BlockSpec((None, 2 * nseq, FF_COL), lambda j: (layer, 0, N_FF_CHUNKS + j)),
            _resident(selp.shape),
            _resident(selo.shape),
        ],
        out_specs=[
            pl.BlockSpec((n, FF_COL), col_a),
            pl.BlockSpec((2 * nseq, FF_COL), col_a),
            pl.BlockSpec((2 * nseq, FF_COL), col_a),
        ],
        out_shape=[
            jax.ShapeDtypeStruct((n, D_FF), BF16),
            jax.ShapeDtypeStruct((2 * nseq, D_FF), F32),
            jax.ShapeDtypeStruct((2 * nseq, D_FF), F32),
        ],
        compiler_params=_cparams("arbitrary"),
        name="ffn_up_s",
    )(x, sc, sh, g, w_up, w_up, conv_w, conv_w, conv_b, conv_b, c0, c0, selp, selo)


def _ffn_down_kernel(act_ref, x_ref, g2_ref, w_ref, fg_ref, o_ref, *, final):
    xn = x_ref[...] + g2_ref[0] * _dot(act_ref[...], w_ref[...])
    o_ref[...] = _rms(xn, fg_ref[...]) if final else xn


def _ffn_down(act, x, g2, w_down, final_g, *, layer, nb, tm, final):
    n = x.shape[0]
    nt = n // (nb * tm)
    mod_rows = g2.shape[1]
    tok = lambda b, i: (b * nt + i, 0)
    return pl.pallas_call(
        functools.partial(_ffn_down_kernel, final=final),
        grid=(nb, nt),
        in_specs=[
            pl.BlockSpec((tm, D_FF), tok),
            pl.BlockSpec((tm, D_MODEL), tok),
            pl.BlockSpec((1, mod_rows, D_MODEL), lambda b, i: (b, 0, 0)),
            _resident_layer(w_down, layer),
            _resident((1, D_MODEL)),
        ],
        out_specs=pl.BlockSpec((tm, D_MODEL), tok),
        out_shape=jax.ShapeDtypeStruct((n, D_MODEL), F32),
        compiler_params=_cparams("arbitrary", "arbitrary"),
        name="ffn_down",
    )(act, x, g2, w_down, final_g)


def _rope_tables(pos):
    half = C_ROPE // 2
    inv = ROPE_THETA ** (-jnp.arange(half, dtype=F32) / half)
    ang = pos.astype(F32)[:, None] * inv[None, :]
    cos, sin = jnp.cos(ang), jnp.sin(ang)
    return _rope_block(jnp.concatenate([cos, cos], -1)), _rope_block(jnp.concatenate([-sin, sin], -1))


def _rope_block(a):
    lead = a.shape[:-1]
    return jnp.concatenate([jnp.zeros(lead + (ROPE_LANE,), a.dtype), a,
                            jnp.zeros(lead + (LANE - ROPE_LANE - C_ROPE,), a.dtype)], axis=-1)


def _swap_halves(w):
    half = C_ROPE // 2
    return jnp.concatenate([w[..., half:], w[..., :half]], axis=-1)


def _matmul_weights(w_in, w_uq, w_ukv, w_branch, w_o, w_up, w_down):
    krr = w_in[..., W_HEAD_COLS:W_HEAD_COLS + C_ROPE]
    w_head = _bf(w_in[..., :W_HEAD_COLS])
    w_tail = _bf(jnp.concatenate([_rope_block(krr), _rope_block(_swap_halves(krr)),
                                  w_in[..., W_HEAD_COLS + C_ROPE:]], axis=-1))

    def head_blocks(a):
        pad = jnp.zeros(a.shape[:-1] + (LANE - a.shape[-1],), F32)
        return jnp.concatenate([a, pad], -1).reshape(a.shape[:2] + (C_HEADS * LANE,))

    wq = w_uq.reshape(DEPTH, C_QL, C_HEADS, C_NOPE + C_ROPE)
    q_nope, q_rope = wq[..., :C_NOPE], wq[..., C_NOPE:]
    wqn = head_blocks(q_nope)
    wqa = _rope_block(q_rope).reshape(DEPTH, C_QL, C_HEADS * LANE)
    wqb = _rope_block(_swap_halves(q_rope)).reshape(DEPTH, C_QL, C_HEADS * LANE)

    wkv = w_ukv.reshape(DEPTH, C_KVL, C_HEADS, C_NOPE + C_V)
    wk = jnp.transpose(wkv[..., :C_NOPE], (0, 2, 3, 1))
    wk = jnp.concatenate([wk, jnp.zeros((DEPTH, C_HEADS, LANE - C_NOPE, C_KVL), F32)], axis=2)
    wv = jnp.transpose(wkv[..., C_NOPE:], (0, 2, 1, 3))
    eye = jnp.eye(C_HEADS, dtype=F32)
    wv_bd = (wv[:, :, :, None, :] * eye[None, :, None, :, None]).reshape(DEPTH, C_HEADS * C_KVL, C_HEADS * C_V)
    attn_sample = [_bf(wqn), _bf(wqa), _bf(wqb), _bf(wk)]
    wv_t = jnp.transpose(wkv[..., C_NOPE:], (0, 2, 3, 1)).reshape(DEPTH, C_HEADS * C_V, C_KVL)
    attn_prompt = [_bf(wqn + wqa), _bf(wqb), _bf(head_blocks(wkv[..., :C_NOPE])), _bf(wv_t)]
    return dict(w_head=w_head, w_tail=w_tail, attn_s=attn_sample, attn_p=attn_prompt, wv=_bf(wv_bd), wb=_bf(w_branch), wo=_bf(w_o),
                w_up=_bf(w_up), w_down=_bf(w_down))


def _sample_mix_tables(w_s_l, b_s_l, seq):
    t = np.arange(8) % seq
    coef = []
    for j in range(seq):
        src = t - j
        wj = w_s_l[:, t, np.maximum(src, 0)]
        wj = jnp.where(jnp.asarray(src >= 0)[None, :], wj, 0.0)
        coef.append(jnp.repeat(wj.T, LANE, axis=1))
    bias = jnp.repeat(b_s_l[:, t].T, LANE, axis=1)
    return jnp.stack(coef), bias


def _trunk(x, mods, lw, *, sample, nb, t, pos, s0, conv0, cache_latent, cache_krope, page_table,
           w_s, b_s, lnv_g, lnv_b, lb_logits, hgrn_g, q_a_g, kv_a_g, norm1_g, norm2_g, final_g, conv_w, conv_b):
    n = nb * t
    cos, sin = _rope_tables(pos)
    w = lw
    if sample:
        conv0 = conv0.reshape(DEPTH, nb * 2, 2 * D_FF)
    gv_rows = n if sample else A_CHUNK
    stacks = (jnp.zeros((DEPTH, 1 if sample else nb, gv_rows, BW), F32), jnp.zeros((DEPTH, n, C_KVL), F32),
              jnp.zeros((DEPTH, n, C_ROPE), F32))
    s_stack = jnp.zeros(s0.shape, F32) if sample else None
    s_out, conv_out = [], []
    for l in range(DEPTH):
        sh1, sc1, g1, sh2, sc2, g2 = mods[l]
        row = lambda a: a[l].reshape(1, -1)
        if sample:
            ws, bs = _sample_mix_tables(w_s[l], b_s[l], t)
            kw = dict(nb=1, tm=n, gv_rows=n, sample=True)
        else:
            ws, bs = w_s[l], jnp.repeat(b_s[l].T, LANE, axis=1)
            kw = dict(nb=nb, tm=512, gv_rows=A_CHUNK, sample=False)
        outs = _in_proj(
            x, sc1, sh1, row(norm1_g), w["w_head"], w["w_tail"], ws, bs, row(lnv_g), row(lnv_b), row(q_a_g), row(kv_a_g),
            cos, sin, w["attn_s"] if sample else w["attn_p"], stacks, layer=l, **kw)
        stacks = (outs[1], outs[-2], outs[-1])
        if sample:
            ya, _, zb, gt, q, k, _, _ = outs
            yb, s_stack = _hgrn_sample(zb, lb_logits, row(hgrn_g), s0, s_stack, layer=l, seq=t, rows=64)
            q_b = q.reshape(nb, t, C_HEADS, QK_W).transpose(0, 2, 1, 3).reshape(nb, C_HEADS * t, QK_W)
            oc = _attn_sample(page_table, q_b, k.reshape(nb, t, QK_W), cache_latent, cache_krope, layer=l, seq=t)
            oc = oc.reshape(nb, C_HEADS, t, C_KVL).transpose(0, 2, 1, 3).reshape(n, C_HEADS * C_KVL)
            mkw = dict(nb=1, tm=n)
            wv = w["wv"]
        else:
            ya, _, zb, gt, q, k, v, _, _ = outs
            yb, s_new = _hgrn_prompt(zb, lb_logits, row(hgrn_g), s0[l], layer=l, nb=nb, tc=512, chunk=128)
            s_out.append(s_new)
            oc = _attn_prompt(q, k, v, nb=nb, t=t, tb=512)
            mkw = dict(nb=nb, tm=512)
            wv = None
        x = _merge(ya, yb, oc, gt, x, g1, w["wb"], w["wo"], wv, layer=l, **mkw)
        if sample:
            act, cs_a, cs_b = _ffn_up_sample(x, sc2, sh2, row(norm2_g), w["w_up"], conv_w[l], row(conv_b), conv0,
                                             layer=l, seq=t)
            conv_out.append(jnp.concatenate([cs_a, cs_b], axis=-1).reshape(nb, 2, 2 * D_FF))
        else:
            c0 = jnp.concatenate([jnp.zeros((nb, 6, 2 * D_FF), F32), conv0[l]], axis=1)
            act, cs = _ffn_up_prompt(x, sc2, sh2, row(norm2_g), w["w_up"], conv_w[l], row(conv_b), c0,
                                     layer=l, nb=nb, tm=512)
            conv_out.append(cs[:, 6:, :])
        x = _ffn_down(act, x, g2, w["w_down"], final_g.reshape(1, -1), layer=l, final=(l == DEPTH - 1), **mkw)
    gv, lat, kr = stacks
    states = s_stack if sample else jnp.stack(s_out)
    return (x.reshape(nb, t, D_MODEL), lat.reshape(DEPTH, nb, t, C_KVL), kr.reshape(DEPTH, nb, t, C_ROPE),
            gv.reshape(DEPTH, nb, -1, BW), states, jnp.stack(conv_out))


def kernel(x_prompt, x_sample, c_prompt, c_sample, cache_latent, cache_krope, state_hgrn, state_conv, page_table,
           norm1_g, norm2_g, final_g, w_ada, b_ada, w_in, w_s, b_s, lnv_g, lnv_b, lb_logits, hgrn_g, q_a_g, kv_a_g,
           w_uq, w_ukv, w_branch, w_o, w_up, conv_w, conv_b, w_down):
    bp, tp, _ = x_prompt.shape
    bs_, ts, _ = x_sample.shape
    past = page_table.shape[1] * cache_latent.shape[2]

    rows = bp + bs_
    rows_pad = -(-rows // 8) * 8
    c_all = jnp.concatenate([c_prompt, c_sample, jnp.zeros((rows_pad - rows, D_MODEL), F32)], axis=0)
    mod = _ada(c_all, w_ada, b_ada)
    mods_p, mods_s = [], []
    for l in range(DEPTH):
        chunks = [mod[l, :, i * D_MODEL:(i + 1) * D_MODEL] for i in range(6)]
        mods_p.append([m[:bp].reshape(bp, 1, D_MODEL) for m in chunks])
        mods_s.append([jnp.repeat(m[bp:rows], ts, axis=0).reshape(1, bs_ * ts, D_MODEL) for m in chunks])

    lw = _matmul_weights(w_in, w_uq, w_ukv, w_branch, w_o, w_up, w_down)
    shared = dict(w_s=w_s, b_s=b_s, lnv_g=lnv_g, lnv_b=lnv_b, lb_logits=lb_logits, hgrn_g=hgrn_g, q_a_g=q_a_g,
                  kv_a_g=kv_a_g, norm1_g=norm1_g, norm2_g=norm2_g, final_g=final_g, conv_w=conv_w, conv_b=conv_b)

    s0_p = jnp.zeros((DEPTH, bp, HG_HEADS, HG_D, HG_D), F32)
    conv0_p = jnp.zeros((DEPTH, bp, 2, 2 * D_FF), F32)
    out_p = _trunk(x_prompt.reshape(bp * tp, D_MODEL), mods_p, lw, sample=False, nb=bp, t=tp, pos=jnp.arange(tp),
                   s0=s0_p, conv0=conv0_p, cache_latent=None, cache_krope=None, page_table=None, **shared)
    pos_s = jnp.tile(past + jnp.arange(ts), bs_)
    out_s = _trunk(x_sample.reshape(bs_ * ts, D_MODEL), mods_s, lw, sample=True, nb=bs_, t=ts, pos=pos_s,
                   s0=state_hgrn, conv0=state_conv, cache_latent=cache_latent,
                   cache_krope=jnp.swapaxes(cache_krope, 2, 3),
                   page_table=page_table, **shared)
    y_p, lat_p, kr_p, gv_p, hgrn_p, conv_p = out_p
    y_s, lat_s, kr_s, gv_s, hgrn_s, conv_s = out_s
    return (y_p, y_s, lat_p, kr_p, gv_p, hgrn_p, conv_p, lat_s, kr_s, gv_s, hgrn_s, conv_s)
```
